```python
import math
import jax, jax.numpy as jnp
from jax import lax
import numpy as np

D_MODEL = 2048
BATCH = 2
SEQ = 4096
DEPTH = 1

CHUNK = 64
Q_BLOCK = 128
EPS = 1e-6
ROPE_THETA = 10000.0

RET_HEADS = 8
RET_QK_DIM = 128
RET_V_DIM = 256
RET_QK_W = RET_HEADS * RET_QK_DIM
RET_V_W = RET_HEADS * RET_V_DIM

MLA_HEADS = 16
Q_LORA = 512
KV_LORA = 512
QK_NOPE = 128
QK_ROPE = 64
V_HEAD = 128
MLA_QK_DIM = QK_NOPE + QK_ROPE
MLA_V_W = MLA_HEADS * V_HEAD

D_FF = 4 * D_MODEL

IN_SPLITS = (RET_QK_W, RET_QK_W, RET_V_W, RET_V_W, Q_LORA, KV_LORA, QK_ROPE, D_MODEL, D_MODEL)
N_IN = sum(IN_SPLITS)

kernel_name = "hybrid_retention_mla_gated_encoder"


def _rmsnorm(x, g):
    x32 = x.astype(jnp.float32)
    y = x32 * lax.rsqrt(jnp.mean(x32 * x32, axis=-1, keepdims=True) + EPS)
    return (y * g.astype(jnp.float32)).astype(x.dtype)


def _rope(t, positions):
    half = t.shape[-1] // 2
    inv = ROPE_THETA ** (-jnp.arange(half, dtype=jnp.float32) / half)
    ang = positions.astype(jnp.float32)[..., None] * inv
    cos = jnp.cos(ang)[:, :, None, :]
    sin = jnp.sin(ang)[:, :, None, :]
    t32 = t.astype(jnp.float32)
    t1, t2 = t32[..., :half], t32[..., half:]
    out = jnp.concatenate([t1 * cos - t2 * sin, t2 * cos + t1 * sin], axis=-1)
    return out.astype(t.dtype)


def _retention(q, k, v):
    B, S, H, dk = q.shape
    dv = v.shape[-1]
    nc = S // CHUNK
    log_gamma = jnp.log(1.0 - 2.0 ** (-5.0 - jnp.arange(H, dtype=jnp.float32)))

    def to_chunks(t):
        return t.reshape(B, nc, CHUNK, H, t.shape[-1]).transpose(1, 0, 3, 2, 4)

    pos = jnp.arange(CHUNK, dtype=jnp.float32)
    lg = log_gamma[:, None]
    intra = jnp.exp(lg[..., None] * jnp.abs(pos[:, None] - pos[None, :]))
    q_dec = jnp.exp(lg * (pos + 1.0))
    k_dec = jnp.exp(lg * (CHUNK - 1.0 - pos))
    c_dec = jnp.exp(log_gamma * CHUNK)

    def step(state, inp):
        qi, ki, vi = inp
        s = jnp.einsum('bhnd,bhmd->bhnm', qi, ki) * intra
        o = (jnp.einsum('bhnm,bhme->bhne', s, vi)
             + jnp.einsum('bhnd,bhde->bhne', qi * q_dec[..., None], state))
        state = state * c_dec[:, None, None] + jnp.einsum(
            'bhmd,bhme->bhde', ki * k_dec[..., None], vi)
        return state, o

    state0 = jnp.zeros((B, H, dk, dv), jnp.float32)
    _, o = lax.scan(step, state0, (to_chunks(q), to_chunks(k), to_chunks(v)))
    return o.transpose(1, 0, 3, 2, 4).reshape(B, S, H, dv)


def _block_causal_attention(q, k, v, scale):
    B, S, H, dqk = q.shape
    nb = S // Q_BLOCK
    qb = q.reshape(B, nb, Q_BLOCK, H, dqk).transpose(1, 0, 2, 3, 4)
    key_chunk = jnp.arange(S) // CHUNK

    def one(args):
        qi, bi = args
        q_chunk = (bi * Q_BLOCK + jnp.arange(Q_BLOCK)) // CHUNK
        mask = key_chunk[None, :] <= q_chunk[:, None]
        s = jnp.einsum('bqhd,bkhd->bhqk', qi, k,
                       preferred_element_type=jnp.float32) * scale
        s = jnp.where(mask, s, -jnp.inf)
        p = jax.nn.softmax(s, axis=-1).astype(v.dtype)
        return jnp.einsum('bhqk,bkhe->bqhe', p, v)

    o = lax.map(one, (qb, jnp.arange(nb)))
    return o.transpose(1, 0, 2, 3, 4).reshape(B, S, H, v.shape[-1])


def _mixer(u, positions, w_in, ret_norm_g, w_ret_o, q_a_norm_g, w_q_b,
           kv_a_norm_g, w_kv_b, w_mla_o, w_out):
    B, S, _ = u.shape
    proj = u @ w_in
    offs = []
    acc = 0
    for w in IN_SPLITS[:-1]:
        acc += w
        offs.append(acc)
    (r_q, r_k, r_v, r_g, c_q, c_kv, k_pe, g_ret, g_mla) = jnp.split(proj, offs, axis=-1)

    rq = _rope(r_q.reshape(B, S, RET_HEADS, RET_QK_DIM), positions).astype(jnp.float32)
    rk = _rope(r_k.reshape(B, S, RET_HEADS, RET_QK_DIM), positions).astype(jnp.float32)
    rk = rk * (RET_QK_DIM ** -0.5)
    rv = r_v.reshape(B, S, RET_HEADS, RET_V_DIM).astype(jnp.float32)
    ry = _retention(rq, rk, rv)
    mu = jnp.mean(ry, axis=-1, keepdims=True)
    var = jnp.mean(jnp.square(ry - mu), axis=-1, keepdims=True)
    ry = ((ry - mu) * lax.rsqrt(var + EPS)).reshape(B, S, RET_V_W) * ret_norm_g.astype(jnp.float32)
    ry = ry.astype(u.dtype) * jax.nn.silu(r_g)
    y_ret = ry @ w_ret_o

    q = (_rmsnorm(c_q, q_a_norm_g) @ w_q_b).reshape(B, S, MLA_HEADS, MLA_QK_DIM)
    q_nope, q_pe = q[..., :QK_NOPE], _rope(q[..., QK_NOPE:], positions)
    kv = (_rmsnorm(c_kv, kv_a_norm_g) @ w_kv_b).reshape(B, S, MLA_HEADS, QK_NOPE + V_HEAD)
    k_nope, v = kv[..., :QK_NOPE], kv[..., QK_NOPE:]
    k_pe = _rope(k_pe.reshape(B, S, 1, QK_ROPE), positions)
    qf = jnp.concatenate([q_nope, q_pe], axis=-1)
    kf = jnp.concatenate([k_nope, jnp.broadcast_to(k_pe, (B, S, MLA_HEADS, QK_ROPE))], axis=-1)
    my = _block_causal_attention(qf, kf, v, MLA_QK_DIM ** -0.5)
    y_mla = my.reshape(B, S, MLA_V_W) @ w_mla_o

    merged = jax.nn.sigmoid(g_ret) * y_ret + jax.nn.sigmoid(g_mla) * y_mla
    return merged @ w_out


def setup_inputs(seed: int = 0) -> dict:
    key = jax.random.key(seed)
    ks = jax.random.split(key, 20)
    f32 = jnp.float32

    def nrm(k, shape, fan_in):
        return jax.random.normal(k, shape, f32) * (fan_in ** -0.5)

    def gain(k, shape):
        return 1.0 + 0.02 * jax.random.normal(k, shape, f32)

    x = jax.random.normal(ks[0], (BATCH, SEQ, D_MODEL), f32)
    start = jax.random.randint(ks[1], (BATCH, 1), 0, 4096, dtype=jnp.int32)
    positions = (start + jnp.arange(SEQ, dtype=jnp.int32)[None, :]).astype(jnp.int32)
    return {
        "x": x,
        "positions": positions,
        "norm_mix_g": gain(ks[2], (DEPTH, D_MODEL)),
        "w_in": nrm(ks[3], (DEPTH, D_MODEL, N_IN), D_MODEL),
        "ret_norm_g": gain(ks[4], (DEPTH, RET_V_W)),
        "w_ret_o": nrm(ks[5], (DEPTH, RET_V_W, D_MODEL), RET_V_W),
        "q_a_norm_g": gain(ks[6], (DEPTH, Q_LORA)),
        "w_q_b": nrm(ks[7], (DEPTH, Q_LORA, MLA_HEADS * MLA_QK_DIM), Q_LORA),
        "kv_a_norm_g": gain(ks[8], (DEPTH, KV_LORA)),
        "w_kv_b": nrm(ks[9], (DEPTH, KV_LORA, MLA_HEADS * (QK_NOPE + V_HEAD)), KV_LORA),
        "w_mla_o": nrm(ks[10], (DEPTH, MLA_V_W, D_MODEL), MLA_V_W),
        "w_out": nrm(ks[11], (DEPTH, D_MODEL, D_MODEL), D_MODEL),
        "norm_mlp_g": gain(ks[12], (DEPTH, D_MODEL)),
        "w_up": nrm(ks[13], (DEPTH, D_MODEL, D_FF), D_MODEL),
        "w_down": nrm(ks[14], (DEPTH, D_FF, D_MODEL), D_FF),
        "norm_f_g": gain(ks[15], (D_MODEL,)),
    }


def reference(x, positions, norm_mix_g, w_in, ret_norm_g, w_ret_o, q_a_norm_g, w_q_b,
              kv_a_norm_g, w_kv_b, w_mla_o, w_out, norm_mlp_g, w_up, w_down, norm_f_g):
    h = x
    for l in range(DEPTH):
        u = _rmsnorm(h, norm_mix_g[l])
        h = h + _mixer(u, positions, w_in[l], ret_norm_g[l], w_ret_o[l], q_a_norm_g[l],
                       w_q_b[l], kv_a_norm_g[l], w_kv_b[l], w_mla_o[l], w_out[l])
        z = _rmsnorm(h, norm_mlp_g[l]) @ w_up[l]
        h = h + jnp.square(jax.nn.relu(z)) @ w_down[l]
    return _rmsnorm(h, norm_f_g)
```

```python
import functools

import jax
import jax.numpy as jnp
from jax import lax
from jax.experimental import pallas as pl
from jax.experimental.pallas import tpu as pltpu

F32 = jnp.float32
BF16 = jnp.bfloat16

EPS = 1e-6
ROPE_THETA = 10000.0
CHUNK = 64

RET_HEADS = 8
RET_QK_DIM = 128
RET_V_DIM = 256
RET_QK_W = RET_HEADS * RET_QK_DIM
RET_V_W = RET_HEADS * RET_V_DIM

MLA_HEADS = 16
Q_LORA = 512
KV_LORA = 512
QK_NOPE = 128
QK_ROPE = 64
V_HEAD = 128
MLA_QK_DIM = QK_NOPE + QK_ROPE
MLA_HEAD_PAD = 256

LANES = 128
VMEM_LIMIT = 56 * 1024 * 1024


def _params(sem):
    return pltpu.CompilerParams(dimension_semantics=sem, vmem_limit_bytes=VMEM_LIMIT)


def _rms(x, g):
    return x * lax.rsqrt(jnp.mean(x * x, axis=-1, keepdims=True) + EPS) * g


def _rope128(t, c, s):
    return t * c + pltpu.roll(t, LANES // 2, 1) * s


def _rope_table_body(pos_ref, invr_ref, sgnr_ref, invm_ref, mskm_ref, sgnm_ref,
                     cr_ref, sr_ref, cm_ref, sm_ref):
    p = pos_ref[...].astype(F32)
    ar = p * invr_ref[...]
    cr_ref[...] = jnp.cos(ar)
    sr_ref[...] = jnp.sin(ar) * sgnr_ref[...]
    am = p * invm_ref[...]
    cm_ref[...] = jnp.cos(am) * mskm_ref[...]
    sm_ref[...] = jnp.sin(am) * sgnm_ref[...]


def _rope_tables(pos, rows=1024):
    T = pos.shape[0]
    half_r = RET_QK_DIM // 2
    half_m = QK_ROPE // 2
    inv_r = ROPE_THETA ** (-jnp.arange(half_r, dtype=F32) / half_r)
    inv_m = ROPE_THETA ** (-jnp.arange(half_m, dtype=F32) / half_m)
    z = jnp.zeros((half_m,), F32)
    o = jnp.ones((half_m,), F32)
    invr = jnp.concatenate([inv_r, inv_r])[None]
    sgnr = jnp.concatenate([-jnp.ones((half_r,), F32), jnp.ones((half_r,), F32)])[None]
    invm = jnp.concatenate([inv_m, z, inv_m, z])[None]
    mskm = jnp.concatenate([o, z, o, z])[None]
    sgnm = jnp.concatenate([-o, z, o, z])[None]
    row = pl.BlockSpec((1, LANES), lambda i: (0, 0))
    tab = pl.BlockSpec((rows, LANES), lambda i: (i, 0))
    return pl.pallas_call(
        _rope_table_body,
        grid=(T // rows,),
        in_specs=[pl.BlockSpec((rows, 1), lambda i: (i, 0)), row, row, row, row, row],
        out_specs=[tab, tab, tab, tab],
        out_shape=[jax.ShapeDtypeStruct((T, LANES), F32)] * 4,
        compiler_params=_params(("parallel",)),
        name="rope_tables",
    )(pos, invr, sgnr, invm, mskm, sgnm)


def _inproj_body(x_ref, g_ref, w_ref, cr_ref, sr_ref, o_ref, u_ref, *, tn):
    j = pl.program_id(1)

    @pl.when(j == 0)
    def _norm():
        u_ref[...] = _rms(x_ref[...], g_ref[...]).astype(BF16)

    acc = jnp.dot(u_ref[...], w_ref[...], preferred_element_type=F32)
    col = j * tn

    @pl.when(col < 2 * RET_QK_W)
    def _rope():
        scale = jnp.where(col >= RET_QK_W, RET_QK_DIM ** -0.5, 1.0).astype(F32)
        c = cr_ref[...] * scale
        s = sr_ref[...] * scale
        for hh in range(tn // LANES):
            sl = slice(hh * LANES, (hh + 1) * LANES)
            o_ref[:, sl] = _rope128(acc[:, sl], c, s).astype(BF16)

    @pl.when((col >= 2 * RET_QK_W) & (col < 2 * RET_QK_W + RET_V_W))
    def _plain():
        o_ref[...] = acc.astype(BF16)

    @pl.when((col >= 2 * RET_QK_W + RET_V_W) & (col < 2 * RET_QK_W + 2 * RET_V_W))
    def _silu():
        o_ref[...] = (acc * jax.nn.sigmoid(acc)).astype(BF16)

    @pl.when(col >= 2 * RET_QK_W + 2 * RET_V_W)
    def _sig():
        o_ref[...] = jax.nn.sigmoid(acc).astype(BF16)


def _inproj(x, g, w, cr, sr, tm=512, tn=1024):
    T, D = x.shape
    N = w.shape[1]
    return pl.pallas_call(
        functools.partial(_inproj_body, tn=tn),
        grid=(T // tm, N // tn),
        in_specs=[
            pl.BlockSpec((tm, D), lambda i, j: (i, 0)),
            pl.BlockSpec((1, D), lambda i, j: (0, 0)),
            pl.BlockSpec((D, tn), lambda i, j: (0, j)),
            pl.BlockSpec((tm, LANES), lambda i, j: (i, 0)),
            pl.BlockSpec((tm, LANES), lambda i, j: (i, 0)),
        ],
        out_specs=pl.BlockSpec((tm, tn), lambda i, j: (i, j)),
        out_shape=jax.ShapeDtypeStruct((T, N), BF16),
        scratch_shapes=[pltpu.VMEM((tm, D), BF16)],
        compiler_params=_params(("parallel", "arbitrary")),
        name="norm_inproj",
    )(x, g, w, cr, sr)


def _mla_proj_body(x_ref, g_ref, wlr_ref, gq_ref, gkv_ref, wq_ref, wk_ref, wv_ref, cm_ref, sm_ref,
                   q_ref, k_ref, v_ref):
    u = _rms(x_ref[...], g_ref[...]).astype(BF16)
    lr = jnp.dot(u, wlr_ref[...], preferred_element_type=F32)
    qa = _rms(lr[:, :Q_LORA], gq_ref[...]).astype(BF16)
    kva = _rms(lr[:, Q_LORA:Q_LORA + KV_LORA], gkv_ref[...]).astype(BF16)
    c = cm_ref[...]
    s = sm_ref[...]
    kpe = _rope128(lr[:, Q_LORA + KV_LORA:], c, s).astype(BF16)
    scale = MLA_QK_DIM ** -0.5
    cq = c * scale
    sq = s * scale
    q = jnp.dot(qa, wq_ref[...], preferred_element_type=F32)
    kn = jnp.dot(kva, wk_ref[...], preferred_element_type=F32)
    for h in range(MLA_HEADS):
        b = h * MLA_HEAD_PAD
        q_ref[:, b:b + QK_NOPE] = (q[:, b:b + QK_NOPE] * scale).astype(BF16)
        q_ref[:, b + QK_NOPE:b + MLA_HEAD_PAD] = _rope128(
            q[:, b + QK_NOPE:b + MLA_HEAD_PAD], cq, sq).astype(BF16)
        k_ref[:, b:b + QK_NOPE] = kn[:, h * QK_NOPE:(h + 1) * QK_NOPE].astype(BF16)
        k_ref[:, b + QK_NOPE:b + MLA_HEAD_PAD] = kpe
    v_ref[...] = jnp.dot(kva, wv_ref[...], preferred_element_type=F32).astype(BF16)


def _mla_proj(x, g, wlr, gq, gkv, wq, wk, wv, cm, sm, tm=512):
    T, D = x.shape
    const = lambda shape: pl.BlockSpec(shape, lambda i: (0, 0), pipeline_mode=pl.Buffered(1))
    rows = lambda w: pl.BlockSpec((tm, w), lambda i: (i, 0))
    QW = MLA_HEADS * MLA_HEAD_PAD
    VW = MLA_HEADS * V_HEAD
    return pl.pallas_call(
        _mla_proj_body,
        grid=(T // tm,),
        in_specs=[rows(D), const((1, D)), const(wlr.shape), const((1, Q_LORA)), const((1, KV_LORA)),
                  const(wq.shape), const(wk.shape), const(wv.shape), rows(LANES), rows(LANES)],
        out_specs=[rows(QW), rows(QW), rows(VW)],
        out_shape=[jax.ShapeDtypeStruct((T, QW), BF16), jax.ShapeDtypeStruct((T, QW), BF16),
                   jax.ShapeDtypeStruct((T, VW), BF16)],
        compiler_params=_params(("parallel",)),
        name="mla_proj",
    )(x, g, wlr, gq, gkv, wq, wk, wv, cm, sm)


def _attn_body(q_ref, k_ref, v_ref, o_ref, m_ref, l_ref, acc_ref, *, tq):
    qi = pl.program_id(2)
    q = q_ref[...]
    m_ref[...] = jnp.full_like(m_ref, -jnp.inf)
    l_ref[...] = jnp.zeros_like(l_ref)
    acc_ref[...] = jnp.zeros_like(acc_ref)

    def step(kb, masked):
        start = pl.multiple_of(kb * tq, tq)
        k = k_ref[pl.ds(start, tq), :]
        v = v_ref[pl.ds(start, tq), :]
        s = lax.dot_general(q, k, (((1,), (1,)), ((), ())), preferred_element_type=F32)
        if masked:
            rq = lax.broadcasted_iota(jnp.int32, (tq, tq), 0) // CHUNK
            ck = lax.broadcasted_iota(jnp.int32, (tq, tq), 1) // CHUNK
            s = jnp.where(ck <= rq, s, -jnp.inf)
        m_old = m_ref[...]
        m_new = jnp.maximum(m_old, jnp.max(s, axis=-1, keepdims=True))
        alpha = jnp.exp(m_old - m_new)
        p = jnp.exp(s - m_new)
        l_ref[...] = alpha * l_ref[...] + jnp.sum(p, axis=-1, keepdims=True)
        acc_ref[...] = alpha * acc_ref[...] + jnp.dot(p.astype(BF16), v, preferred_element_type=F32)
        m_ref[...] = m_new

    def full_step(kb, carry):
        step(kb, False)
        return carry

    lax.fori_loop(0, qi, full_step, 0)
    step(qi, True)
    o_ref[...] = (acc_ref[...] / l_ref[...]).astype(BF16)


def _attention(q, k, v, batch, seq, tq=512):
    nq = seq // tq
    return pl.pallas_call(
        functools.partial(_attn_body, tq=tq),
        grid=(batch, MLA_HEADS, nq),
        in_specs=[
            pl.BlockSpec((tq, MLA_HEAD_PAD), lambda b, h, i: (b * nq + i, h)),
            pl.BlockSpec((seq, MLA_HEAD_PAD), lambda b, h, i: (b, h)),
            pl.BlockSpec((seq, V_HEAD), lambda b, h, i: (b, h)),
        ],
        out_specs=pl.BlockSpec((tq, V_HEAD), lambda b, h, i: (b * nq + i, h)),
        out_shape=jax.ShapeDtypeStruct((batch * seq, MLA_HEADS * V_HEAD), BF16),
        scratch_shapes=[pltpu.VMEM((tq, 1), F32), pltpu.VMEM((tq, 1), F32),
                        pltpu.VMEM((tq, V_HEAD), F32)],
        compiler_params=_params(("parallel", "parallel", "arbitrary")),
        name="mla_attention",
    )(q, k, v)


def _retention_body(lg_ref, q_ref, k_ref, v_ref, sg_ref, g_ref, o_ref, st_ref, *, bc):
    blk = pl.program_id(1)

    @pl.when(blk == 0)
    def _init():
        st_ref[...] = jnp.zeros_like(st_ref)

    n = lax.broadcasted_iota(jnp.int32, (bc, bc), 0)
    m = lax.broadcasted_iota(jnp.int32, (bc, bc), 1)
    dist = jnp.abs(n - m).astype(F32)
    visible = (m // CHUNK) <= (n // CHUNK)
    nl = lax.broadcasted_iota(jnp.int32, (bc, 1), 0).astype(F32)
    for h in range(RET_HEADS):
        lg = lg_ref[h]
        decay = jnp.where(visible, jnp.exp(lg * dist), 0.0)
        q = q_ref[:, h * RET_QK_DIM:(h + 1) * RET_QK_DIM]
        k = k_ref[:, h * RET_QK_DIM:(h + 1) * RET_QK_DIM]
        v = v_ref[:, h * RET_V_DIM:(h + 1) * RET_V_DIM]
        s = lax.dot_general(q, k, (((1,), (1,)), ((), ())), preferred_element_type=F32) * decay
        qd = (q.astype(F32) * jnp.exp(lg * (nl + 1.0))).astype(BF16)
        st = st_ref[h]
        o = (jnp.dot(s.astype(BF16), v, preferred_element_type=F32)
             + jnp.dot(qd, st.astype(BF16), preferred_element_type=F32))
        kd = (k.astype(F32) * jnp.exp(lg * (bc - 1.0 - nl))).astype(BF16)
        c_dec = jnp.exp(jnp.full((1, RET_V_DIM), lg * bc, F32))
        st_ref[h] = st * c_dec + lax.dot_general(
            kd, v, (((0,), (0,)), ((), ())), preferred_element_type=F32)
        mu = jnp.mean(o, axis=-1, keepdims=True)
        oc = o - mu
        var = jnp.mean(oc * oc, axis=-1, keepdims=True)
        vs = slice(h * RET_V_DIM, (h + 1) * RET_V_DIM)
        y = oc * lax.rsqrt(var + EPS) * g_ref[:, vs]
        o_ref[:, vs] = (y * sg_ref[:, vs].astype(F32)).astype(BF16)


def _retention(proj, lg, g, batch, seq, bc=256):
    nb = seq // bc
    grid_spec = pltpu.PrefetchScalarGridSpec(
        num_scalar_prefetch=1,
        grid=(batch, nb),
        in_specs=[
            pl.BlockSpec((bc, RET_QK_W), lambda b, i, lg: (b * nb + i, 0)),
            pl.BlockSpec((bc, RET_QK_W), lambda b, i, lg: (b * nb + i, 1)),
            pl.BlockSpec((bc, RET_V_W), lambda b, i, lg: (b * nb + i, 1)),
            pl.BlockSpec((bc, RET_V_W), lambda b, i, lg: (b * nb + i, 2)),
            pl.BlockSpec((1, RET_V_W), lambda b, i, lg: (0, 0)),
        ],
        out_specs=pl.BlockSpec((bc, RET_V_W), lambda b, i, lg: (b * nb + i, 0)),
        scratch_shapes=[pltpu.VMEM((RET_HEADS, RET_QK_DIM, RET_V_DIM), F32)],
    )
    return pl.pallas_call(
        functools.partial(_retention_body, bc=bc),
        grid_spec=grid_spec,
        out_shape=jax.ShapeDtypeStruct((batch * seq, RET_V_W), BF16),
        compiler_params=_params(("parallel", "arbitrary")),
        name="retention",
    )(lg, proj, proj, proj, proj, g)


def _merge_body(a_ref, b_ref, wa_ref, wb_ref, ga_ref, gb_ref, o_ref):
    ya = jnp.dot(a_ref[...], wa_ref[...], preferred_element_type=F32)
    yb = jnp.dot(b_ref[...], wb_ref[...], preferred_element_type=F32)
    o_ref[...] = (ga_ref[...].astype(F32) * ya + gb_ref[...].astype(F32) * yb).astype(BF16)


def _merge(ry, my, w_ret_o, w_mla_o, proj, gate_col0, tm=512, tn=1024):
    T, K = ry.shape
    N = w_ret_o.shape[1]
    nj = N // tn
    ga0 = gate_col0 // tn
    return pl.pallas_call(
        _merge_body,
        grid=(T // tm, nj),
        in_specs=[
            pl.BlockSpec((tm, K), lambda i, j: (i, 0)),
            pl.BlockSpec((tm, K), lambda i, j: (i, 0)),
            pl.BlockSpec((K, tn), lambda i, j: (0, j)),
            pl.BlockSpec((K, tn), lambda i, j: (0, j)),
            pl.BlockSpec((tm, tn), lambda i, j: (i, ga0 + j)),
            pl.BlockSpec((tm, tn), lambda i, j: (i, ga0 + nj + j)),
        ],
        out_specs=pl.BlockSpec((tm, tn), lambda i, j: (i, j)),
        out_shape=jax.ShapeDtypeStruct((T, N), BF16),
        compiler_params=_params(("parallel", "arbitrary")),
        name="gated_merge",
    )(ry, my, w_ret_o, w_mla_o, proj, proj)


def _outproj_body(a_ref, w_ref, x_ref, o_ref):
    o_ref[...] = x_ref[...] + jnp.dot(a_ref[...], w_ref[...], preferred_element_type=F32)


def _outproj(a, w, x, tm=512, tn=1024):
    T, K = a.shape
    N = w.shape[1]
    return pl.pallas_call(
        _outproj_body,
        grid=(T // tm, N // tn),
        in_specs=[
            pl.BlockSpec((tm, K), lambda i, j: (i, 0)),
            pl.BlockSpec((K, tn), lambda i, j: (0, j)),
            pl.BlockSpec((tm, tn), lambda i, j: (i, j)),
        ],
        out_specs=pl.BlockSpec((tm, tn), lambda i, j: (i, j)),
        out_shape=jax.ShapeDtypeStruct((T, N), F32),
        compiler_params=_params(("parallel", "arbitrary")),
        name="out_proj",
    )(a, w, x)


def _mlp_body(h_ref, g_ref, wu_ref, wd_ref, gf_ref, o_ref, hn_ref, acc_ref):
    f = pl.program_id(1)

    @pl.when(f == 0)
    def _init():
        hn_ref[...] = _rms(h_ref[...], g_ref[...]).astype(BF16)
        acc_ref[...] = jnp.zeros_like(acc_ref)

    z = jnp.dot(hn_ref[...], wu_ref[...], preferred_element_type=F32)
    a = jnp.square(jnp.maximum(z, 0.0)).astype(BF16)
    acc_ref[...] += jnp.dot(a, wd_ref[...], preferred_element_type=F32)

    @pl.when(f == pl.num_programs(1) - 1)
    def _finish():
        o_ref[...] = _rms(h_ref[...] + acc_ref[...], gf_ref[...])


def _mlp(h, g, w_up, w_down, gf, tm=512, tf=1024):
    T, D = h.shape
    FF = w_up.shape[1]
    return pl.pallas_call(
        _mlp_body,
        grid=(T // tm, FF // tf),
        in_specs=[
            pl.BlockSpec((tm, D), lambda i, f: (i, 0)),
            pl.BlockSpec((1, D), lambda i, f: (0, 0)),
            pl.BlockSpec((D, tf), lambda i, f: (0, f)),
            pl.BlockSpec((tf, D), lambda i, f: (f, 0)),
            pl.BlockSpec((1, D), lambda i, f: (0, 0)),
        ],
        out_specs=pl.BlockSpec((tm, D), lambda i, f: (i, 0)),
        out_shape=jax.ShapeDtypeStruct((T, D), F32),
        scratch_shapes=[pltpu.VMEM((tm, D), BF16), pltpu.VMEM((tm, D), F32)],
        compiler_params=_params(("parallel", "arbitrary")),
        name="mlp_final_norm",
    )(h, g, w_up, w_down, gf)


def _prep_in_weights(w_in):
    lo = 2 * RET_QK_W + 2 * RET_V_W
    c_end = lo + Q_LORA + KV_LORA
    pe_end = c_end + QK_ROPE
    w_main = jnp.concatenate([w_in[:, :lo], w_in[:, pe_end:]], axis=1).astype(BF16)
    pe = w_in[:, c_end:pe_end]
    half = QK_ROPE // 2
    z = jnp.zeros((w_in.shape[0], half), w_in.dtype)
    w_lr = jnp.concatenate([w_in[:, lo:c_end], pe[:, :half], z, pe[:, half:], z], axis=1).astype(BF16)
    return w_main, w_lr


def _prep_q_weights(w_q_b):
    half = QK_ROPE // 2
    w = w_q_b.reshape(Q_LORA, MLA_HEADS, MLA_QK_DIM)
    z = jnp.zeros((Q_LORA, MLA_HEADS, half), w.dtype)
    w = jnp.concatenate([w[..., :QK_NOPE], w[..., QK_NOPE:QK_NOPE + half], z,
                         w[..., QK_NOPE + half:], z], axis=-1)
    return w.reshape(Q_LORA, MLA_HEADS * MLA_HEAD_PAD).astype(BF16)


def _prep_kv_weights(w_kv_b):
    w = w_kv_b.reshape(KV_LORA, MLA_HEADS, QK_NOPE + V_HEAD)
    wk = w[..., :QK_NOPE].reshape(KV_LORA, MLA_HEADS * QK_NOPE).astype(BF16)
    wv = w[..., QK_NOPE:].reshape(KV_LORA, MLA_HEADS * V_HEAD).astype(BF16)
    return wk, wv


def kernel(x, positions, norm_mix_g, w_in, ret_norm_g, w_ret_o, q_a_norm_g, w_q_b, kv_a_norm_g,
           w_kv_b, w_mla_o, w_out, norm_mlp_g, w_up, w_down, norm_f_g):
    B, S, D = x.shape
    T = B * S
    assert w_in.shape[0] == 1, "single-layer block: the final norm is fused into the MLP kernel"
    log_gamma = jnp.log(1.0 - 2.0 ** (-5.0 - jnp.arange(RET_HEADS, dtype=F32)))
    cr, sr, cm, sm = _rope_tables(positions.reshape(T, 1))
    h = x.reshape(T, D)
    w_main, w_lr = _prep_in_weights(w_in[0])
    wq = _prep_q_weights(w_q_b[0])
    wk, wv = _prep_kv_weights(w_kv_b[0])
    g_mix = norm_mix_g[0][None]
    proj = _inproj(h, g_mix, w_main, cr, sr)
    q, k, v = _mla_proj(h, g_mix, w_lr, q_a_norm_g[0][None], kv_a_norm_g[0][None],
                        wq, wk, wv, cm, sm)
    my = _attention(q, k, v, B, S)
    ry = _retention(proj, log_gamma, ret_norm_g[0][None], B, S)
    merged = _merge(ry, my, w_ret_o[0].astype(BF16), w_mla_o[0].astype(BF16), proj,
                    2 * RET_QK_W + 2 * RET_V_W)
    h1 = _outproj(merged, w_out[0].astype(BF16), h)
    out = _mlp(h1, norm_mlp_g[0][None], w_up[0].astype(BF16), w_down[0].astype(BF16),
               norm_f_g[None])
    return out.reshape(B, S, D)
```

```python
import functools

import jax
import jax.numpy as jnp
from jax import lax
from jax.experimental import pallas as pl
from jax.experimental.pallas import tpu as pltpu

F32 = jnp.float32
BF16 = jnp.bfloat16

EPS = 1e-6
ROPE_THETA = 10000.0
CHUNK = 64

RET_HEADS = 8
RET_QK_DIM = 128
RET_V_DIM = 256
RET_QK_W = RET_HEADS * RET_QK_DIM
RET_V_W = RET_HEADS * RET_V_DIM

MLA_HEADS = 16
Q_LORA = 512
KV_LORA = 512
QK_NOPE = 128
QK_ROPE = 64
V_HEAD = 128
MLA_QK_DIM = QK_NOPE + QK_ROPE
MLA_HEAD_PAD = 256

LOG2_E = 1.4426950408889634
LANES = 128
VMEM_LIMIT = 56 * 1024 * 1024


def _params(sem):
    return pltpu.CompilerParams(dimension_semantics=sem, vmem_limit_bytes=VMEM_LIMIT)


def _rms(x, g):
    return x * lax.rsqrt(jnp.mean(x * x, axis=-1, keepdims=True) + EPS) * g


def _rope128(t, c, s):
    return t * c + pltpu.roll(t, LANES // 2, 1) * s


def _rope_table_body(pos_ref, invr_ref, sgnr_ref, invm_ref, mskm_ref, sgnm_ref,
                     cr_ref, sr_ref, cm_ref, sm_ref):
    p = pos_ref[...].astype(F32)
    ar = p * invr_ref[...]
    cr_ref[...] = jnp.cos(ar)
    sr_ref[...] = jnp.sin(ar) * sgnr_ref[...]
    am = p * invm_ref[...]
    cm_ref[...] = jnp.cos(am) * mskm_ref[...]
    sm_ref[...] = jnp.sin(am) * sgnm_ref[...]


def _rope_tables(pos, rows=1024):
    T = pos.shape[0]
    half_r = RET_QK_DIM // 2
    half_m = QK_ROPE // 2
    inv_r = ROPE_THETA ** (-jnp.arange(half_r, dtype=F32) / half_r)
    inv_m = ROPE_THETA ** (-jnp.arange(half_m, dtype=F32) / half_m)
    z = jnp.zeros((half_m,), F32)
    o = jnp.ones((half_m,), F32)
    invr = jnp.concatenate([inv_r, inv_r])[None]
    sgnr = jnp.concatenate([-jnp.ones((half_r,), F32), jnp.ones((half_r,), F32)])[None]
    invm = jnp.concatenate([inv_m, z, inv_m, z])[None]
    mskm = jnp.concatenate([o, z, o, z])[None]
    sgnm = jnp.concatenate([-o, z, o, z])[None]
    row = pl.BlockSpec((1, LANES), lambda i: (0, 0))
    tab = pl.BlockSpec((rows, LANES), lambda i: (i, 0))
    return pl.pallas_call(
        _rope_table_body,
        grid=(T // rows,),
        in_specs=[pl.BlockSpec((rows, 1), lambda i: (i, 0)), row, row, row, row, row],
        out_specs=[tab, tab, tab, tab],
        out_shape=[jax.ShapeDtypeStruct((T, LANES), F32)] * 4,
        compiler_params=_params(("parallel",)),
        name="rope_tables",
    )(pos, invr, sgnr, invm, mskm, sgnm)


def _inproj_body(x_ref, g_ref, w_ref, cr_ref, sr_ref, o_ref, u_ref, *, tn):
    j = pl.program_id(1)

    @pl.when(j == 0)
    def _norm():
        u_ref[...] = _rms(x_ref[...], g_ref[...]).astype(BF16)

    acc = jnp.dot(u_ref[...], w_ref[...], preferred_element_type=F32)
    col = j * tn

    @pl.when(col < 2 * RET_QK_W)
    def _rope():
        scale = jnp.where(col >= RET_QK_W, RET_QK_DIM ** -0.5, 1.0).astype(F32)
        c = cr_ref[...] * scale
        s = sr_ref[...] * scale
        for hh in range(tn // LANES):
            sl = slice(hh * LANES, (hh + 1) * LANES)
            o_ref[:, sl] = _rope128(acc[:, sl], c, s).astype(BF16)

    @pl.when((col >= 2 * RET_QK_W) & (col < 2 * RET_QK_W + RET_V_W))
    def _plain():
        o_ref[...] = acc.astype(BF16)

    @pl.when((col >= 2 * RET_QK_W + RET_V_W) & (col < 2 * RET_QK_W + 2 * RET_V_W))
    def _silu():
        o_ref[...] = (acc * jax.nn.sigmoid(acc)).astype(BF16)

    @pl.when(col >= 2 * RET_QK_W + 2 * RET_V_W)
    def _sig():
        o_ref[...] = jax.nn.sigmoid(acc).astype(BF16)


def _inproj(x, g, w, cr, sr, tm=512, tn=1024):
    T, D = x.shape
    N = w.shape[1]
    return pl.pallas_call(
        functools.partial(_inproj_body, tn=tn),
        grid=(T // tm, N // tn),
        in_specs=[
            pl.BlockSpec((tm, D), lambda i, j: (i, 0)),
            pl.BlockSpec((1, D), lambda i, j: (0, 0)),
            pl.BlockSpec((D, tn), lambda i, j: (0, j)),
            pl.BlockSpec((tm, LANES), lambda i, j: (i, 0)),
            pl.BlockSpec((tm, LANES), lambda i, j: (i, 0)),
        ],
        out_specs=pl.BlockSpec((tm, tn), lambda i, j: (i, j)),
        out_shape=jax.ShapeDtypeStruct((T, N), BF16),
        scratch_shapes=[pltpu.VMEM((tm, D), BF16)],
        compiler_params=_params(("parallel", "arbitrary")),
        name="norm_inproj",
    )(x, g, w, cr, sr)


def _mla_proj_body(x_ref, g_ref, wlr_ref, gq_ref, gkv_ref, wq_ref, wk_ref, wv_ref, cm_ref, sm_ref,
                   q_ref, k_ref, v_ref):
    u = _rms(x_ref[...], g_ref[...]).astype(BF16)
    lr = jnp.dot(u, wlr_ref[...], preferred_element_type=F32)
    qa = _rms(lr[:, :Q_LORA], gq_ref[...]).astype(BF16)
    kva = _rms(lr[:, Q_LORA:Q_LORA + KV_LORA], gkv_ref[...]).astype(BF16)
    c = cm_ref[...]
    s = sm_ref[...]
    kpe = _rope128(lr[:, Q_LORA + KV_LORA:], c, s).astype(BF16)
    scale = MLA_QK_DIM ** -0.5 * LOG2_E
    cq = c * scale
    sq = s * scale
    q = jnp.dot(qa, wq_ref[...], preferred_element_type=F32)
    kn = jnp.dot(kva, wk_ref[...], preferred_element_type=F32)
    for h in range(MLA_HEADS):
        b = h * MLA_HEAD_PAD
        q_ref[:, b:b + QK_NOPE] = (q[:, b:b + QK_NOPE] * scale).astype(BF16)
        q_ref[:, b + QK_NOPE:b + MLA_HEAD_PAD] = _rope128(
            q[:, b + QK_NOPE:b + MLA_HEAD_PAD], cq, sq).astype(BF16)
        k_ref[:, b:b + QK_NOPE] = kn[:, h * QK_NOPE:(h + 1) * QK_NOPE].astype(BF16)
        k_ref[:, b + QK_NOPE:b + MLA_HEAD_PAD] = kpe
    v_ref[...] = jnp.dot(kva, wv_ref[...], preferred_element_type=F32).astype(BF16)


def _mla_proj(x, g, wlr, gq, gkv, wq, wk, wv, cm, sm, tm=512):
    T, D = x.shape
    const = lambda shape: pl.BlockSpec(shape, lambda i: (0, 0), pipeline_mode=pl.Buffered(1))
    rows = lambda w: pl.BlockSpec((tm, w), lambda i: (i, 0))
    QW = MLA_HEADS * MLA_HEAD_PAD
    VW = MLA_HEADS * V_HEAD
    return pl.pallas_call(
        _mla_proj_body,
        grid=(T // tm,),
        in_specs=[rows(D), const((1, D)), const(wlr.shape), const((1, Q_LORA)), const((1, KV_LORA)),
                  const(wq.shape), const(wk.shape), const(wv.shape), rows(LANES), rows(LANES)],
        out_specs=[rows(QW), rows(QW), rows(VW)],
        out_shape=[jax.ShapeDtypeStruct((T, QW), BF16), jax.ShapeDtypeStruct((T, QW), BF16),
                   jax.ShapeDtypeStruct((T, VW), BF16)],
        compiler_params=_params(("parallel",)),
        name="mla_proj",
    )(x, g, wlr, gq, gkv, wq, wk, wv, cm, sm)


def _attn_body(q_ref, k_ref, v_ref, o_ref, m_ref, l_ref, acc_ref, *, tq, tk):
    qi = pl.program_id(2)
    q = q_ref[...]
    m_ref[...] = jnp.full_like(m_ref, -jnp.inf)
    l_ref[...] = jnp.zeros_like(l_ref)
    acc_ref[...] = jnp.zeros_like(acc_ref)
    reps = tk // LANES

    def step(start, diag_off):
        k = k_ref[pl.ds(start, tk), :]
        v = v_ref[pl.ds(start, tk), :]
        s = lax.dot_general(q, k, (((1,), (1,)), ((), ())), preferred_element_type=F32)
        if diag_off is not None:
            rq = lax.broadcasted_iota(jnp.int32, (tq, tk), 0) // CHUNK
            ck = (lax.broadcasted_iota(jnp.int32, (tq, tk), 1) + diag_off) // CHUNK
            s = jnp.where(ck <= rq, s, -jnp.inf)
        m_old = m_ref[...]
        m_new = jnp.maximum(m_old, jnp.max(s, axis=-1, keepdims=True))
        alpha = jnp.exp2(m_old - m_new)
        p = jnp.exp2(s - jnp.concatenate([m_new] * reps, axis=1))
        l_ref[...] = alpha * l_ref[...] + jnp.sum(p, axis=-1, keepdims=True)
        acc_ref[...] = alpha * acc_ref[...] + jnp.dot(p.astype(BF16), v, preferred_element_type=F32)
        m_ref[...] = m_new

    def full_steps(kb, carry):
        for j in range(tq // tk):
            step(pl.multiple_of(kb * tq + j * tk, tk), None)
        return carry

    lax.fori_loop(0, qi, full_steps, 0)
    for j in range(tq // tk):
        step(pl.multiple_of(qi * tq + j * tk, tk), j * tk)
    o_ref[...] = (acc_ref[...] / l_ref[...]).astype(BF16)


def _attention(q, k, v, batch, seq, tq=1024, tk=512):
    nq = seq // tq
    return pl.pallas_call(
        functools.partial(_attn_body, tq=tq, tk=tk),
        grid=(batch, MLA_HEADS, nq),
        in_specs=[
            pl.BlockSpec((tq, MLA_HEAD_PAD), lambda b, h, i: (b * nq + i, h)),
            pl.BlockSpec((seq, MLA_HEAD_PAD), lambda b, h, i: (b, h)),
            pl.BlockSpec((seq, V_HEAD), lambda b, h, i: (b, h)),
        ],
        out_specs=pl.BlockSpec((tq, V_HEAD), lambda b, h, i: (b * nq + i, h)),
        out_shape=jax.ShapeDtypeStruct((batch * seq, MLA_HEADS * V_HEAD), BF16),
        scratch_shapes=[pltpu.VMEM((tq, LANES), F32), pltpu.VMEM((tq, LANES), F32),
                        pltpu.VMEM((tq, V_HEAD), F32)],
        compiler_params=_params(("parallel", "parallel", "arbitrary")),
        name="mla_attention",
    )(q, k, v)


def _retention_body(lg_ref, q_ref, k_ref, v_ref, sg_ref, g_ref, o_ref, st_ref, *, bc):
    blk = pl.program_id(1)

    @pl.when(blk == 0)
    def _init():
        st_ref[...] = jnp.zeros_like(st_ref)

    n = lax.broadcasted_iota(jnp.int32, (bc, bc), 0)
    m = lax.broadcasted_iota(jnp.int32, (bc, bc), 1)
    dist = jnp.abs(n - m).astype(F32)
    visible = (m // CHUNK) <= (n // CHUNK)
    nl = lax.broadcasted_iota(jnp.int32, (bc, 1), 0).astype(F32)
    for h in range(RET_HEADS):
        lg = lg_ref[h]
        decay = jnp.where(visible, jnp.exp(lg * dist), 0.0)
        q = q_ref[:, h * RET_QK_DIM:(h + 1) * RET_QK_DIM]
        k = k_ref[:, h * RET_QK_DIM:(h + 1) * RET_QK_DIM]
        v = v_ref[:, h * RET_V_DIM:(h + 1) * RET_V_DIM]
        s = lax.dot_general(q, k, (((1,), (1,)), ((), ())), preferred_element_type=F32) * decay
        qd = (q.astype(F32) * jnp.exp(lg * (nl + 1.0))).astype(BF16)
        st = st_ref[h]
        o = (jnp.dot(s.astype(BF16), v, preferred_element_type=F32)
             + jnp.dot(qd, st.astype(BF16), preferred_element_type=F32))
        kd = (k.astype(F32) * jnp.exp(lg * (bc - 1.0 - nl))).astype(BF16)
        c_dec = jnp.exp(jnp.full((1, RET_V_DIM), lg * bc, F32))
        st_ref[h] = st * c_dec + lax.dot_general(
            kd, v, (((0,), (0,)), ((), ())), preferred_element_type=F32)
        mu = jnp.mean(o, axis=-1, keepdims=True)
        oc = o - mu
        var = jnp.mean(oc * oc, axis=-1, keepdims=True)
        vs = slice(h * RET_V_DIM, (h + 1) * RET_V_DIM)
        y = oc * lax.rsqrt(var + EPS) * g_ref[:, vs]
        o_ref[:, vs] = (y * sg_ref[:, vs].astype(F32)).astype(BF16)


def _retention(proj, lg, g, batch, seq, bc=256):
    nb = seq // bc
    grid_spec = pltpu.PrefetchScalarGridSpec(
        num_scalar_prefetch=1,
        grid=(batch, nb),
        in_specs=[
            pl.BlockSpec((bc, RET_QK_W), lambda b, i, lg: (b * nb + i, 0)),
            pl.BlockSpec((bc, RET_QK_W), lambda b, i, lg: (b * nb + i, 1)),
            pl.BlockSpec((bc, RET_V_W), lambda b, i, lg: (b * nb + i, 1)),
            pl.BlockSpec((bc, RET_V_W), lambda b, i, lg: (b * nb + i, 2)),
            pl.BlockSpec((1, RET_V_W), lambda b, i, lg: (0, 0)),
        ],
        out_specs=pl.BlockSpec((bc, RET_V_W), lambda b, i, lg: (b * nb + i, 0)),
        scratch_shapes=[pltpu.VMEM((RET_HEADS, RET_QK_DIM, RET_V_DIM), F32)],
    )
    return pl.pallas_call(
        functools.partial(_retention_body, bc=bc),
        grid_spec=grid_spec,
        out_shape=jax.ShapeDtypeStruct((batch * seq, RET_V_W), BF16),
        compiler_params=_params(("parallel", "arbitrary")),
        name="retention",
    )(lg, proj, proj, proj, proj, g)


def _merge_body(a_ref, b_ref, wa_ref, wb_ref, ga_ref, gb_ref, o_ref):
    ya = jnp.dot(a_ref[...], wa_ref[...], preferred_element_type=F32)
    yb = jnp.dot(b_ref[...], wb_ref[...], preferred_element_type=F32)
    o_ref[...] = (ga_ref[...].astype(F32) * ya + gb_ref[...].astype(F32) * yb).astype(BF16)


def _merge(ry, my, w_ret_o, w_mla_o, proj, gate_col0, tm=512, tn=1024):
    T, K = ry.shape
    N = w_ret_o.shape[1]
    nj = N // tn
    ga0 = gate_col0 // tn
    return pl.pallas_call(
        _merge_body,
        grid=(T // tm, nj),
        in_specs=[
            pl.BlockSpec((tm, K), lambda i, j: (i, 0)),
            pl.BlockSpec((tm, K), lambda i, j: (i, 0)),
            pl.BlockSpec((K, tn), lambda i, j: (0, j)),
            pl.BlockSpec((K, tn), lambda i, j: (0, j)),
            pl.BlockSpec((tm, tn), lambda i, j: (i, ga0 + j)),
            pl.BlockSpec((tm, tn), lambda i, j: (i, ga0 + nj + j)),
        ],
        out_specs=pl.BlockSpec((tm, tn), lambda i, j: (i, j)),
        out_shape=jax.ShapeDtypeStruct((T, N), BF16),
        compiler_params=_params(("parallel", "arbitrary")),
        name="gated_merge",
    )(ry, my, w_ret_o, w_mla_o, proj, proj)


def _outproj_body(a_ref, w_ref, x_ref, o_ref):
    o_ref[...] = x_ref[...] + jnp.dot(a_ref[...], w_ref[...], preferred_element_type=F32)


def _outproj(a, w, x, tm=512, tn=1024):
    T, K = a.shape
    N = w.shape[1]
    return pl.pallas_call(
        _outproj_body,
        grid=(T // tm, N // tn),
        in_specs=[
            pl.BlockSpec((tm, K), lambda i, j: (i, 0)),
            pl.BlockSpec((K, tn), lambda i, j: (0, j)),
            pl.BlockSpec((tm, tn), lambda i, j: (i, j)),
        ],
        out_specs=pl.BlockSpec((tm, tn), lambda i, j: (i, j)),
        out_shape=jax.ShapeDtypeStruct((T, N), F32),
        compiler_params=_params(("parallel", "arbitrary")),
        name="out_proj",
    )(a, w, x)


def _mlp_body(h_ref, g_ref, wu_ref, wd_ref, gf_ref, o_ref, hn_ref, acc_ref):
    f = pl.program_id(1)

    @pl.when(f == 0)
    def _init():
        hn_ref[...] = _rms(h_ref[...], g_ref[...]).astype(BF16)
        acc_ref[...] = jnp.zeros_like(acc_ref)

    z = jnp.dot(hn_ref[...], wu_ref[...], preferred_element_type=F32)
    a = jnp.square(jnp.maximum(z, 0.0)).astype(BF16)
    acc_ref[...] += jnp.dot(a, wd_ref[...], preferred_element_type=F32)

    @pl.when(f == pl.num_programs(1) - 1)
    def _finish():
        o_ref[...] = _rms(h_ref[...] + acc_ref[...], gf_ref[...])


def _mlp(h, g, w_up, w_down, gf, tm=512, tf=1024):
    T, D = h.shape
    FF = w_up.shape[1]
    return pl.pallas_call(
        _mlp_body,
        grid=(T // tm, FF // tf),
        in_specs=[
            pl.BlockSpec((tm, D), lambda i, f: (i, 0)),
            pl.BlockSpec((1, D), lambda i, f: (0, 0)),
            pl.BlockSpec((D, tf), lambda i, f: (0, f)),
            pl.BlockSpec((tf, D), lambda i, f: (f, 0)),
            pl.BlockSpec((1, D), lambda i, f: (0, 0)),
        ],
        out_specs=pl.BlockSpec((tm, D), lambda i, f: (i, 0)),
        out_shape=jax.ShapeDtypeStruct((T, D), F32),
        scratch_shapes=[pltpu.VMEM((tm, D), BF16), pltpu.VMEM((tm, D), F32)],
        compiler_params=_params(("parallel", "arbitrary")),
        name="mlp_final_norm",
    )(h, g, w_up, w_down, gf)


def _prep_in_weights(w_in):
    lo = 2 * RET_QK_W + 2 * RET_V_W
    c_end = lo + Q_LORA + KV_LORA
    pe_end = c_end + QK_ROPE
    w_main = jnp.concatenate([w_in[:, :lo], w_in[:, pe_end:]], axis=1).astype(BF16)
    pe = w_in[:, c_end:pe_end]
    half = QK_ROPE // 2
    z = jnp.zeros((w_in.shape[0], half), w_in.dtype)
    w_lr = jnp.concatenate([w_in[:, lo:c_end], pe[:, :half], z, pe[:, half:], z], axis=1).astype(BF16)
    return w_main, w_lr


def _prep_q_weights(w_q_b):
    half = QK_ROPE // 2
    w = w_q_b.reshape(Q_LORA, MLA_HEADS, MLA_QK_DIM)
    z = jnp.zeros((Q_LORA, MLA_HEADS, half), w.dtype)
    w = jnp.concatenate([w[..., :QK_NOPE], w[..., QK_NOPE:QK_NOPE + half], z,
                         w[..., QK_NOPE + half:], z], axis=-1)
    return w.reshape(Q_LORA, MLA_HEADS * MLA_HEAD_PAD).astype(BF16)


def _prep_kv_weights(w_kv_b):
    w = w_kv_b.reshape(KV_LORA, MLA_HEADS, QK_NOPE + V_HEAD)
    wk = w[..., :QK_NOPE].reshape(KV_LORA, MLA_HEADS * QK_NOPE).astype(BF16)
    wv = w[..., QK_NOPE:].reshape(KV_LORA, MLA_HEADS * V_HEAD).astype(BF16)
    return wk, wv


def kernel(x, positions, norm_mix_g, w_in, ret_norm_g, w_ret_o, q_a_norm_g, w_q_b, kv_a_norm_g,
           w_kv_b, w_mla_o, w_out, norm_mlp_g, w_up, w_down, norm_f_g):
    B, S, D = x.shape
    T = B * S
    assert w_in.shape[0] == 1, "single-layer block: the final norm is fused into the MLP kernel"
    log_gamma = jnp.log(1.0 - 2.0 ** (-5.0 - jnp.arange(RET_HEADS, dtype=F32)))
    cr, sr, cm, sm = _rope_tables(positions.reshape(T, 1))
    h = x.reshape(T, D)
    w_main, w_lr = _prep_in_weights(w_in[0])
    wq = _prep_q_weights(w_q_b[0])
    wk, wv = _prep_kv_weights(w_kv_b[0])
    g_mix = norm_mix_g[0][None]
    proj = _inproj(h, g_mix, w_main, cr, sr)
    q, k, v = _mla_proj(h, g_mix, w_lr, q_a_norm_g[0][None], kv_a_norm_g[0][None],
                        wq, wk, wv, cm, sm)
    my = _attention(q, k, v, B, S)
    ry = _retention(proj, log_gamma, ret_norm_g[0][None], B, S)
    merged = _merge(ry, my, w_ret_o[0].astype(BF16), w_mla_o[0].astype(BF16), proj,
                    2 * RET_QK_W + 2 * RET_V_W)
    h1 = _outproj(merged, w_out[0].astype(BF16), h)
    out = _mlp(h1, norm_mlp_g[0][None], w_up[0].astype(BF16), w_down[0].astype(BF16),
               norm_f_g[None])
    return out.reshape(B, S, D)
```

```python
import functools

import jax
import jax.numpy as jnp
from jax import lax
from jax.experimental import pallas as pl
from jax.experimental.pallas import tpu as pltpu

F32 = jnp.float32
BF16 = jnp.bfloat16

EPS = 1e-6
ROPE_THETA = 10000.0
CHUNK = 64

RET_HEADS = 8
RET_QK_DIM = 128
RET_V_DIM = 256
RET_QK_W = RET_HEADS * RET_QK_DIM
RET_V_W = RET_HEADS * RET_V_DIM

MLA_HEADS = 16
Q_LORA = 512
KV_LORA = 512
QK_NOPE = 128
QK_ROPE = 64
V_HEAD = 128
MLA_QK_DIM = QK_NOPE + QK_ROPE
MLA_HEAD_PAD = 256

COL_RV = 2 * RET_QK_W
COL_RG = COL_RV + RET_V_W
COL_CQ = COL_RG + RET_V_W
COL_CKV = COL_CQ + Q_LORA
COL_KPE = COL_CKV + KV_LORA
COL_GRET = COL_KPE + QK_ROPE

LOG2_E = 1.4426950408889634
LANES = 128
VMEM_LIMIT = 56 * 1024 * 1024


def _params(sem):
    return pltpu.CompilerParams(dimension_semantics=sem, vmem_limit_bytes=VMEM_LIMIT)


def _rms(x, g):
    return x * lax.rsqrt(jnp.mean(x * x, axis=-1, keepdims=True) + EPS) * g


def _rope128(t, c, s):
    return t * c + pltpu.roll(t, LANES // 2, 1) * s


def _sigmoid(x):
    return 1.0 / (1.0 + jnp.exp(-x))


def _rope_table_body(pos_ref, invr_ref, sgnr_ref, invm_ref, mskm_ref, sgnm_ref,
                     cr_ref, sr_ref, cm_ref, sm_ref):
    p = pos_ref[...].astype(F32)
    ar = p * invr_ref[...]
    cr_ref[...] = jnp.cos(ar)
    sr_ref[...] = jnp.sin(ar) * sgnr_ref[...]
    am = p * invm_ref[...]
    cm_ref[...] = jnp.cos(am) * mskm_ref[...]
    sm_ref[...] = jnp.sin(am) * sgnm_ref[...]


def _rope_tables(pos, rows=1024):
    T = pos.shape[0]
    half_r = RET_QK_DIM // 2
    half_m = QK_ROPE // 2
    inv_r = ROPE_THETA ** (-jnp.arange(half_r, dtype=F32) / half_r)
    inv_m = ROPE_THETA ** (-jnp.arange(half_m, dtype=F32) / half_m)
    z = jnp.zeros((half_m,), F32)
    o = jnp.ones((half_m,), F32)
    invr = jnp.concatenate([inv_r, inv_r])[None]
    sgnr = jnp.concatenate([-jnp.ones((half_r,), F32), jnp.ones((half_r,), F32)])[None]
    invm = jnp.concatenate([inv_m, z, inv_m, z])[None]
    mskm = jnp.concatenate([o, z, o, z])[None]
    sgnm = jnp.concatenate([-o, z, o, z])[None]
    row = pl.BlockSpec((1, LANES), lambda i: (0, 0))
    tab = pl.BlockSpec((rows, LANES), lambda i: (i, 0))
    return pl.pallas_call(
        _rope_table_body,
        grid=(T // rows,),
        in_specs=[pl.BlockSpec((rows, 1), lambda i: (i, 0)), row, row, row, row, row],
        out_specs=[tab, tab, tab, tab],
        out_shape=[jax.ShapeDtypeStruct((T, LANES), F32)] * 4,
        compiler_params=_params(("parallel",)),
        name="rope_tables",
    )(pos, invr, sgnr, invm, mskm, sgnm)


def _rmsnorm_body(x_ref, g_ref, o_ref):
    o_ref[...] = _rms(x_ref[...], g_ref[...]).astype(o_ref.dtype)


def _rmsnorm(x, g, rows=512):
    T, D = x.shape
    return pl.pallas_call(
        _rmsnorm_body,
        grid=(T // rows,),
        in_specs=[pl.BlockSpec((rows, D), lambda i: (i, 0)), pl.BlockSpec((1, D), lambda i: (0, 0))],
        out_specs=pl.BlockSpec((rows, D), lambda i: (i, 0)),
        out_shape=jax.ShapeDtypeStruct((T, D), BF16),
        compiler_params=_params(("parallel",)),
        name="rmsnorm",
    )(x, g)


def _cast_weight(w_ref, wb_ref):
    @pl.when(pl.program_id(1) == 0)
    def _cast():
        wb_ref[...] = w_ref[...].astype(BF16)


def _inproj_body(u_ref, w_ref, o_ref, wb_ref):
    _cast_weight(w_ref, wb_ref)
    o_ref[...] = jnp.dot(u_ref[...], wb_ref[...], preferred_element_type=F32).astype(BF16)


def _inproj_head(u, w_in, n_cols, tm=1024, tn=1024):
    T, D = u.shape
    return pl.pallas_call(
        _inproj_body,
        grid=(n_cols // tn, T // tm),
        in_specs=[
            pl.BlockSpec((tm, D), lambda j, i: (i, 0)),
            pl.BlockSpec((D, tn), lambda j, i: (0, j), pipeline_mode=pl.Buffered(1)),
        ],
        out_specs=pl.BlockSpec((tm, tn), lambda j, i: (i, j)),
        out_shape=jax.ShapeDtypeStruct((T, n_cols), BF16),
        scratch_shapes=[pltpu.VMEM((D, tn), BF16)],
        compiler_params=_params(("parallel", "arbitrary")),
        name="inproj_head",
    )(u, w_in)


def _inproj_tail_body(u_ref, w_ref, o_ref):
    o_ref[...] = jnp.dot(u_ref[...], w_ref[...], preferred_element_type=F32).astype(BF16)


def _inproj_tail(u, w_tail, tm=512):
    T, D = u.shape
    N = w_tail.shape[1]
    return pl.pallas_call(
        _inproj_tail_body,
        grid=(T // tm,),
        in_specs=[
            pl.BlockSpec((tm, D), lambda i: (i, 0)),
            pl.BlockSpec((D, N), lambda i: (0, 0), pipeline_mode=pl.Buffered(1)),
        ],
        out_specs=pl.BlockSpec((tm, N), lambda i: (i, 0)),
        out_shape=jax.ShapeDtypeStruct((T, N), BF16),
        compiler_params=_params(("parallel",)),
        name="inproj_tail",
    )(u, w_tail)


def _mla_proj_body(cq_ref, ckv_ref, kpe_ref, gq_ref, gkv_ref, wq_ref, wk_ref, wv_ref, cm_ref, sm_ref,
                   q_ref, k_ref, v_ref):
    qa = _rms(cq_ref[...].astype(F32), gq_ref[...]).astype(BF16)
    kva = _rms(ckv_ref[...].astype(F32), gkv_ref[...]).astype(BF16)
    c = cm_ref[...]
    s = sm_ref[...]
    kpe = _rope128(kpe_ref[...].astype(F32), c, s).astype(BF16)
    scale = MLA_QK_DIM ** -0.5 * LOG2_E
    cq = c * scale
    sq = s * scale
    q = jnp.dot(qa, wq_ref[...], preferred_element_type=F32)
    kn = jnp.dot(kva, wk_ref[...], preferred_element_type=F32)
    for h in range(MLA_HEADS):
        b = h * MLA_HEAD_PAD
        q_ref[:, b:b + QK_NOPE] = (q[:, b:b + QK_NOPE] * scale).astype(BF16)
        q_ref[:, b + QK_NOPE:b + MLA_HEAD_PAD] = _rope128(
            q[:, b + QK_NOPE:b + MLA_HEAD_PAD], cq, sq).astype(BF16)
        k_ref[:, b:b + QK_NOPE] = kn[:, h * QK_NOPE:(h + 1) * QK_NOPE].astype(BF16)
        k_ref[:, b + QK_NOPE:b + MLA_HEAD_PAD] = kpe
    v_ref[...] = jnp.dot(kva, wv_ref[...], preferred_element_type=F32).astype(BF16)


def _mla_proj(proj, tail, kpe_col, gq, gkv, wq, wk, wv, cm, sm, tm=512):
    T = proj.shape[0]
    const = lambda shape: pl.BlockSpec(shape, lambda i: (0, 0), pipeline_mode=pl.Buffered(1))
    rows = lambda w: pl.BlockSpec((tm, w), lambda i: (i, 0))
    QW = MLA_HEADS * MLA_HEAD_PAD
    VW = MLA_HEADS * V_HEAD
    return pl.pallas_call(
        _mla_proj_body,
        grid=(T // tm,),
        in_specs=[pl.BlockSpec((tm, Q_LORA), lambda i: (i, COL_CQ // Q_LORA)),
                  pl.BlockSpec((tm, KV_LORA), lambda i: (i, COL_CKV // KV_LORA)),
                  pl.BlockSpec((tm, LANES), lambda i: (i, kpe_col // LANES)),
                  const((1, Q_LORA)), const((1, KV_LORA)),
                  const(wq.shape), const(wk.shape), const(wv.shape), rows(LANES), rows(LANES)],
        out_specs=[rows(QW), rows(QW), rows(VW)],
        out_shape=[jax.ShapeDtypeStruct((T, QW), BF16), jax.ShapeDtypeStruct((T, QW), BF16),
                   jax.ShapeDtypeStruct((T, VW), BF16)],
        compiler_params=_params(("parallel",)),
        name="mla_proj",
    )(proj, proj, tail, gq, gkv, wq, wk, wv, cm, sm)


def _attn_body(q_ref, k_ref, v_ref, o_ref, m_ref, l_ref, acc_ref, *, tq, tk):
    qi = pl.program_id(2)
    q = q_ref[...]
    m_ref[...] = jnp.full_like(m_ref, -jnp.inf)
    l_ref[...] = jnp.zeros_like(l_ref)
    acc_ref[...] = jnp.zeros_like(acc_ref)
    reps = tk // LANES

    def step(start, diag_off):
        k = k_ref[pl.ds(start, tk), :]
        v = v_ref[pl.ds(start, tk), :]
        s = lax.dot_general(q, k, (((1,), (1,)), ((), ())), preferred_element_type=F32)
        if diag_off is not None:
            rq = lax.broadcasted_iota(jnp.int32, (tq, tk), 0) // CHUNK
            ck = (lax.broadcasted_iota(jnp.int32, (tq, tk), 1) + diag_off) // CHUNK
            s = jnp.where(ck <= rq, s, -jnp.inf)
        m_old = m_ref[...]
        m_new = jnp.maximum(m_old, jnp.max(s, axis=-1, keepdims=True))
        alpha = jnp.exp2(m_old - m_new)
        p = jnp.exp2(s - jnp.concatenate([m_new] * reps, axis=1))
        l_ref[...] = alpha * l_ref[...] + jnp.sum(p, axis=-1, keepdims=True)
        acc_ref[...] = alpha * acc_ref[...] + jnp.dot(p.astype(BF16), v, preferred_element_type=F32)
        m_ref[...] = m_new

    def full_steps(kb, carry):
        for j in range(tq // tk):
            step(pl.multiple_of(kb * tq + j * tk, tk), None)
        return carry

    lax.fori_loop(0, qi, full_steps, 0)
    for j in range(tq // tk):
        step(pl.multiple_of(qi * tq + j * tk, tk), j * tk)
    o_ref[...] = (acc_ref[...] / l_ref[...]).astype(BF16)


def _attention(q, k, v, batch, seq, tq=1024, tk=512):
    nq = seq // tq
    return pl.pallas_call(
        functools.partial(_attn_body, tq=tq, tk=tk),
        grid=(batch, MLA_HEADS, nq),
        in_specs=[
            pl.BlockSpec((tq, MLA_HEAD_PAD), lambda b, h, i: (b * nq + i, h)),
            pl.BlockSpec((seq, MLA_HEAD_PAD), lambda b, h, i: (b, h)),
            pl.BlockSpec((seq, V_HEAD), lambda b, h, i: (b, h)),
        ],
        out_specs=pl.BlockSpec((tq, V_HEAD), lambda b, h, i: (b * nq + i, h)),
        out_shape=jax.ShapeDtypeStruct((batch * seq, MLA_HEADS * V_HEAD), BF16),
        scratch_shapes=[pltpu.VMEM((tq, LANES), F32), pltpu.VMEM((tq, LANES), F32),
                        pltpu.VMEM((tq, V_HEAD), F32)],
        compiler_params=_params(("parallel", "parallel", "arbitrary")),
        name="mla_attention",
    )(q, k, v)


def _retention_body(lg_ref, q_ref, k_ref, v_ref, rg_ref, cr_ref, sr_ref, g_ref, o_ref, st_ref, *, bc):
    blk = pl.program_id(1)

    @pl.when(blk == 0)
    def _init():
        st_ref[...] = jnp.zeros_like(st_ref)

    n = lax.broadcasted_iota(jnp.int32, (bc, bc), 0)
    m = lax.broadcasted_iota(jnp.int32, (bc, bc), 1)
    dist = jnp.abs(n - m).astype(F32)
    visible = (m // CHUNK) <= (n // CHUNK)
    nl = lax.broadcasted_iota(jnp.int32, (bc, 1), 0).astype(F32)
    c = cr_ref[...]
    s_ = sr_ref[...]
    k_scale = RET_QK_DIM ** -0.5
    ck = c * k_scale
    sk = s_ * k_scale
    for h in range(RET_HEADS):
        lg = lg_ref[h]
        decay = jnp.where(visible, jnp.exp(lg * dist), 0.0)
        qs = slice(h * RET_QK_DIM, (h + 1) * RET_QK_DIM)
        vs = slice(h * RET_V_DIM, (h + 1) * RET_V_DIM)
        q32 = _rope128(q_ref[:, qs].astype(F32), c, s_)
        k32 = _rope128(k_ref[:, qs].astype(F32), ck, sk)
        q = q32.astype(BF16)
        k = k32.astype(BF16)
        v = v_ref[:, vs]
        s = lax.dot_general(q, k, (((1,), (1,)), ((), ())), preferred_element_type=F32) * decay
        qd = (q32 * jnp.exp(lg * (nl + 1.0))).astype(BF16)
        st = st_ref[h]
        o = (jnp.dot(s.astype(BF16), v, preferred_element_type=F32)
             + jnp.dot(qd, st.astype(BF16), preferred_element_type=F32))
        kd = (k32 * jnp.exp(lg * (bc - 1.0 - nl))).astype(BF16)
        c_dec = jnp.exp(jnp.full((1, RET_V_DIM), lg * bc, F32))
        st_ref[h] = st * c_dec + lax.dot_general(
            kd, v, (((0,), (0,)), ((), ())), preferred_element_type=F32)
        mu = jnp.mean(o, axis=-1, keepdims=True)
        oc = o - mu
        var = jnp.mean(oc * oc, axis=-1, keepdims=True)
        y = oc * lax.rsqrt(var + EPS) * g_ref[:, vs]
        rg = rg_ref[:, vs].astype(F32)
        o_ref[:, vs] = (y * (rg * _sigmoid(rg))).astype(BF16)


def _retention(proj, cr, sr, lg, g, batch, seq, bc=256):
    nb = seq // bc
    row = lambda b, i, lg: (b * nb + i, 0)
    grid_spec = pltpu.PrefetchScalarGridSpec(
        num_scalar_prefetch=1,
        grid=(batch, nb),
        in_specs=[
            pl.BlockSpec((bc, RET_QK_W), row),
            pl.BlockSpec((bc, RET_QK_W), lambda b, i, lg: (b * nb + i, 1)),
            pl.BlockSpec((bc, RET_V_W), lambda b, i, lg: (b * nb + i, COL_RV // RET_V_W)),
            pl.BlockSpec((bc, RET_V_W), lambda b, i, lg: (b * nb + i, COL_RG // RET_V_W)),
            pl.BlockSpec((bc, LANES), row),
            pl.BlockSpec((bc, LANES), row),
            pl.BlockSpec((1, RET_V_W), lambda b, i, lg: (0, 0)),
        ],
        out_specs=pl.BlockSpec((bc, RET_V_W), row),
        scratch_shapes=[pltpu.VMEM((RET_HEADS, RET_QK_DIM, RET_V_DIM), F32)],
    )
    return pl.pallas_call(
        functools.partial(_retention_body, bc=bc),
        grid_spec=grid_spec,
        out_shape=jax.ShapeDtypeStruct((batch * seq, RET_V_W), BF16),
        compiler_params=_params(("parallel", "arbitrary")),
        name="retention",
    )(lg, proj, proj, proj, proj, cr, sr, g)


def _merge_body(a_ref, b_ref, wa_ref, wb_ref, ga_ref, gb_ref, o_ref, wab_ref, wbb_ref):
    _cast_weight(wa_ref, wab_ref)
    _cast_weight(wb_ref, wbb_ref)
    ya = jnp.dot(a_ref[...], wab_ref[...], preferred_element_type=F32)
    yb = jnp.dot(b_ref[...], wbb_ref[...], preferred_element_type=F32)
    ga = _sigmoid(ga_ref[...].astype(F32))
    gb = _sigmoid(gb_ref[...].astype(F32))
    o_ref[...] = (ga * ya + gb * yb).astype(BF16)


def _merge(ry, my, w_ret_o, w_mla_o, tail, tm=512, tn=1024):
    T, K = ry.shape
    N = w_ret_o.shape[1]
    nj = N // tn
    wspec = pl.BlockSpec((K, tn), lambda j, i: (0, j), pipeline_mode=pl.Buffered(1))
    return pl.pallas_call(
        _merge_body,
        grid=(nj, T // tm),
        in_specs=[
            pl.BlockSpec((tm, K), lambda j, i: (i, 0)),
            pl.BlockSpec((tm, K), lambda j, i: (i, 0)),
            wspec, wspec,
            pl.BlockSpec((tm, tn), lambda j, i: (i, j)),
            pl.BlockSpec((tm, tn), lambda j, i: (i, nj + j)),
        ],
        out_specs=pl.BlockSpec((tm, tn), lambda j, i: (i, j)),
        out_shape=jax.ShapeDtypeStruct((T, N), BF16),
        scratch_shapes=[pltpu.VMEM((K, tn), BF16), pltpu.VMEM((K, tn), BF16)],
        compiler_params=_params(("parallel", "arbitrary")),
        name="gated_merge",
    )(ry, my, w_ret_o, w_mla_o, tail, tail)


def _outproj_body(a_ref, w_ref, x_ref, o_ref, wb_ref):
    _cast_weight(w_ref, wb_ref)
    o_ref[...] = x_ref[...] + jnp.dot(a_ref[...], wb_ref[...], preferred_element_type=F32)


def _outproj(a, w, x, tm=512, tn=1024):
    T, K = a.shape
    N = w.shape[1]
    return pl.pallas_call(
        _outproj_body,
        grid=(N // tn, T // tm),
        in_specs=[
            pl.BlockSpec((tm, K), lambda j, i: (i, 0)),
            pl.BlockSpec((K, tn), lambda j, i: (0, j), pipeline_mode=pl.Buffered(1)),
            pl.BlockSpec((tm, tn), lambda j, i: (i, j)),
        ],
        out_specs=pl.BlockSpec((tm, tn), lambda j, i: (i, j)),
        out_shape=jax.ShapeDtypeStruct((T, N), F32),
        scratch_shapes=[pltpu.VMEM((K, tn), BF16)],
        compiler_params=_params(("parallel", "arbitrary")),
        name="out_proj",
    )(a, w, x)


def _mlp_body(h_ref, g_ref, wu_ref, wd_ref, gf_ref, o_ref, hn_ref, acc_ref):
    f = pl.program_id(1)

    @pl.when(f == 0)
    def _init():
        hn_ref[...] = _rms(h_ref[...], g_ref[...]).astype(BF16)
        acc_ref[...] = jnp.zeros_like(acc_ref)

    z = jnp.dot(hn_ref[...], wu_ref[...], preferred_element_type=F32)
    a = jnp.square(jnp.maximum(z, 0.0)).astype(BF16)
    acc_ref[...] += jnp.dot(a, wd_ref[...], preferred_element_type=F32)

    @pl.when(f == pl.num_programs(1) - 1)
    def _finish():
        o_ref[...] = _rms(h_ref[...] + acc_ref[...], gf_ref[...])


def _mlp(h, g, w_up, w_down, gf, tm=512, tf=1024):
    T, D = h.shape
    FF = w_up.shape[1]
    return pl.pallas_call(
        _mlp_body,
        grid=(T // tm, FF // tf),
        in_specs=[
            pl.BlockSpec((tm, D), lambda i, f: (i, 0)),
            pl.BlockSpec((1, D), lambda i, f: (0, 0)),
            pl.BlockSpec((D, tf), lambda i, f: (0, f)),
            pl.BlockSpec((tf, D), lambda i, f: (f, 0)),
            pl.BlockSpec((1, D), lambda i, f: (0, 0)),
        ],
        out_specs=pl.BlockSpec((tm, D), lambda i, f: (i, 0)),
        out_shape=jax.ShapeDtypeStruct((T, D), F32),
        scratch_shapes=[pltpu.VMEM((tm, D), BF16), pltpu.VMEM((tm, D), F32)],
        compiler_params=_params(("parallel", "arbitrary")),
        name="mlp_final_norm",
    )(h, g, w_up, w_down, gf)


def _prep_tail_weights(w_in):
    half = QK_ROPE // 2
    pe = w_in[:, COL_KPE:COL_GRET]
    z = jnp.zeros((w_in.shape[0], half), w_in.dtype)
    return jnp.concatenate([w_in[:, COL_GRET:], pe[:, :half], z, pe[:, half:], z], axis=1).astype(BF16)


def _prep_q_weights(w_q_b):
    half = QK_ROPE // 2
    w = w_q_b.reshape(Q_LORA, MLA_HEADS, MLA_QK_DIM)
    z = jnp.zeros((Q_LORA, MLA_HEADS, half), w.dtype)
    w = jnp.concatenate([w[..., :QK_NOPE], w[..., QK_NOPE:QK_NOPE + half], z,
                         w[..., QK_NOPE + half:], z], axis=-1)
    return w.reshape(Q_LORA, MLA_HEADS * MLA_HEAD_PAD).astype(BF16)


def _prep_kv_weights(w_kv_b):
    w = w_kv_b.reshape(KV_LORA, MLA_HEADS, QK_NOPE + V_HEAD)
    wk = w[..., :QK_NOPE].reshape(KV_LORA, MLA_HEADS * QK_NOPE).astype(BF16)
    wv = w[..., QK_NOPE:].reshape(KV_LORA, MLA_HEADS * V_HEAD).astype(BF16)
    return wk, wv


def kernel(x, positions, norm_mix_g, w_in, ret_norm_g, w_ret_o, q_a_norm_g, w_q_b, kv_a_norm_g,
           w_kv_b, w_mla_o, w_out, norm_mlp_g, w_up, w_down, norm_f_g):
    B, S, D = x.shape
    T = B * S
    assert w_in.shape[0] == 1, "single-layer block: the final norm is fused into the MLP kernel"
    log_gamma = jnp.log(1.0 - 2.0 ** (-5.0 - jnp.arange(RET_HEADS, dtype=F32)))
    cr, sr, cm, sm = _rope_tables(positions.reshape(T, 1))
    h = x.reshape(T, D)
    w_tail = _prep_tail_weights(w_in[0])
    wq = _prep_q_weights(w_q_b[0])
    wk, wv = _prep_kv_weights(w_kv_b[0])
    u = _rmsnorm(h, norm_mix_g[0][None])
    proj = _inproj_head(u, w_in[0], COL_KPE)
    tail = _inproj_tail(u, w_tail)
    q, k, v = _mla_proj(proj, tail, 2 * D, q_a_norm_g[0][None], kv_a_norm_g[0][None],
                        wq, wk, wv, cm, sm)
    my = _attention(q, k, v, B, S)
    ry = _retention(proj, cr, sr, log_gamma, ret_norm_g[0][None], B, S)
    merged = _merge(ry, my, w_ret_o[0], w_mla_o[0], tail)
    h1 = _outproj(merged, w_out[0], h)
    out = _mlp(h1, norm_mlp_g[0][None], w_up[0].astype(BF16), w_down[0].astype(BF16),
               norm_f_g[None])
    return out.reshape(B, S, D)
```

```python
import functools

import jax
import jax.numpy as jnp
from jax import lax
from jax.experimental import pallas as pl
from jax.experimental.pallas import tpu as pltpu

F32 = jnp.float32
BF16 = jnp.bfloat16

EPS = 1e-6
ROPE_THETA = 10000.0
CHUNK = 64

RET_HEADS = 8
RET_QK_DIM = 128
RET_V_DIM = 256
RET_QK_W = RET_HEADS * RET_QK_DIM
RET_V_W = RET_HEADS * RET_V_DIM

MLA_HEADS = 16
Q_LORA = 512
KV_LORA = 512
QK_NOPE = 128
QK_ROPE = 64
V_HEAD = 128
MLA_QK_DIM = QK_NOPE + QK_ROPE
MLA_HEAD_PAD = 256

COL_RV = 2 * RET_QK_W
COL_RG = COL_RV + RET_V_W
COL_CQ = COL_RG + RET_V_W
COL_CKV = COL_CQ + Q_LORA
COL_KPE = COL_CKV + KV_LORA
COL_GRET = COL_KPE + QK_ROPE

LOG2_E = 1.4426950408889634
LANES = 128
VMEM_LIMIT = 56 * 1024 * 1024


def _params(sem):
    return pltpu.CompilerParams(dimension_semantics=sem, vmem_limit_bytes=VMEM_LIMIT)


def _rms(x, g):
    return x * lax.rsqrt(jnp.mean(x * x, axis=-1, keepdims=True) + EPS) * g


def _rope128(t, c, s):
    return t * c + pltpu.roll(t, LANES // 2, 1) * s


def _rope64(t, c, s_lo, s_hi):
    return t * c + pltpu.roll(t, LANES - QK_ROPE // 2, 1) * s_lo + pltpu.roll(t, QK_ROPE // 2, 1) * s_hi


def _sigmoid(x):
    return 1.0 / (1.0 + jnp.exp(-x))


def _rope_table_body(pos_ref, invr_ref, sgnr_ref, invm_ref, mskm_ref, mlo_ref, mhi_ref,
                     cr_ref, sr_ref, cm_ref, slo_ref, shi_ref):
    p = pos_ref[...].astype(F32)
    ar = p * invr_ref[...]
    cr_ref[...] = jnp.cos(ar)
    sr_ref[...] = jnp.sin(ar) * sgnr_ref[...]
    am = p * invm_ref[...]
    cm_ref[...] = jnp.cos(am) * mskm_ref[...]
    sn = jnp.sin(am)
    slo_ref[...] = sn * mlo_ref[...]
    shi_ref[...] = sn * mhi_ref[...]


def _rope_tables(pos, rows=1024):
    T = pos.shape[0]
    half_r = RET_QK_DIM // 2
    half_m = QK_ROPE // 2
    inv_r = ROPE_THETA ** (-jnp.arange(half_r, dtype=F32) / half_r)
    inv_m = ROPE_THETA ** (-jnp.arange(half_m, dtype=F32) / half_m)
    z = jnp.zeros((half_m,), F32)
    o = jnp.ones((half_m,), F32)
    invr = jnp.concatenate([inv_r, inv_r])[None]
    sgnr = jnp.concatenate([-jnp.ones((half_r,), F32), jnp.ones((half_r,), F32)])[None]
    invm = jnp.concatenate([inv_m, inv_m, z, z])[None]
    mskm = jnp.concatenate([o, o, z, z])[None]
    mlo = jnp.concatenate([-o, z, z, z])[None]
    mhi = jnp.concatenate([z, o, z, z])[None]
    row = pl.BlockSpec((1, LANES), lambda i: (0, 0))
    tab = pl.BlockSpec((rows, LANES), lambda i: (i, 0))
    return pl.pallas_call(
        _rope_table_body,
        grid=(T // rows,),
        in_specs=[pl.BlockSpec((rows, 1), lambda i: (i, 0)), row, row, row, row, row, row],
        out_specs=[tab] * 5,
        out_shape=[jax.ShapeDtypeStruct((T, LANES), F32)] * 5,
        compiler_params=_params(("parallel",)),
        name="rope_tables",
    )(pos, invr, sgnr, invm, mskm, mlo, mhi)


def _rmsnorm_body(x_ref, g_ref, o_ref):
    o_ref[...] = _rms(x_ref[...], g_ref[...]).astype(o_ref.dtype)


def _rmsnorm(x, g, rows=512):
    T, D = x.shape
    return pl.pallas_call(
        _rmsnorm_body,
        grid=(T // rows,),
        in_specs=[pl.BlockSpec((rows, D), lambda i: (i, 0)), pl.BlockSpec((1, D), lambda i: (0, 0))],
        out_specs=pl.BlockSpec((rows, D), lambda i: (i, 0)),
        out_shape=jax.ShapeDtypeStruct((T, D), BF16),
        compiler_params=_params(("parallel",)),
        name="rmsnorm",
    )(x, g)


def _cast_weight(w_ref, wb_ref):
    @pl.when(pl.program_id(1) == 0)
    def _cast():
        wb_ref[...] = w_ref[...].astype(BF16)


def _inproj_body(u_ref, wt_ref, o_ref, wb_ref):
    @pl.when(pl.program_id(1) == 0)
    def _cast():
        wb_ref[...] = wt_ref[...].T.astype(BF16)

    o_ref[...] = jnp.dot(u_ref[...], wb_ref[...], preferred_element_type=F32).astype(BF16)


def _inproj(u, w_in_t, tm=1024, tn=1024):
    T, D = u.shape
    n_head = COL_KPE // tn
    n_gate = (w_in_t.shape[0] - COL_GRET) // tn

    def w_rows(j, i):
        row = jnp.where(j < n_head, j * tn, COL_GRET + (j - n_head) * tn)
        return (pl.multiple_of(row, QK_ROPE), 0)

    return pl.pallas_call(
        _inproj_body,
        grid=(n_head + n_gate, T // tm),
        in_specs=[
            pl.BlockSpec((tm, D), lambda j, i: (i, 0)),
            pl.BlockSpec((pl.Element(tn), pl.Element(D)), w_rows, pipeline_mode=pl.Buffered(1)),
        ],
        out_specs=pl.BlockSpec((tm, tn), lambda j, i: (i, j)),
        out_shape=jax.ShapeDtypeStruct((T, (n_head + n_gate) * tn), BF16),
        scratch_shapes=[pltpu.VMEM((D, tn), BF16)],
        compiler_params=_params(("parallel", "arbitrary")),
        name="inproj",
    )(u, w_in_t)


def _mla_proj_body(cq_ref, ckv_ref, u_ref, wpe_ref, gq_ref, gkv_ref, wq_ref, wk_ref, wv_ref,
                   cm_ref, slo_ref, shi_ref, q_ref, k_ref, v_ref):
    qa = _rms(cq_ref[...].astype(F32), gq_ref[...]).astype(BF16)
    kva = _rms(ckv_ref[...].astype(F32), gkv_ref[...]).astype(BF16)
    c = cm_ref[...]
    s_lo = slo_ref[...]
    s_hi = shi_ref[...]
    kpe_raw = lax.dot_general(u_ref[...], wpe_ref[...].astype(BF16), (((1,), (1,)), ((), ())),
                              preferred_element_type=F32)
    kpe = _rope64(kpe_raw, c, s_lo, s_hi).astype(BF16)
    scale = MLA_QK_DIM ** -0.5 * LOG2_E
    cq = c * scale
    sq_lo = s_lo * scale
    sq_hi = s_hi * scale
    q = jnp.dot(qa, wq_ref[...], preferred_element_type=F32)
    kn = jnp.dot(kva, wk_ref[...], preferred_element_type=F32)
    for h in range(MLA_HEADS):
        b = h * MLA_HEAD_PAD
        q_ref[:, b:b + QK_NOPE] = (q[:, b:b + QK_NOPE] * scale).astype(BF16)
        q_ref[:, b + QK_NOPE:b + MLA_HEAD_PAD] = _rope64(
            q[:, b + QK_NOPE:b + MLA_HEAD_PAD], cq, sq_lo, sq_hi).astype(BF16)
        k_ref[:, b:b + QK_NOPE] = kn[:, h * QK_NOPE:(h + 1) * QK_NOPE].astype(BF16)
        k_ref[:, b + QK_NOPE:b + MLA_HEAD_PAD] = kpe
    v_ref[...] = jnp.dot(kva, wv_ref[...], preferred_element_type=F32).astype(BF16)


def _mla_proj(proj, u, w_in_t, gq, gkv, wq, wk, wv, cm, slo, shi, tm=512):
    T, D = u.shape
    const = lambda shape: pl.BlockSpec(shape, lambda i: (0, 0), pipeline_mode=pl.Buffered(1))
    rows = lambda w: pl.BlockSpec((tm, w), lambda i: (i, 0))
    QW = MLA_HEADS * MLA_HEAD_PAD
    VW = MLA_HEADS * V_HEAD
    return pl.pallas_call(
        _mla_proj_body,
        grid=(T // tm,),
        in_specs=[pl.BlockSpec((tm, Q_LORA), lambda i: (i, COL_CQ // Q_LORA)),
                  pl.BlockSpec((tm, KV_LORA), lambda i: (i, COL_CKV // KV_LORA)),
                  rows(D),
                  pl.BlockSpec((LANES, D), lambda i: (COL_KPE // LANES, 0), pipeline_mode=pl.Buffered(1)),
                  const((1, Q_LORA)), const((1, KV_LORA)),
                  const(wq.shape), const(wk.shape), const(wv.shape),
                  rows(LANES), rows(LANES), rows(LANES)],
        out_specs=[rows(QW), rows(QW), rows(VW)],
        out_shape=[jax.ShapeDtypeStruct((T, QW), BF16), jax.ShapeDtypeStruct((T, QW), BF16),
                   jax.ShapeDtypeStruct((T, VW), BF16)],
        compiler_params=_params(("parallel",)),
        name="mla_proj",
    )(proj, proj, u, w_in_t, gq, gkv, wq, wk, wv, cm, slo, shi)


def _attn_body(q_ref, k_ref, v_ref, o_ref, m_ref, l_ref, acc_ref, *, tq, tk):
    qi = pl.program_id(2)
    q = q_ref[...]
    m_ref[...] = jnp.full_like(m_ref, -jnp.inf)
    l_ref[...] = jnp.zeros_like(l_ref)
    acc_ref[...] = jnp.zeros_like(acc_ref)
    reps = tk // LANES

    def step(start, diag_off):
        k = k_ref[pl.ds(start, tk), :]
        v = v_ref[pl.ds(start, tk), :]
        s = lax.dot_general(q, k, (((1,), (1,)), ((), ())), preferred_element_type=F32)
        if diag_off is not None:
            rq = lax.broadcasted_iota(jnp.int32, (tq, tk), 0) // CHUNK
            ck = (lax.broadcasted_iota(jnp.int32, (tq, tk), 1) + diag_off) // CHUNK
            s = jnp.where(ck <= rq, s, -jnp.inf)
        m_old = m_ref[...]
        m_new = jnp.maximum(m_old, jnp.max(s, axis=-1, keepdims=True))
        alpha = jnp.exp2(m_old - m_new)
        p = jnp.exp2(s - jnp.concatenate([m_new] * reps, axis=1))
        l_ref[...] = alpha * l_ref[...] + jnp.sum(p, axis=-1, keepdims=True)
        acc_ref[...] = alpha * acc_ref[...] + jnp.dot(p.astype(BF16), v, preferred_element_type=F32)
        m_ref[...] = m_new

    def full_steps(kb, carry):
        for j in range(tq // tk):
            step(pl.multiple_of(kb * tq + j * tk, tk), None)
        return carry

    lax.fori_loop(0, qi, full_steps, 0)
    for j in range(tq // tk):
        step(pl.multiple_of(qi * tq + j * tk, tk), j * tk)
    o_ref[...] = (acc_ref[...] / l_ref[...]).astype(BF16)


def _attention(q, k, v, batch, seq, tq=1024, tk=512):
    nq = seq // tq
    return pl.pallas_call(
        functools.partial(_attn_body, tq=tq, tk=tk),
        grid=(batch, MLA_HEADS, nq),
        in_specs=[
            pl.BlockSpec((tq, MLA_HEAD_PAD), lambda b, h, i: (b * nq + i, h)),
            pl.BlockSpec((seq, MLA_HEAD_PAD), lambda b, h, i: (b, h)),
            pl.BlockSpec((seq, V_HEAD), lambda b, h, i: (b, h)),
        ],
        out_specs=pl.BlockSpec((tq, V_HEAD), lambda b, h, i: (b * nq + i, h)),
        out_shape=jax.ShapeDtypeStruct((batch * seq, MLA_HEADS * V_HEAD), BF16),
        scratch_shapes=[pltpu.VMEM((tq, LANES), F32), pltpu.VMEM((tq, LANES), F32),
                        pltpu.VMEM((tq, V_HEAD), F32)],
        compiler_params=_params(("parallel", "parallel", "arbitrary")),
        name="mla_attention",
    )(q, k, v)


def _retention_body(lg_ref, q_ref, k_ref, v_ref, rg_ref, cr_ref, sr_ref, g_ref, o_ref, st_ref, *, bc):
    blk = pl.program_id(1)

    @pl.when(blk == 0)
    def _init():
        st_ref[...] = jnp.zeros_like(st_ref)

    n = lax.broadcasted_iota(jnp.int32, (bc, bc), 0)
    m = lax.broadcasted_iota(jnp.int32, (bc, bc), 1)
    dist = jnp.abs(n - m).astype(F32)
    visible = (m // CHUNK) <= (n // CHUNK)
    nl = lax.broadcasted_iota(jnp.int32, (bc, 1), 0).astype(F32)
    c = cr_ref[...]
    s_ = sr_ref[...]
    k_scale = RET_QK_DIM ** -0.5
    ck = c * k_scale
    sk = s_ * k_scale
    for h in range(RET_HEADS):
        lg = lg_ref[h]
        decay = jnp.where(visible, jnp.exp(lg * dist), 0.0)
        qs = slice(h * RET_QK_DIM, (h + 1) * RET_QK_DIM)
        vs = slice(h * RET_V_DIM, (h + 1) * RET_V_DIM)
        q32 = _rope128(q_ref[:, qs].astype(F32), c, s_)
        k32 = _rope128(k_ref[:, qs].astype(F32), ck, sk)
        q = q32.astype(BF16)
        k = k32.astype(BF16)
        v = v_ref[:, vs]
        s = lax.dot_general(q, k, (((1,), (1,)), ((), ())), preferred_element_type=F32) * decay
        qd = (q32 * jnp.exp(lg * (nl + 1.0))).astype(BF16)
        st = st_ref[h]
        o = (jnp.dot(s.astype(BF16), v, preferred_element_type=F32)
             + jnp.dot(qd, st.astype(BF16), preferred_element_type=F32))
        kd = (k32 * jnp.exp(lg * (bc - 1.0 - nl))).astype(BF16)
        c_dec = jnp.exp(jnp.full((1, RET_V_DIM), lg * bc, F32))
        st_ref[h] = st * c_dec + lax.dot_general(
            kd, v, (((0,), (0,)), ((), ())), preferred_element_type=F32)
        mu = jnp.mean(o, axis=-1, keepdims=True)
        oc = o - mu
        var = jnp.mean(oc * oc, axis=-1, keepdims=True)
        y = oc * lax.rsqrt(var + EPS) * g_ref[:, vs]
        rg = rg_ref[:, vs].astype(F32)
        o_ref[:, vs] = (y * (rg * _sigmoid(rg))).astype(BF16)


def _retention(proj, cr, sr, lg, g, batch, seq, bc=256):
    nb = seq // bc
    row = lambda b, i, lg: (b * nb + i, 0)
    grid_spec = pltpu.PrefetchScalarGridSpec(
        num_scalar_prefetch=1,
        grid=(batch, nb),
        in_specs=[
            pl.BlockSpec((bc, RET_QK_W), row),
            pl.BlockSpec((bc, RET_QK_W), lambda b, i, lg: (b * nb + i, 1)),
            pl.BlockSpec((bc, RET_V_W), lambda b, i, lg: (b * nb + i, COL_RV // RET_V_W)),
            pl.BlockSpec((bc, RET_V_W), lambda b, i, lg: (b * nb + i, COL_RG // RET_V_W)),
            pl.BlockSpec((bc, LANES), row),
            pl.BlockSpec((bc, LANES), row),
            pl.BlockSpec((1, RET_V_W), lambda b, i, lg: (0, 0)),
        ],
        out_specs=pl.BlockSpec((bc, RET_V_W), row),
        scratch_shapes=[pltpu.VMEM((RET_HEADS, RET_QK_DIM, RET_V_DIM), F32)],
    )
    return pl.pallas_call(
        functools.partial(_retention_body, bc=bc),
        grid_spec=grid_spec,
        out_shape=jax.ShapeDtypeStruct((batch * seq, RET_V_W), BF16),
        compiler_params=_params(("parallel", "arbitrary")),
        name="retention",
    )(lg, proj, proj, proj, proj, cr, sr, g)


def _merge_body(a_ref, b_ref, wa_ref, wb_ref, ga_ref, gb_ref, o_ref, wab_ref, wbb_ref):
    _cast_weight(wa_ref, wab_ref)
    _cast_weight(wb_ref, wbb_ref)
    ya = jnp.dot(a_ref[...], wab_ref[...], preferred_element_type=F32)
    yb = jnp.dot(b_ref[...], wbb_ref[...], preferred_element_type=F32)
    ga = _sigmoid(ga_ref[...].astype(F32))
    gb = _sigmoid(gb_ref[...].astype(F32))
    o_ref[...] = (ga * ya + gb * yb).astype(BF16)


def _merge(ry, my, w_ret_o, w_mla_o, proj, gate_col, tm=512, tn=1024):
    T, K = ry.shape
    N = w_ret_o.shape[1]
    nj = N // tn
    g0 = gate_col // tn
    wspec = pl.BlockSpec((K, tn), lambda j, i: (0, j), pipeline_mode=pl.Buffered(1))
    return pl.pallas_call(
        _merge_body,
        grid=(nj, T // tm),
        in_specs=[
            pl.BlockSpec((tm, K), lambda j, i: (i, 0)),
            pl.BlockSpec((tm, K), lambda j, i: (i, 0)),
            wspec, wspec,
            pl.BlockSpec((tm, tn), lambda j, i: (i, g0 + j)),
            pl.BlockSpec((tm, tn), lambda j, i: (i, g0 + nj + j)),
        ],
        out_specs=pl.BlockSpec((tm, tn), lambda j, i: (i, j)),
        out_shape=jax.ShapeDtypeStruct((T, N), BF16),
        scratch_shapes=[pltpu.VMEM((K, tn), BF16), pltpu.VMEM((K, tn), BF16)],
        compiler_params=_params(("parallel", "arbitrary")),
        name="gated_merge",
    )(ry, my, w_ret_o, w_mla_o, proj, proj)


def _outproj_body(a_ref, w_ref, x_ref, o_ref, wb_ref):
    _cast_weight(w_ref, wb_ref)
    o_ref[...] = x_ref[...] + jnp.dot(a_ref[...], wb_ref[...], preferred_element_type=F32)


def _outproj(a, w, x, tm=512, tn=1024):
    T, K = a.shape
    N = w.shape[1]
    return pl.pallas_call(
        _outproj_body,
        grid=(N // tn, T // tm),
        in_specs=[
            pl.BlockSpec((tm, K), lambda j, i: (i, 0)),
            pl.BlockSpec((K, tn), lambda j, i: (0, j), pipeline_mode=pl.Buffered(1)),
            pl.BlockSpec((tm, tn), lambda j, i: (i, j)),
        ],
        out_specs=pl.BlockSpec((tm, tn), lambda j, i: (i, j)),
        out_shape=jax.ShapeDtypeStruct((T, N), F32),
        scratch_shapes=[pltpu.VMEM((K, tn), BF16)],
        compiler_params=_params(("parallel", "arbitrary")),
        name="out_proj",
    )(a, w, x)


def _mlp_body(h_ref, g_ref, wu_ref, wd_ref, gf_ref, o_ref, hn_ref, acc_ref):
    f = pl.program_id(1)

    @pl.when(f == 0)
    def _init():
        hn_ref[...] = _rms(h_ref[...], g_ref[...]).astype(BF16)
        acc_ref[...] = jnp.zeros_like(acc_ref)

    z = jnp.dot(hn_ref[...], wu_ref[...], preferred_element_type=F32)
    a = jnp.square(jnp.maximum(z, 0.0)).astype(BF16)
    acc_ref[...] += jnp.dot(a, wd_ref[...], preferred_element_type=F32)

    @pl.when(f == pl.num_programs(1) - 1)
    def _finish():
        o_ref[...] = _rms(h_ref[...] + acc_ref[...], gf_ref[...])


def _mlp(h, g, w_up, w_down, gf, tm=512, tf=1024):
    T, D = h.shape
    FF = w_up.shape[1]
    return pl.pallas_call(
        _mlp_body,
        grid=(T // tm, FF // tf),
        in_specs=[
            pl.BlockSpec((tm, D), lambda i, f: (i, 0)),
            pl.BlockSpec((1, D), lambda i, f: (0, 0)),
            pl.BlockSpec((D, tf), lambda i, f: (0, f)),
            pl.BlockSpec((tf, D), lambda i, f: (f, 0)),
            pl.BlockSpec((1, D), lambda i, f: (0, 0)),
        ],
        out_specs=pl.BlockSpec((tm, D), lambda i, f: (i, 0)),
        out_shape=jax.ShapeDtypeStruct((T, D), F32),
        scratch_shapes=[pltpu.VMEM((tm, D), BF16), pltpu.VMEM((tm, D), F32)],
        compiler_params=_params(("parallel", "arbitrary")),
        name="mlp_final_norm",
    )(h, g, w_up, w_down, gf)


def _prep_q_weights(w_q_b):
    w = w_q_b.reshape(Q_LORA, MLA_HEADS, MLA_QK_DIM)
    w = jnp.pad(w, ((0, 0), (0, 0), (0, MLA_HEAD_PAD - MLA_QK_DIM)))
    return w.reshape(Q_LORA, MLA_HEADS * MLA_HEAD_PAD).astype(BF16)


def _prep_kv_weights(w_kv_b):
    w = w_kv_b.reshape(KV_LORA, MLA_HEADS, QK_NOPE + V_HEAD)
    wk = w[..., :QK_NOPE].reshape(KV_LORA, MLA_HEADS * QK_NOPE).astype(BF16)
    wv = w[..., QK_NOPE:].reshape(KV_LORA, MLA_HEADS * V_HEAD).astype(BF16)
    return wk, wv


def kernel(x, positions, norm_mix_g, w_in, ret_norm_g, w_ret_o, q_a_norm_g, w_q_b, kv_a_norm_g,
           w_kv_b, w_mla_o, w_out, norm_mlp_g, w_up, w_down, norm_f_g):
    B, S, D = x.shape
    T = B * S
    assert w_in.shape[0] == 1, "single-layer block: the final norm is fused into the MLP kernel"
    log_gamma = jnp.log(1.0 - 2.0 ** (-5.0 - jnp.arange(RET_HEADS, dtype=F32)))
    cr, sr, cm, slo, shi = _rope_tables(positions.reshape(T, 1))
    h = x.reshape(T, D)
    w_in_t = jnp.swapaxes(w_in[0], 0, 1)
    wq = _prep_q_weights(w_q_b[0])
    wk, wv = _prep_kv_weights(w_kv_b[0])
    u = _rmsnorm(h, norm_mix_g[0][None])
    proj = _inproj(u, w_in_t)
    q, k, v = _mla_proj(proj, u, w_in_t, q_a_norm_g[0][None], kv_a_norm_g[0][None],
                        wq, wk, wv, cm, slo, shi)
    my = _attention(q, k, v, B, S)
    ry = _retention(proj, cr, sr, log_gamma, ret_norm_g[0][None], B, S)
    merged = _merge(ry, my, w_ret_o[0], w_mla_o[0], proj, COL_KPE)
    h1 = _outproj(merged, w_out[0], h)
    out = _mlp(h1, norm_mlp_g[0][None], w_up[0].astype(BF16), w_down[0].astype(BF16),
               norm_f_g[None])
    return out.reshape(B, S, D)
```

```python
import functools

import jax
import jax.numpy as jnp
from jax import lax
from jax.experimental import pallas as pl
from jax.experimental.pallas import tpu as pltpu

F32 = jnp.float32
BF16 = jnp.bfloat16

EPS = 1e-6
ROPE_THETA = 10000.0
CHUNK = 64

RET_HEADS = 8
RET_QK_DIM = 128
RET_V_DIM = 256
RET_QK_W = RET_HEADS * RET_QK_DIM
RET_V_W = RET_HEADS * RET_V_DIM

MLA_HEADS = 16
Q_LORA = 512
KV_LORA = 512
QK_NOPE = 128
QK_ROPE = 64
V_HEAD = 128
MLA_QK_DIM = QK_NOPE + QK_ROPE
MLA_HEAD_PAD = 256

COL_RV = 2 * RET_QK_W
COL_RG = COL_RV + RET_V_W
COL_CQ = COL_RG + RET_V_W
COL_CKV = COL_CQ + Q_LORA
COL_KPE = COL_CKV + KV_LORA
COL_GRET = COL_KPE + QK_ROPE

LOG2_E = 1.4426950408889634
LANES = 128
VMEM_LIMIT = 56 * 1024 * 1024


def _params(sem):
    return pltpu.CompilerParams(dimension_semantics=sem, vmem_limit_bytes=VMEM_LIMIT)


def _rms(x, g):
    return x * lax.rsqrt(jnp.mean(x * x, axis=-1, keepdims=True) + EPS) * g


def _rope128(t, c, s):
    return t * c + pltpu.roll(t, LANES // 2, 1) * s


def _rope64(t, c, s_lo, s_hi):
    return t * c + pltpu.roll(t, LANES - QK_ROPE // 2, 1) * s_lo + pltpu.roll(t, QK_ROPE // 2, 1) * s_hi


def _sigmoid(x):
    return 1.0 / (1.0 + jnp.exp(-x))


def _rope_table_body(pos_ref, invr_ref, sgnr_ref, invm_ref, mskm_ref, mlo_ref, mhi_ref,
                     cr_ref, sr_ref, cm_ref, slo_ref, shi_ref):
    p = pos_ref[...].astype(F32)
    ar = p * invr_ref[...]
    cr_ref[...] = jnp.cos(ar)
    sr_ref[...] = jnp.sin(ar) * sgnr_ref[...]
    am = p * invm_ref[...]
    cm_ref[...] = jnp.cos(am) * mskm_ref[...]
    sn = jnp.sin(am)
    slo_ref[...] = sn * mlo_ref[...]
    shi_ref[...] = sn * mhi_ref[...]


def _rope_tables(pos, rows=1024):
    T = pos.shape[0]
    half_r = RET_QK_DIM // 2
    half_m = QK_ROPE // 2
    inv_r = ROPE_THETA ** (-jnp.arange(half_r, dtype=F32) / half_r)
    inv_m = ROPE_THETA ** (-jnp.arange(half_m, dtype=F32) / half_m)
    z = jnp.zeros((half_m,), F32)
    o = jnp.ones((half_m,), F32)
    invr = jnp.concatenate([inv_r, inv_r])[None]
    sgnr = jnp.concatenate([-jnp.ones((half_r,), F32), jnp.ones((half_r,), F32)])[None]
    invm = jnp.concatenate([inv_m, inv_m, z, z])[None]
    mskm = jnp.concatenate([o, o, z, z])[None]
    mlo = jnp.concatenate([-o, z, z, z])[None]
    mhi = jnp.concatenate([z, o, z, z])[None]
    row = pl.BlockSpec((1, LANES), lambda i: (0, 0))
    tab = pl.BlockSpec((rows, LANES), lambda i: (i, 0))
    return pl.pallas_call(
        _rope_table_body,
        grid=(T // rows,),
        in_specs=[pl.BlockSpec((rows, 1), lambda i: (i, 0)), row, row, row, row, row, row],
        out_specs=[tab] * 5,
        out_shape=[jax.ShapeDtypeStruct((T, LANES), F32)] * 5,
        compiler_params=_params(("parallel",)),
        name="rope_tables",
    )(pos, invr, sgnr, invm, mskm, mlo, mhi)


def _rmsnorm_body(x_ref, g_ref, o_ref):
    o_ref[...] = _rms(x_ref[...], g_ref[...]).astype(o_ref.dtype)


def _rmsnorm(x, g, rows=512):
    T, D = x.shape
    return pl.pallas_call(
        _rmsnorm_body,
        grid=(T // rows,),
        in_specs=[pl.BlockSpec((rows, D), lambda i: (i, 0)), pl.BlockSpec((1, D), lambda i: (0, 0))],
        out_specs=pl.BlockSpec((rows, D), lambda i: (i, 0)),
        out_shape=jax.ShapeDtypeStruct((T, D), BF16),
        compiler_params=_params(("parallel",)),
        name="rmsnorm",
    )(x, g)


def _cast_weight(w_ref, wb_ref):
    @pl.when(pl.program_id(1) == 0)
    def _cast():
        wb_ref[...] = w_ref[...].astype(BF16)


def _inproj_body(u_ref, wt_ref, o_ref, wb_ref):
    @pl.when(pl.program_id(1) == 0)
    def _cast():
        wb_ref[...] = wt_ref[...].T.astype(BF16)

    o_ref[...] = jnp.dot(u_ref[...], wb_ref[...], preferred_element_type=F32).astype(BF16)


def _inproj(u, w_in_t, tm=1024, tn=1024):
    T, D = u.shape
    n_head = COL_KPE // tn
    n_gate = (w_in_t.shape[0] - COL_GRET) // tn

    def w_rows(j, i):
        row = jnp.where(j < n_head, j * tn, COL_GRET + (j - n_head) * tn)
        return (pl.multiple_of(row, QK_ROPE), 0)

    return pl.pallas_call(
        _inproj_body,
        grid=(n_head + n_gate, T // tm),
        in_specs=[
            pl.BlockSpec((tm, D), lambda j, i: (i, 0)),
            pl.BlockSpec((pl.Element(tn), pl.Element(D)), w_rows, pipeline_mode=pl.Buffered(1)),
        ],
        out_specs=pl.BlockSpec((tm, tn), lambda j, i: (i, j)),
        out_shape=jax.ShapeDtypeStruct((T, (n_head + n_gate) * tn), BF16),
        scratch_shapes=[pltpu.VMEM((D, tn), BF16)],
        compiler_params=_params(("parallel", "arbitrary")),
        name="inproj",
    )(u, w_in_t)


def _mla_proj_body(cq_ref, ckv_ref, u_ref, wpe_ref, gq_ref, gkv_ref, wq_ref, wk_ref, wv_ref,
                   cm_ref, slo_ref, shi_ref, q_ref, k_ref, v_ref):
    qa = _rms(cq_ref[...].astype(F32), gq_ref[...]).astype(BF16)
    kva = _rms(ckv_ref[...].astype(F32), gkv_ref[...]).astype(BF16)
    c = cm_ref[...]
    s_lo = slo_ref[...]
    s_hi = shi_ref[...]
    kpe_raw = lax.dot_general(u_ref[...], wpe_ref[...].astype(BF16), (((1,), (1,)), ((), ())),
                              preferred_element_type=F32)
    kpe = _rope64(kpe_raw, c, s_lo, s_hi).astype(BF16)
    scale = MLA_QK_DIM ** -0.5 * LOG2_E
    cq = c * scale
    sq_lo = s_lo * scale
    sq_hi = s_hi * scale
    q = jnp.dot(qa, wq_ref[...], preferred_element_type=F32)
    kn = jnp.dot(kva, wk_ref[...], preferred_element_type=F32)
    for h in range(MLA_HEADS):
        b = h * MLA_HEAD_PAD
        q_ref[:, b:b + QK_NOPE] = (q[:, b:b + QK_NOPE] * scale).astype(BF16)
        q_ref[:, b + QK_NOPE:b + MLA_HEAD_PAD] = _rope64(
            q[:, b + QK_NOPE:b + MLA_HEAD_PAD], cq, sq_lo, sq_hi).astype(BF16)
        k_ref[:, b:b + QK_NOPE] = kn[:, h * QK_NOPE:(h + 1) * QK_NOPE].astype(BF16)
        k_ref[:, b + QK_NOPE:b + MLA_HEAD_PAD] = kpe
    v_ref[...] = jnp.dot(kva, wv_ref[...], preferred_element_type=F32).astype(BF16)


def _mla_proj(proj, u, w_in_t, gq, gkv, wq, wk, wv, cm, slo, shi, tm=512):
    T, D = u.shape
    const = lambda shape: pl.BlockSpec(shape, lambda i: (0, 0), pipeline_mode=pl.Buffered(1))
    rows = lambda w: pl.BlockSpec((tm, w), lambda i: (i, 0))
    QW = MLA_HEADS * MLA_HEAD_PAD
    VW = MLA_HEADS * V_HEAD
    return pl.pallas_call(
        _mla_proj_body,
        grid=(T // tm,),
        in_specs=[pl.BlockSpec((tm, Q_LORA), lambda i: (i, COL_CQ // Q_LORA)),
                  pl.BlockSpec((tm, KV_LORA), lambda i: (i, COL_CKV // KV_LORA)),
                  rows(D),
                  pl.BlockSpec((LANES, D), lambda i: (COL_KPE // LANES, 0), pipeline_mode=pl.Buffered(1)),
                  const((1, Q_LORA)), const((1, KV_LORA)),
                  const(wq.shape), const(wk.shape), const(wv.shape),
                  rows(LANES), rows(LANES), rows(LANES)],
        out_specs=[rows(QW), rows(QW), rows(VW)],
        out_shape=[jax.ShapeDtypeStruct((T, QW), BF16), jax.ShapeDtypeStruct((T, QW), BF16),
                   jax.ShapeDtypeStruct((T, VW), BF16)],
        compiler_params=_params(("parallel",)),
        name="mla_proj",
    )(proj, proj, u, w_in_t, gq, gkv, wq, wk, wv, cm, slo, shi)


def _attn_body(q_ref, k_ref, v_ref, o_ref, m_ref, acc_ref, *, tq, tk, nh):
    qi = pl.program_id(2)
    m_ref[...] = jnp.full_like(m_ref, -jnp.inf)
    acc_ref[...] = jnp.zeros_like(acc_ref)
    reps = tk // LANES

    def step(start, diag_off):
        r0 = 0 if diag_off is None else diag_off
        rows = tq - r0
        for hh in range(nh):
            q = q_ref[r0:, hh * MLA_HEAD_PAD:(hh + 1) * MLA_HEAD_PAD]
            k = k_ref[pl.ds(start, tk), hh * MLA_HEAD_PAD:(hh + 1) * MLA_HEAD_PAD]
            v = v_ref[pl.ds(start, tk), hh * V_HEAD:(hh + 1) * V_HEAD]
            s = lax.dot_general(q, k, (((1,), (1,)), ((), ())), preferred_element_type=F32)
            if diag_off is not None:
                rq = lax.broadcasted_iota(jnp.int32, (rows, tk), 0) // CHUNK
                ck = lax.broadcasted_iota(jnp.int32, (rows, tk), 1) // CHUNK
                s = jnp.where(ck <= rq, s, -jnp.inf)
            m_old = m_ref[hh, r0:, :]
            m_new = jnp.maximum(m_old, jnp.max(s, axis=-1, keepdims=True))
            alpha = jnp.exp2(m_old - m_new)
            p = jnp.exp2(s - jnp.concatenate([m_new] * reps, axis=1))
            pv = jnp.dot(p.astype(BF16), jnp.concatenate([v, jnp.ones_like(v)], axis=1),
                         preferred_element_type=F32)
            acc_ref[hh, r0:, :] = jnp.concatenate([alpha, alpha], axis=1) * acc_ref[hh, r0:, :] + pv
            m_ref[hh, r0:, :] = m_new

    def full_steps(kb, carry):
        for j in range(tq // tk):
            step(pl.multiple_of(kb * tq + j * tk, tk), None)
        return carry

    lax.fori_loop(0, qi, full_steps, 0)
    for j in range(tq // tk):
        step(pl.multiple_of(qi * tq + j * tk, tk), j * tk)
    for hh in range(nh):
        o_ref[:, hh * V_HEAD:(hh + 1) * V_HEAD] = (
            acc_ref[hh, :, :V_HEAD] / acc_ref[hh, :, V_HEAD:]).astype(BF16)


def _attention(q, k, v, batch, seq, tq=2048, tk=512, nh=2):
    nq = seq // tq
    return pl.pallas_call(
        functools.partial(_attn_body, tq=tq, tk=tk, nh=nh),
        grid=(batch, MLA_HEADS // nh, nq),
        in_specs=[
            pl.BlockSpec((tq, nh * MLA_HEAD_PAD), lambda b, h, i: (b * nq + i, h)),
            pl.BlockSpec((seq, nh * MLA_HEAD_PAD), lambda b, h, i: (b, h)),
            pl.BlockSpec((seq, nh * V_HEAD), lambda b, h, i: (b, h)),
        ],
        out_specs=pl.BlockSpec((tq, nh * V_HEAD), lambda b, h, i: (b * nq + i, h)),
        out_shape=jax.ShapeDtypeStruct((batch * seq, MLA_HEADS * V_HEAD), BF16),
        scratch_shapes=[pltpu.VMEM((nh, tq, LANES), F32), pltpu.VMEM((nh, tq, 2 * V_HEAD), F32)],
        compiler_params=_params(("parallel", "parallel", "arbitrary")),
        name="mla_attention",
    )(q, k, v)


def _retention_body(lg_ref, q_ref, k_ref, v_ref, rg_ref, cr_ref, sr_ref, g_ref, o_ref, st_ref, *, bc):
    blk = pl.program_id(1)

    @pl.when(blk == 0)
    def _init():
        st_ref[...] = jnp.zeros_like(st_ref)

    n = lax.broadcasted_iota(jnp.int32, (bc, bc), 0)
    m = lax.broadcasted_iota(jnp.int32, (bc, bc), 1)
    dist = jnp.abs(n - m).astype(F32)
    visible = (m // CHUNK) <= (n // CHUNK)
    nl = lax.broadcasted_iota(jnp.int32, (bc, 1), 0).astype(F32)
    c = cr_ref[...]
    s_ = sr_ref[...]
    k_scale = RET_QK_DIM ** -0.5
    ck = c * k_scale
    sk = s_ * k_scale
    for h in range(RET_HEADS):
        lg = lg_ref[h]
        decay = jnp.where(visible, jnp.exp(lg * dist), 0.0)
        qs = slice(h * RET_QK_DIM, (h + 1) * RET_QK_DIM)
        vs = slice(h * RET_V_DIM, (h + 1) * RET_V_DIM)
        q32 = _rope128(q_ref[:, qs].astype(F32), c, s_)
        k32 = _rope128(k_ref[:, qs].astype(F32), ck, sk)
        q = q32.astype(BF16)
        k = k32.astype(BF16)
        v = v_ref[:, vs]
        s = lax.dot_general(q, k, (((1,), (1,)), ((), ())), preferred_element_type=F32) * decay
        qd = (q32 * jnp.exp(lg * (nl + 1.0))).astype(BF16)
        st = st_ref[h]
        o = (jnp.dot(s.astype(BF16), v, preferred_element_type=F32)
             + jnp.dot(qd, st.astype(BF16), preferred_element_type=F32))
        kd = (k32 * jnp.exp(lg * (bc - 1.0 - nl))).astype(BF16)
        c_dec = jnp.exp(jnp.full((1, RET_V_DIM), lg * bc, F32))
        st_ref[h] = st * c_dec + lax.dot_general(
            kd, v, (((0,), (0,)), ((), ())), preferred_element_type=F32)
        mu = jnp.mean(o, axis=-1, keepdims=True)
        oc = o - mu
        var = jnp.mean(oc * oc, axis=-1, keepdims=True)
        y = oc * lax.rsqrt(var + EPS) * g_ref[:, vs]
        rg = rg_ref[:, vs].astype(F32)
        o_ref[:, vs] = (y * (rg * _sigmoid(rg))).astype(BF16)


def _retention(proj, cr, sr, lg, g, batch, seq, bc=256):
    nb = seq // bc
    row = lambda b, i, lg: (b * nb + i, 0)
    grid_spec = pltpu.PrefetchScalarGridSpec(
        num_scalar_prefetch=1,
        grid=(batch, nb),
        in_specs=[
            pl.BlockSpec((bc, RET_QK_W), row),
            pl.BlockSpec((bc, RET_QK_W), lambda b, i, lg: (b * nb + i, 1)),
            pl.BlockSpec((bc, RET_V_W), lambda b, i, lg: (b * nb + i, COL_RV // RET_V_W)),
            pl.BlockSpec((bc, RET_V_W), lambda b, i, lg: (b * nb + i, COL_RG // RET_V_W)),
            pl.BlockSpec((bc, LANES), row),
            pl.BlockSpec((bc, LANES), row),
            pl.BlockSpec((1, RET_V_W), lambda b, i, lg: (0, 0)),
        ],
        out_specs=pl.BlockSpec((bc, RET_V_W), row),
        scratch_shapes=[pltpu.VMEM((RET_HEADS, RET_QK_DIM, RET_V_DIM), F32)],
    )
    return pl.pallas_call(
        functools.partial(_retention_body, bc=bc),
        grid_spec=grid_spec,
        out_shape=jax.ShapeDtypeStruct((batch * seq, RET_V_W), BF16),
        compiler_params=_params(("parallel", "arbitrary")),
        name="retention",
    )(lg, proj, proj, proj, proj, cr, sr, g)


def _merge_body(a_ref, b_ref, wa_ref, wb_ref, ga_ref, gb_ref, o_ref, wab_ref, wbb_ref):
    _cast_weight(wa_ref, wab_ref)
    _cast_weight(wb_ref, wbb_ref)
    ya = jnp.dot(a_ref[...], wab_ref[...], preferred_element_type=F32)
    yb = jnp.dot(b_ref[...], wbb_ref[...], preferred_element_type=F32)
    ga = _sigmoid(ga_ref[...].astype(F32))
    gb = _sigmoid(gb_ref[...].astype(F32))
    o_ref[...] = (ga * ya + gb * yb).astype(BF16)


def _merge(ry, my, w_ret_o, w_mla_o, proj, gate_col, tm=512, tn=1024):
    T, K = ry.shape
    N = w_ret_o.shape[1]
    nj = N // tn
    g0 = gate_col // tn
    wspec = pl.BlockSpec((K, tn), lambda j, i: (0, j), pipeline_mode=pl.Buffered(1))
    return pl.pallas_call(
        _merge_body,
        grid=(nj, T // tm),
        in_specs=[
            pl.BlockSpec((tm, K), lambda j, i: (i, 0)),
            pl.BlockSpec((tm, K), lambda j, i: (i, 0)),
            wspec, wspec,
            pl.BlockSpec((tm, tn), lambda j, i: (i, g0 + j)),
            pl.BlockSpec((tm, tn), lambda j, i: (i, g0 + nj + j)),
        ],
        out_specs=pl.BlockSpec((tm, tn), lambda j, i: (i, j)),
        out_shape=jax.ShapeDtypeStruct((T, N), BF16),
        scratch_shapes=[pltpu.VMEM((K, tn), BF16), pltpu.VMEM((K, tn), BF16)],
        compiler_params=_params(("parallel", "arbitrary")),
        name="gated_merge",
    )(ry, my, w_ret_o, w_mla_o, proj, proj)


def _outproj_body(a_ref, w_ref, x_ref, o_ref, wb_ref):
    _cast_weight(w_ref, wb_ref)
    o_ref[...] = x_ref[...] + jnp.dot(a_ref[...], wb_ref[...], preferred_element_type=F32)


def _outproj(a, w, x, tm=512, tn=1024):
    T, K = a.shape
    N = w.shape[1]
    return pl.pallas_call(
        _outproj_body,
        grid=(N // tn, T // tm),
        in_specs=[
            pl.BlockSpec((tm, K), lambda j, i: (i, 0)),
            pl.BlockSpec((K, tn), lambda j, i: (0, j), pipeline_mode=pl.Buffered(1)),
            pl.BlockSpec((tm, tn), lambda j, i: (i, j)),
        ],
        out_specs=pl.BlockSpec((tm, tn), lambda j, i: (i, j)),
        out_shape=jax.ShapeDtypeStruct((T, N), F32),
        scratch_shapes=[pltpu.VMEM((K, tn), BF16)],
        compiler_params=_params(("parallel", "arbitrary")),
        name="out_proj",
    )(a, w, x)


def _mlp_body(h_ref, g_ref, wu_ref, wd_ref, gf_ref, o_ref, hn_ref, acc_ref):
    f = pl.program_id(1)

    @pl.when(f == 0)
    def _init():
        hn_ref[...] = _rms(h_ref[...], g_ref[...]).astype(BF16)
        acc_ref[...] = jnp.zeros_like(acc_ref)

    z = jnp.dot(hn_ref[...], wu_ref[...], preferred_element_type=F32)
    a = jnp.square(jnp.maximum(z, 0.0)).astype(BF16)
    acc_ref[...] += jnp.dot(a, wd_ref[...], preferred_element_type=F32)

    @pl.when(f == pl.num_programs(1) - 1)
    def _finish():
        o_ref[...] = _rms(h_ref[...] + acc_ref[...], gf_ref[...])


def _mlp(h, g, w_up, w_down, gf, tm=512, tf=1024):
    T, D = h.shape
    FF = w_up.shape[1]
    return pl.pallas_call(
        _mlp_body,
        grid=(T // tm, FF // tf),
        in_specs=[
            pl.BlockSpec((tm, D), lambda i, f: (i, 0)),
            pl.BlockSpec((1, D), lambda i, f: (0, 0)),
            pl.BlockSpec((D, tf), lambda i, f: (0, f)),
            pl.BlockSpec((tf, D), lambda i, f: (f, 0)),
            pl.BlockSpec((1, D), lambda i, f: (0, 0)),
        ],
        out_specs=pl.BlockSpec((tm, D), lambda i, f: (i, 0)),
        out_shape=jax.ShapeDtypeStruct((T, D), F32),
        scratch_shapes=[pltpu.VMEM((tm, D), BF16), pltpu.VMEM((tm, D), F32)],
        compiler_params=_params(("parallel", "arbitrary")),
        name="mlp_final_norm",
    )(h, g, w_up, w_down, gf)


def _prep_q_weights(w_q_b):
    w = w_q_b.reshape(Q_LORA, MLA_HEADS, MLA_QK_DIM)
    w = jnp.pad(w, ((0, 0), (0, 0), (0, MLA_HEAD_PAD - MLA_QK_DIM)))
    return w.reshape(Q_LORA, MLA_HEADS * MLA_HEAD_PAD).astype(BF16)


def _prep_kv_weights(w_kv_b):
    w = w_kv_b.reshape(KV_LORA, MLA_HEADS, QK_NOPE + V_HEAD)
    wk = w[..., :QK_NOPE].reshape(KV_LORA, MLA_HEADS * QK_NOPE).astype(BF16)
    wv = w[..., QK_NOPE:].reshape(KV_LORA, MLA_HEADS * V_HEAD).astype(BF16)
    return wk, wv


def kernel(x, positions, norm_mix_g, w_in, ret_norm_g, w_ret_o, q_a_norm_g, w_q_b, kv_a_norm_g,
           w_kv_b, w_mla_o, w_out, norm_mlp_g, w_up, w_down, norm_f_g):
    B, S, D = x.shape
    T = B * S
    assert w_in.shape[0] == 1, "single-layer block: the final norm is fused into the MLP kernel"
    log_gamma = jnp.log(1.0 - 2.0 ** (-5.0 - jnp.arange(RET_HEADS, dtype=F32)))
    cr, sr, cm, slo, shi = _rope_tables(positions.reshape(T, 1))
    h = x.reshape(T, D)
    w_in_t = jnp.swapaxes(w_in[0], 0, 1)
    wq = _prep_q_weights(w_q_b[0])
    wk, wv = _prep_kv_weights(w_kv_b[0])
    u = _rmsnorm(h, norm_mix_g[0][None])
    proj = _inproj(u, w_in_t)
    q, k, v = _mla_proj(proj, u, w_in_t, q_a_norm_g[0][None], kv_a_norm_g[0][None],
                        wq, wk, wv, cm, slo, shi)
    my = _attention(q, k, v, B, S)
    ry = _retention(proj, cr, sr, log_gamma, ret_norm_g[0][None], B, S)
    merged = _merge(ry, my, w_ret_o[0], w_mla_o[0], proj, COL_KPE)
    h1 = _outproj(merged, w_out[0], h)
    out = _mlp(h1, norm_mlp_g[0][None], w_up[0].astype(BF16), w_down[0].astype(BF16),
               norm_f_g[None])
    return out.reshape(B, S, D)
```

```python
import functools

import jax
import jax.numpy as jnp
from jax import lax
from jax.experimental import pallas as pl
from jax.experimental.pallas import tpu as pltpu

F32 = jnp.float32
BF16 = jnp.bfloat16

EPS = 1e-6
ROPE_THETA = 10000.0
CHUNK = 64

RET_HEADS = 8
RET_QK_DIM = 128
RET_V_DIM = 256
RET_QK_W = RET_HEADS * RET_QK_DIM
RET_V_W = RET_HEADS * RET_V_DIM

MLA_HEADS = 16
Q_LORA = 512
KV_LORA = 512
QK_NOPE = 128
QK_ROPE = 64
V_HEAD = 128
MLA_QK_DIM = QK_NOPE + QK_ROPE
MLA_HEAD_PAD = 256

COL_RV = 2 * RET_QK_W
COL_RG = COL_RV + RET_V_W
COL_CQ = COL_RG + RET_V_W
COL_CKV = COL_CQ + Q_LORA
COL_KPE = COL_CKV + KV_LORA
COL_GRET = COL_KPE + QK_ROPE

LOG2_E = 1.4426950408889634
LANES = 128
VMEM_LIMIT = 56 * 1024 * 1024


def _params(sem):
    return pltpu.CompilerParams(dimension_semantics=sem, vmem_limit_bytes=VMEM_LIMIT)


def _rms(x, g):
    return x * lax.rsqrt(jnp.mean(x * x, axis=-1, keepdims=True) + EPS) * g


def _rope128(t, c, s):
    return t * c + pltpu.roll(t, LANES // 2, 1) * s


def _rope64(t, c, s_lo, s_hi):
    return t * c + pltpu.roll(t, LANES - QK_ROPE // 2, 1) * s_lo + pltpu.roll(t, QK_ROPE // 2, 1) * s_hi


def _sigmoid(x):
    return 1.0 / (1.0 + jnp.exp(-x))


def _rope_table_body(pos_ref, invr_ref, sgnr_ref, invm_ref, mskm_ref, mlo_ref, mhi_ref,
                     cr_ref, sr_ref, cm_ref, slo_ref, shi_ref):
    p = pos_ref[...].astype(F32)
    ar = p * invr_ref[...]
    cr_ref[...] = jnp.cos(ar)
    sr_ref[...] = jnp.sin(ar) * sgnr_ref[...]
    am = p * invm_ref[...]
    cm_ref[...] = jnp.cos(am) * mskm_ref[...]
    sn = jnp.sin(am)
    slo_ref[...] = sn * mlo_ref[...]
    shi_ref[...] = sn * mhi_ref[...]


def _rope_tables(pos, rows=1024):
    T = pos.shape[0]
    half_r = RET_QK_DIM // 2
    half_m = QK_ROPE // 2
    inv_r = ROPE_THETA ** (-jnp.arange(half_r, dtype=F32) / half_r)
    inv_m = ROPE_THETA ** (-jnp.arange(half_m, dtype=F32) / half_m)
    z = jnp.zeros((half_m,), F32)
    o = jnp.ones((half_m,), F32)
    invr = jnp.concatenate([inv_r, inv_r])[None]
    sgnr = jnp.concatenate([-jnp.ones((half_r,), F32), jnp.ones((half_r,), F32)])[None]
    invm = jnp.concatenate([inv_m, inv_m, z, z])[None]
    mskm = jnp.concatenate([o, o, z, z])[None]
    mlo = jnp.concatenate([-o, z, z, z])[None]
    mhi = jnp.concatenate([z, o, z, z])[None]
    row = pl.BlockSpec((1, LANES), lambda i: (0, 0))
    tab = pl.BlockSpec((rows, LANES), lambda i: (i, 0))
    return pl.pallas_call(
        _rope_table_body,
        grid=(T // rows,),
        in_specs=[pl.BlockSpec((rows, 1), lambda i: (i, 0)), row, row, row, row, row, row],
        out_specs=[tab] * 5,
        out_shape=[jax.ShapeDtypeStruct((T, LANES), F32)] * 5,
        compiler_params=_params(("parallel",)),
        name="rope_tables",
    )(pos, invr, sgnr, invm, mskm, mlo, mhi)


def _rmsnorm_body(x_ref, g_ref, o_ref):
    o_ref[...] = _rms(x_ref[...], g_ref[...]).astype(o_ref.dtype)


def _rmsnorm(x, g, rows=512):
    T, D = x.shape
    return pl.pallas_call(
        _rmsnorm_body,
        grid=(T // rows,),
        in_specs=[pl.BlockSpec((rows, D), lambda i: (i, 0)), pl.BlockSpec((1, D), lambda i: (0, 0))],
        out_specs=pl.BlockSpec((rows, D), lambda i: (i, 0)),
        out_shape=jax.ShapeDtypeStruct((T, D), BF16),
        compiler_params=_params(("parallel",)),
        name="rmsnorm",
    )(x, g)


def _cast_weight(w_ref, wb_ref):
    @pl.when(pl.program_id(1) == 0)
    def _cast():
        wb_ref[...] = w_ref[...].astype(BF16)


def _inproj_body(u_ref, wt_ref, o_ref, wb_ref):
    @pl.when(pl.program_id(1) == 0)
    def _cast():
        wb_ref[...] = wt_ref[...].T.astype(BF16)

    o_ref[...] = jnp.dot(u_ref[...], wb_ref[...], preferred_element_type=F32).astype(BF16)


def _inproj(u, w_in_t, tm=1024, tn=1024):
    T, D = u.shape
    n_head = COL_KPE // tn
    n_gate = (w_in_t.shape[0] - COL_GRET) // tn

    def w_rows(j, i):
        row = jnp.where(j < n_head, j * tn, COL_GRET + (j - n_head) * tn)
        return (pl.multiple_of(row, QK_ROPE), 0)

    return pl.pallas_call(
        _inproj_body,
        grid=(n_head + n_gate, T // tm),
        in_specs=[
            pl.BlockSpec((tm, D), lambda j, i: (i, 0)),
            pl.BlockSpec((pl.Element(tn), pl.Element(D)), w_rows),
        ],
        out_specs=pl.BlockSpec((tm, tn), lambda j, i: (i, j)),
        out_shape=jax.ShapeDtypeStruct((T, (n_head + n_gate) * tn), BF16),
        scratch_shapes=[pltpu.VMEM((D, tn), BF16)],
        compiler_params=_params(("parallel", "arbitrary")),
        name="inproj",
    )(u, w_in_t)


def _mla_proj_body(cq_ref, ckv_ref, u_ref, wpe_ref, gq_ref, gkv_ref, wq_ref, wk_ref, wv_ref,
                   cm_ref, slo_ref, shi_ref, q_ref, k_ref, v_ref):
    qa = _rms(cq_ref[...].astype(F32), gq_ref[...]).astype(BF16)
    kva = _rms(ckv_ref[...].astype(F32), gkv_ref[...]).astype(BF16)
    c = cm_ref[...]
    s_lo = slo_ref[...]
    s_hi = shi_ref[...]
    kpe_raw = lax.dot_general(u_ref[...], wpe_ref[...].astype(BF16), (((1,), (1,)), ((), ())),
                              preferred_element_type=F32)
    kpe = _rope64(kpe_raw, c, s_lo, s_hi).astype(BF16)
    scale = MLA_QK_DIM ** -0.5 * LOG2_E
    cq = c * scale
    sq_lo = s_lo * scale
    sq_hi = s_hi * scale
    q = jnp.dot(qa, wq_ref[...], preferred_element_type=F32)
    kn = jnp.dot(kva, wk_ref[...], preferred_element_type=F32)
    for h in range(MLA_HEADS):
        b = h * MLA_HEAD_PAD
        q_ref[:, b:b + QK_NOPE] = (q[:, b:b + QK_NOPE] * scale).astype(BF16)
        q_ref[:, b + QK_NOPE:b + MLA_HEAD_PAD] = _rope64(
            q[:, b + QK_NOPE:b + MLA_HEAD_PAD], cq, sq_lo, sq_hi).astype(BF16)
        k_ref[:, b:b + QK_NOPE] = kn[:, h * QK_NOPE:(h + 1) * QK_NOPE].astype(BF16)
        k_ref[:, b + QK_NOPE:b + MLA_HEAD_PAD] = kpe
    v_ref[...] = jnp.dot(kva, wv_ref[...], preferred_element_type=F32).astype(BF16)


def _mla_proj(proj, u, w_in_t, gq, gkv, wq, wk, wv, cm, slo, shi, tm=512):
    T, D = u.shape
    const = lambda shape: pl.BlockSpec(shape, lambda i: (0, 0), pipeline_mode=pl.Buffered(1))
    rows = lambda w: pl.BlockSpec((tm, w), lambda i: (i, 0))
    QW = MLA_HEADS * MLA_HEAD_PAD
    VW = MLA_HEADS * V_HEAD
    return pl.pallas_call(
        _mla_proj_body,
        grid=(T // tm,),
        in_specs=[pl.BlockSpec((tm, Q_LORA), lambda i: (i, COL_CQ // Q_LORA)),
                  pl.BlockSpec((tm, KV_LORA), lambda i: (i, COL_CKV // KV_LORA)),
                  rows(D),
                  pl.BlockSpec((LANES, D), lambda i: (COL_KPE // LANES, 0), pipeline_mode=pl.Buffered(1)),
                  const((1, Q_LORA)), const((1, KV_LORA)),
                  const(wq.shape), const(wk.shape), const(wv.shape),
                  rows(LANES), rows(LANES), rows(LANES)],
        out_specs=[rows(QW), rows(QW), rows(VW)],
        out_shape=[jax.ShapeDtypeStruct((T, QW), BF16), jax.ShapeDtypeStruct((T, QW), BF16),
                   jax.ShapeDtypeStruct((T, VW), BF16)],
        compiler_params=_params(("parallel",)),
        name="mla_proj",
    )(proj, proj, u, w_in_t, gq, gkv, wq, wk, wv, cm, slo, shi)


def _attn_body(q_ref, k_ref, v_ref, o_ref, m_ref, acc_ref, *, tq, tk, nh):
    qi = pl.program_id(2)
    m_ref[...] = jnp.full_like(m_ref, -jnp.inf)
    acc_ref[...] = jnp.zeros_like(acc_ref)
    reps = tk // LANES

    def step(start, diag_off):
        r0 = 0 if diag_off is None else diag_off
        rows = tq - r0
        for hh in range(nh):
            q = q_ref[r0:, hh * MLA_HEAD_PAD:(hh + 1) * MLA_HEAD_PAD]
            k = k_ref[pl.ds(start, tk), hh * MLA_HEAD_PAD:(hh + 1) * MLA_HEAD_PAD]
            v = v_ref[pl.ds(start, tk), hh * V_HEAD:(hh + 1) * V_HEAD]
            s = lax.dot_general(q, k, (((1,), (1,)), ((), ())), preferred_element_type=F32)
            if diag_off is not None:
                rq = lax.broadcasted_iota(jnp.int32, (rows, tk), 0) // CHUNK
                ck = lax.broadcasted_iota(jnp.int32, (rows, tk), 1) // CHUNK
                s = jnp.where(ck <= rq, s, -jnp.inf)
            m_old = m_ref[hh, r0:, :]
            m_new = jnp.maximum(m_old, jnp.max(s, axis=-1, keepdims=True))
            alpha = jnp.exp2(m_old - m_new)
            p = jnp.exp2(s - jnp.concatenate([m_new] * reps, axis=1))
            pv = jnp.dot(p.astype(BF16), jnp.concatenate([v, jnp.ones_like(v)], axis=1),
                         preferred_element_type=F32)
            acc_ref[hh, r0:, :] = jnp.concatenate([alpha, alpha], axis=1) * acc_ref[hh, r0:, :] + pv
            m_ref[hh, r0:, :] = m_new

    def full_steps(kb, carry):
        for j in range(tq // tk):
            step(pl.multiple_of(kb * tq + j * tk, tk), None)
        return carry

    lax.fori_loop(0, qi, full_steps, 0)
    for j in range(tq // tk):
        step(pl.multiple_of(qi * tq + j * tk, tk), j * tk)
    for hh in range(nh):
        o_ref[:, hh * V_HEAD:(hh + 1) * V_HEAD] = (
            acc_ref[hh, :, :V_HEAD] / acc_ref[hh, :, V_HEAD:]).astype(BF16)


def _attention(q, k, v, batch, seq, tq=2048, tk=512, nh=2):
    nq = seq // tq
    return pl.pallas_call(
        functools.partial(_attn_body, tq=tq, tk=tk, nh=nh),
        grid=(batch, MLA_HEADS // nh, nq),
        in_specs=[
            pl.BlockSpec((tq, nh * MLA_HEAD_PAD), lambda b, h, i: (b * nq + i, h)),
            pl.BlockSpec((seq, nh * MLA_HEAD_PAD), lambda b, h, i: (b, h)),
            pl.BlockSpec((seq, nh * V_HEAD), lambda b, h, i: (b, h)),
        ],
        out_specs=pl.BlockSpec((tq, nh * V_HEAD), lambda b, h, i: (b * nq + i, h)),
        out_shape=jax.ShapeDtypeStruct((batch * seq, MLA_HEADS * V_HEAD), BF16),
        scratch_shapes=[pltpu.VMEM((nh, tq, LANES), F32), pltpu.VMEM((nh, tq, 2 * V_HEAD), F32)],
        compiler_params=_params(("parallel", "parallel", "arbitrary")),
        name="mla_attention",
    )(q, k, v)


def _retention_body(lg_ref, q_ref, k_ref, v_ref, rg_ref, cr_ref, sr_ref, g_ref, o_ref, st_ref, *, bc):
    blk = pl.program_id(1)

    @pl.when(blk == 0)
    def _init():
        st_ref[...] = jnp.zeros_like(st_ref)

    n = lax.broadcasted_iota(jnp.int32, (bc, bc), 0)
    m = lax.broadcasted_iota(jnp.int32, (bc, bc), 1)
    dist = jnp.abs(n - m).astype(F32)
    visible = (m // CHUNK) <= (n // CHUNK)
    nl = lax.broadcasted_iota(jnp.int32, (bc, 1), 0).astype(F32)
    c = cr_ref[...]
    s_ = sr_ref[...]
    k_scale = RET_QK_DIM ** -0.5
    ck = c * k_scale
    sk = s_ * k_scale
    for h in range(RET_HEADS):
        lg = lg_ref[h]
        decay = jnp.where(visible, jnp.exp(lg * dist), 0.0)
        qs = slice(h * RET_QK_DIM, (h + 1) * RET_QK_DIM)
        vs = slice(h * RET_V_DIM, (h + 1) * RET_V_DIM)
        q32 = _rope128(q_ref[:, qs].astype(F32), c, s_)
        k32 = _rope128(k_ref[:, qs].astype(F32), ck, sk)
        q = q32.astype(BF16)
        k = k32.astype(BF16)
        v = v_ref[:, vs]
        s = lax.dot_general(q, k, (((1,), (1,)), ((), ())), preferred_element_type=F32) * decay
        qd = (q32 * jnp.exp(lg * (nl + 1.0))).astype(BF16)
        st = st_ref[h]
        o = (jnp.dot(s.astype(BF16), v, preferred_element_type=F32)
             + jnp.dot(qd, st.astype(BF16), preferred_element_type=F32))
        kd = (k32 * jnp.exp(lg * (bc - 1.0 - nl))).astype(BF16)
        c_dec = jnp.exp(jnp.full((1, RET_V_DIM), lg * bc, F32))
        st_ref[h] = st * c_dec + lax.dot_general(
            kd, v, (((0,), (0,)), ((), ())), preferred_element_type=F32)
        mu = jnp.mean(o, axis=-1, keepdims=True)
        oc = o - mu
        var = jnp.mean(oc * oc, axis=-1, keepdims=True)
        y = oc * lax.rsqrt(var + EPS) * g_ref[:, vs]
        rg = rg_ref[:, vs].astype(F32)
        o_ref[:, vs] = (y * (rg * _sigmoid(rg))).astype(BF16)


def _retention(proj, cr, sr, lg, g, batch, seq, bc=256):
    nb = seq // bc
    row = lambda b, i, lg: (b * nb + i, 0)
    grid_spec = pltpu.PrefetchScalarGridSpec(
        num_scalar_prefetch=1,
        grid=(batch, nb),
        in_specs=[
            pl.BlockSpec((bc, RET_QK_W), row),
            pl.BlockSpec((bc, RET_QK_W), lambda b, i, lg: (b * nb + i, 1)),
            pl.BlockSpec((bc, RET_V_W), lambda b, i, lg: (b * nb + i, COL_RV // RET_V_W)),
            pl.BlockSpec((bc, RET_V_W), lambda b, i, lg: (b * nb + i, COL_RG // RET_V_W)),
            pl.BlockSpec((bc, LANES), row),
            pl.BlockSpec((bc, LANES), row),
            pl.BlockSpec((1, RET_V_W), lambda b, i, lg: (0, 0)),
        ],
        out_specs=pl.BlockSpec((bc, RET_V_W), row),
        scratch_shapes=[pltpu.VMEM((RET_HEADS, RET_QK_DIM, RET_V_DIM), F32)],
    )
    return pl.pallas_call(
        functools.partial(_retention_body, bc=bc),
        grid_spec=grid_spec,
        out_shape=jax.ShapeDtypeStruct((batch * seq, RET_V_W), BF16),
        compiler_params=_params(("parallel", "arbitrary")),
        name="retention",
    )(lg, proj, proj, proj, proj, cr, sr, g)


def _merge_body(a_ref, b_ref, wa_ref, wb_ref, ga_ref, gb_ref, o_ref, wab_ref, wbb_ref):
    _cast_weight(wa_ref, wab_ref)
    _cast_weight(wb_ref, wbb_ref)
    ya = jnp.dot(a_ref[...], wab_ref[...], preferred_element_type=F32)
    yb = jnp.dot(b_ref[...], wbb_ref[...], preferred_element_type=F32)
    ga = _sigmoid(ga_ref[...].astype(F32))
    gb = _sigmoid(gb_ref[...].astype(F32))
    o_ref[...] = (ga * ya + gb * yb).astype(BF16)


def _merge(ry, my, w_ret_o, w_mla_o, proj, gate_col, tm=512, tn=1024):
    T, K = ry.shape
    N = w_ret_o.shape[1]
    nj = N // tn
    g0 = gate_col // tn
    wspec = pl.BlockSpec((K, tn), lambda j, i: (0, j), pipeline_mode=pl.Buffered(1))
    return pl.pallas_call(
        _merge_body,
        grid=(nj, T // tm),
        in_specs=[
            pl.BlockSpec((tm, K), lambda j, i: (i, 0)),
            pl.BlockSpec((tm, K), lambda j, i: (i, 0)),
            wspec, wspec,
            pl.BlockSpec((tm, tn), lambda j, i: (i, g0 + j)),
            pl.BlockSpec((tm, tn), lambda j, i: (i, g0 + nj + j)),
        ],
        out_specs=pl.BlockSpec((tm, tn), lambda j, i: (i, j)),
        out_shape=jax.ShapeDtypeStruct((T, N), BF16),
        scratch_shapes=[pltpu.VMEM((K, tn), BF16), pltpu.VMEM((K, tn), BF16)],
        compiler_params=_params(("parallel", "arbitrary")),
        name="gated_merge",
    )(ry, my, w_ret_o, w_mla_o, proj, proj)


def _outproj_body(a_ref, w_ref, x_ref, o_ref, wb_ref):
    _cast_weight(w_ref, wb_ref)
    o_ref[...] = x_ref[...] + jnp.dot(a_ref[...], wb_ref[...], preferred_element_type=F32)


def _outproj(a, w, x, tm=512, tn=1024):
    T, K = a.shape
    N = w.shape[1]
    return pl.pallas_call(
        _outproj_body,
        grid=(N // tn, T // tm),
        in_specs=[
            pl.BlockSpec((tm, K), lambda j, i: (i, 0)),
            pl.BlockSpec((K, tn), lambda j, i: (0, j), pipeline_mode=pl.Buffered(1)),
            pl.BlockSpec((tm, tn), lambda j, i: (i, j)),
        ],
        out_specs=pl.BlockSpec((tm, tn), lambda j, i: (i, j)),
        out_shape=jax.ShapeDtypeStruct((T, N), F32),
        scratch_shapes=[pltpu.VMEM((K, tn), BF16)],
        compiler_params=_params(("parallel", "arbitrary")),
        name="out_proj",
    )(a, w, x)


def _mlp_body(h_ref, g_ref, wu_ref, wd_ref, gf_ref, o_ref, hn_ref):
    f = pl.program_id(1)

    @pl.when(f == 0)
    def _init():
        hn_ref[...] = _rms(h_ref[...], g_ref[...]).astype(BF16)
        o_ref[...] = jnp.zeros_like(o_ref)

    z = jnp.dot(hn_ref[...], wu_ref[...].astype(BF16), preferred_element_type=F32)
    a = jnp.square(jnp.maximum(z, 0.0)).astype(BF16)
    o_ref[...] += jnp.dot(a, wd_ref[...].astype(BF16), preferred_element_type=F32)

    @pl.when(f == pl.num_programs(1) - 1)
    def _finish():
        o_ref[...] = _rms(h_ref[...] + o_ref[...], gf_ref[...])


def _mlp(h, g, w_up, w_down, gf, tm=1024, tf=512):
    T, D = h.shape
    FF = w_up.shape[1]
    return pl.pallas_call(
        _mlp_body,
        grid=(T // tm, FF // tf),
        in_specs=[
            pl.BlockSpec((tm, D), lambda i, f: (i, 0), pipeline_mode=pl.Buffered(1)),
            pl.BlockSpec((1, D), lambda i, f: (0, 0)),
            pl.BlockSpec((D, tf), lambda i, f: (0, f)),
            pl.BlockSpec((tf, D), lambda i, f: (f, 0)),
            pl.BlockSpec((1, D), lambda i, f: (0, 0)),
        ],
        out_specs=pl.BlockSpec((tm, D), lambda i, f: (i, 0)),
        out_shape=jax.ShapeDtypeStruct((T, D), F32),
        scratch_shapes=[pltpu.VMEM((tm, D), BF16)],
        compiler_params=_params(("parallel", "arbitrary")),
        name="mlp_final_norm",
    )(h, g, w_up, w_down, gf)


def _prep_q_weights(w_q_b):
    w = w_q_b.reshape(Q_LORA, MLA_HEADS, MLA_QK_DIM)
    w = jnp.pad(w, ((0, 0), (0, 0), (0, MLA_HEAD_PAD - MLA_QK_DIM)))
    return w.reshape(Q_LORA, MLA_HEADS * MLA_HEAD_PAD).astype(BF16)


def _prep_kv_weights(w_kv_b):
    w = w_kv_b.reshape(KV_LORA, MLA_HEADS, QK_NOPE + V_HEAD)
    wk = w[..., :QK_NOPE].reshape(KV_LORA, MLA_HEADS * QK_NOPE).astype(BF16)
    wv = w[..., QK_NOPE:].reshape(KV_LORA, MLA_HEADS * V_HEAD).astype(BF16)
    return wk, wv


def kernel(x, positions, norm_mix_g, w_in, ret_norm_g, w_ret_o, q_a_norm_g, w_q_b, kv_a_norm_g,
           w_kv_b, w_mla_o, w_out, norm_mlp_g, w_up, w_down, norm_f_g):
    B, S, D = x.shape
    T = B * S
    assert w_in.shape[0] == 1, "single-layer block: the final norm is fused into the MLP kernel"
    log_gamma = jnp.log(1.0 - 2.0 ** (-5.0 - jnp.arange(RET_HEADS, dtype=F32)))
    cr, sr, cm, slo, shi = _rope_tables(positions.reshape(T, 1))
    h = x.reshape(T, D)
    w_in_t = jnp.swapaxes(w_in[0], 0, 1)
    wq = _prep_q_weights(w_q_b[0])
    wk, wv = _prep_kv_weights(w_kv_b[0])
    u = _rmsnorm(h, norm_mix_g[0][None])
    proj = _inproj(u, w_in_t)
    q, k, v = _mla_proj(proj, u, w_in_t, q_a_norm_g[0][None], kv_a_norm_g[0][None],
                        wq, wk, wv, cm, slo, shi)
    my = _attention(q, k, v, B, S)
    ry = _retention(proj, cr, sr, log_gamma, ret_norm_g[0][None], B, S)
    merged = _merge(ry, my, w_ret_o[0], w_mla_o[0], proj, COL_KPE)
    h1 = _outproj(merged, w_out[0], h)
    out = _mlp(h1, norm_mlp_g[0][None], w_up[0], w_down[0], norm_f_g[None])
    return out.reshape(B, S, D)
```

```python
import functools

import jax
import jax.numpy as jnp
from jax import lax
from jax.experimental import pallas as pl
from jax.experimental.pallas import tpu as pltpu

F32 = jnp.float32
BF16 = jnp.bfloat16

EPS = 1e-6
ROPE_THETA = 10000.0
CHUNK = 64

RET_HEADS = 8
RET_QK_DIM = 128
RET_V_DIM = 256
RET_QK_W = RET_HEADS * RET_QK_DIM
RET_V_W = RET_HEADS * RET_V_DIM

MLA_HEADS = 16
Q_LORA = 512
KV_LORA = 512
QK_NOPE = 128
QK_ROPE = 64
V_HEAD = 128
MLA_QK_DIM = QK_NOPE + QK_ROPE
MLA_HEAD_PAD = 256

COL_RV = 2 * RET_QK_W
COL_RG = COL_RV + RET_V_W
COL_CQ = COL_RG + RET_V_W
COL_CKV = COL_CQ + Q_LORA
COL_KPE = COL_CKV + KV_LORA
COL_GRET = COL_KPE + QK_ROPE

LOG2_E = 1.4426950408889634
LANES = 128
VMEM_LIMIT = 56 * 1024 * 1024


def _params(sem):
    return pltpu.CompilerParams(dimension_semantics=sem, vmem_limit_bytes=VMEM_LIMIT)


def _rms(x, g):
    return x * lax.rsqrt(jnp.mean(x * x, axis=-1, keepdims=True) + EPS) * g


def _rope128(t, c, s):
    return t * c + pltpu.roll(t, LANES // 2, 1) * s


def _rope64(t, c, s_lo, s_hi):
    return t * c + pltpu.roll(t, LANES - QK_ROPE // 2, 1) * s_lo + pltpu.roll(t, QK_ROPE // 2, 1) * s_hi


def _sigmoid(x):
    return 1.0 / (1.0 + jnp.exp(-x))


def _norm_rope_body(x_ref, g_ref, pos_ref, inv_ref, u_ref, cs_ref, sn_ref):
    u_ref[...] = _rms(x_ref[...], g_ref[...]).astype(BF16)
    ang = pos_ref[...].astype(F32) * inv_ref[...]
    cs_ref[...] = jnp.cos(ang)
    sn_ref[...] = jnp.sin(ang)


def _norm_rope(x, g, pos, rows=512):
    T, D = x.shape
    half_r = RET_QK_DIM // 2
    half_m = QK_ROPE // 2
    inv_r = ROPE_THETA ** (-jnp.arange(half_r, dtype=F32) / half_r)
    inv_m = ROPE_THETA ** (-jnp.arange(half_m, dtype=F32) / half_m)
    inv = jnp.concatenate([inv_r, inv_m, inv_m])[None]
    tab = pl.BlockSpec((rows, LANES), lambda i: (i, 0))
    return pl.pallas_call(
        _norm_rope_body,
        grid=(T // rows,),
        in_specs=[pl.BlockSpec((rows, D), lambda i: (i, 0)), pl.BlockSpec((1, D), lambda i: (0, 0)),
                  pl.BlockSpec((rows, 1), lambda i: (i, 0)), pl.BlockSpec((1, LANES), lambda i: (0, 0))],
        out_specs=[pl.BlockSpec((rows, D), lambda i: (i, 0)), tab, tab],
        out_shape=[jax.ShapeDtypeStruct((T, D), BF16), jax.ShapeDtypeStruct((T, LANES), F32),
                   jax.ShapeDtypeStruct((T, LANES), F32)],
        compiler_params=_params(("parallel",)),
        name="norm_rope",
    )(x, g, pos, inv)


def _ret_rope_tables(cs, sn):
    lo = lax.broadcasted_iota(jnp.int32, cs.shape, 1) < LANES // 2
    return (jnp.where(lo, cs, pltpu.roll(cs, LANES // 2, 1)),
            jnp.where(lo, -sn, pltpu.roll(sn, LANES // 2, 1)))


def _mla_rope_tables(cs, sn):
    lane = lax.broadcasted_iota(jnp.int32, cs.shape, 1)
    half = QK_ROPE // 2
    c = pltpu.roll(cs, LANES // 2, 1)
    s = pltpu.roll(sn, LANES // 2, 1)
    return (jnp.where(lane < QK_ROPE, c, 0.0), jnp.where(lane < half, -s, 0.0),
            jnp.where((lane >= half) & (lane < QK_ROPE), s, 0.0))


def _cast_weight(w_ref, wb_ref):
    @pl.when(pl.program_id(1) == 0)
    def _cast():
        wb_ref[...] = w_ref[...].astype(BF16)


def _inproj_body(u_ref, wt_ref, o_ref, wb_ref):
    @pl.when(pl.program_id(1) == 0)
    def _cast():
        wb_ref[...] = wt_ref[...].astype(BF16)

    o_ref[...] = lax.dot_general(u_ref[...], wb_ref[...], (((1,), (1,)), ((), ())),
                                 preferred_element_type=F32).astype(BF16)


def _inproj(u, w_in_t, tm=1024, tn=1024):
    T, D = u.shape
    n_head = COL_KPE // tn
    n_gate = (w_in_t.shape[0] - COL_GRET) // tn

    def w_rows(j, i):
        row = jnp.where(j < n_head, j * tn, COL_GRET + (j - n_head) * tn)
        return (pl.multiple_of(row, QK_ROPE), 0)

    return pl.pallas_call(
        _inproj_body,
        grid=(n_head + n_gate, T // tm),
        in_specs=[
            pl.BlockSpec((tm, D), lambda j, i: (i, 0)),
            pl.BlockSpec((pl.Element(tn), pl.Element(D)), w_rows),
        ],
        out_specs=pl.BlockSpec((tm, tn), lambda j, i: (i, j)),
        out_shape=jax.ShapeDtypeStruct((T, (n_head + n_gate) * tn), BF16),
        scratch_shapes=[pltpu.VMEM((tn, D), BF16)],
        compiler_params=_params(("parallel", "arbitrary")),
        name="inproj",
    )(u, w_in_t)


def _mla_proj_body(cq_ref, ckv_ref, u_ref, wpe_ref, gq_ref, gkv_ref, wq_ref, wk_ref, wv_ref,
                   cs_ref, sn_ref, q_ref, k_ref, v_ref):
    qa = _rms(cq_ref[...].astype(F32), gq_ref[...]).astype(BF16)
    kva = _rms(ckv_ref[...].astype(F32), gkv_ref[...]).astype(BF16)
    c, s_lo, s_hi = _mla_rope_tables(cs_ref[...], sn_ref[...])
    kpe_raw = lax.dot_general(u_ref[...], wpe_ref[...].astype(BF16), (((1,), (1,)), ((), ())),
                              preferred_element_type=F32)
    kpe = _rope64(kpe_raw, c, s_lo, s_hi).astype(BF16)
    scale = MLA_QK_DIM ** -0.5 * LOG2_E
    cq = c * scale
    sq_lo = s_lo * scale
    sq_hi = s_hi * scale
    q = jnp.dot(qa, wq_ref[...], preferred_element_type=F32)
    kn = jnp.dot(kva, wk_ref[...], preferred_element_type=F32)
    for h in range(MLA_HEADS):
        b = h * MLA_HEAD_PAD
        q_ref[:, b:b + QK_NOPE] = (q[:, b:b + QK_NOPE] * scale).astype(BF16)
        q_ref[:, b + QK_NOPE:b + MLA_HEAD_PAD] = _rope64(
            q[:, b + QK_NOPE:b + MLA_HEAD_PAD], cq, sq_lo, sq_hi).astype(BF16)
        k_ref[:, b:b + QK_NOPE] = kn[:, h * QK_NOPE:(h + 1) * QK_NOPE].astype(BF16)
        k_ref[:, b + QK_NOPE:b + MLA_HEAD_PAD] = kpe
    v_ref[...] = jnp.dot(kva, wv_ref[...], preferred_element_type=F32).astype(BF16)


def _mla_proj(proj, u, w_in_t, gq, gkv, wq, wk, wv, cs, sn, tm=512):
    T, D = u.shape
    const = lambda shape: pl.BlockSpec(shape, lambda i: (0, 0), pipeline_mode=pl.Buffered(1))
    rows = lambda w: pl.BlockSpec((tm, w), lambda i: (i, 0))
    QW = MLA_HEADS * MLA_HEAD_PAD
    VW = MLA_HEADS * V_HEAD
    return pl.pallas_call(
        _mla_proj_body,
        grid=(T // tm,),
        in_specs=[pl.BlockSpec((tm, Q_LORA), lambda i: (i, COL_CQ // Q_LORA)),
                  pl.BlockSpec((tm, KV_LORA), lambda i: (i, COL_CKV // KV_LORA)),
                  rows(D),
                  pl.BlockSpec((LANES, D), lambda i: (COL_KPE // LANES, 0), pipeline_mode=pl.Buffered(1)),
                  const((1, Q_LORA)), const((1, KV_LORA)),
                  const(wq.shape), const(wk.shape), const(wv.shape),
                  rows(LANES), rows(LANES)],
        out_specs=[rows(QW), rows(QW), rows(VW)],
        out_shape=[jax.ShapeDtypeStruct((T, QW), BF16), jax.ShapeDtypeStruct((T, QW), BF16),
                   jax.ShapeDtypeStruct((T, VW), BF16)],
        compiler_params=_params(("parallel",)),
        name="mla_proj",
    )(proj, proj, u, w_in_t, gq, gkv, wq, wk, wv, cs, sn)


def _attn_body(q_ref, k_ref, v_ref, o_ref, m_ref, acc_ref, *, tq, tk, nh):
    qi = pl.program_id(2)
    m_ref[...] = jnp.full_like(m_ref, -jnp.inf)
    acc_ref[...] = jnp.zeros_like(acc_ref)
    reps = tk // LANES

    def step(start, diag_off):
        r0 = 0 if diag_off is None else diag_off
        rows = tq - r0
        for hh in range(nh):
            q = q_ref[r0:, hh * MLA_HEAD_PAD:(hh + 1) * MLA_HEAD_PAD]
            k = k_ref[pl.ds(start, tk), hh * MLA_HEAD_PAD:(hh + 1) * MLA_HEAD_PAD]
            v = v_ref[pl.ds(start, tk), hh * V_HEAD:(hh + 1) * V_HEAD]
            s = lax.dot_general(q, k, (((1,), (1,)), ((), ())), preferred_element_type=F32)
            if diag_off is not None:
                rq = lax.broadcasted_iota(jnp.int32, (rows, tk), 0) // CHUNK
                ck = lax.broadcasted_iota(jnp.int32, (rows, tk), 1) // CHUNK
                s = jnp.where(ck <= rq, s, -jnp.inf)
            m_old = m_ref[hh, r0:, :]
            m_new = jnp.maximum(m_old, jnp.max(s, axis=-1, keepdims=True))
            alpha = jnp.exp2(m_old - m_new)
            p = jnp.exp2(s - jnp.concatenate([m_new] * reps, axis=1))
            pv = jnp.dot(p.astype(BF16), jnp.concatenate([v, jnp.ones_like(v)], axis=1),
                         preferred_element_type=F32)
            acc_ref[hh, r0:, :] = jnp.concatenate([alpha, alpha], axis=1) * acc_ref[hh, r0:, :] + pv
            m_ref[hh, r0:, :] = m_new

    def full_steps(kb, carry):
        for j in range(tq // tk):
            step(pl.multiple_of(kb * tq + j * tk, tk), None)
        return carry

    lax.fori_loop(0, qi, full_steps, 0)
    for j in range(tq // tk):
        step(pl.multiple_of(qi * tq + j * tk, tk), j * tk)
    for hh in range(nh):
        o_ref[:, hh * V_HEAD:(hh + 1) * V_HEAD] = (
            acc_ref[hh, :, :V_HEAD] / acc_ref[hh, :, V_HEAD:]).astype(BF16)


def _attention(q, k, v, batch, seq, tq=2048, tk=512, nh=2):
    nq = seq // tq
    return pl.pallas_call(
        functools.partial(_attn_body, tq=tq, tk=tk, nh=nh),
        grid=(batch, MLA_HEADS // nh, nq),
        in_specs=[
            pl.BlockSpec((tq, nh * MLA_HEAD_PAD), lambda b, h, i: (b * nq + i, h)),
            pl.BlockSpec((seq, nh * MLA_HEAD_PAD), lambda b, h, i: (b, h)),
            pl.BlockSpec((seq, nh * V_HEAD), lambda b, h, i: (b, h)),
        ],
        out_specs=pl.BlockSpec((tq, nh * V_HEAD), lambda b, h, i: (b * nq + i, h)),
        out_shape=jax.ShapeDtypeStruct((batch * seq, MLA_HEADS * V_HEAD), BF16),
        scratch_shapes=[pltpu.VMEM((nh, tq, LANES), F32), pltpu.VMEM((nh, tq, 2 * V_HEAD), F32)],
        compiler_params=_params(("parallel", "parallel", "arbitrary")),
        name="mla_attention",
    )(q, k, v)


def _retention_body(lg_ref, q_ref, k_ref, v_ref, rg_ref, cs_ref, sn_ref, g_ref, o_ref,
                    st_ref, dec_ref, qdec_ref, kdec_ref, *, bc):
    blk = pl.program_id(1)

    @pl.when(blk == 0)
    def _init():
        st_ref[...] = jnp.zeros_like(st_ref)
        n = lax.broadcasted_iota(jnp.int32, (bc, bc), 0)
        m = lax.broadcasted_iota(jnp.int32, (bc, bc), 1)
        dist = jnp.abs(n - m).astype(F32)
        visible = (m // CHUNK) <= (n // CHUNK)
        nl = lax.broadcasted_iota(jnp.int32, (bc, LANES), 0).astype(F32)
        for h in range(RET_HEADS):
            lg = lg_ref[h]
            dec_ref[h] = jnp.where(visible, jnp.exp(lg * dist), 0.0)
            qdec_ref[h] = jnp.exp(lg * (nl + 1.0))
            kdec_ref[h] = jnp.exp(lg * (bc - 1.0 - nl))

    c, s_ = _ret_rope_tables(cs_ref[...], sn_ref[...])
    k_scale = RET_QK_DIM ** -0.5
    ck = c * k_scale
    sk = s_ * k_scale
    for h in range(RET_HEADS):
        qs = slice(h * RET_QK_DIM, (h + 1) * RET_QK_DIM)
        vs = slice(h * RET_V_DIM, (h + 1) * RET_V_DIM)
        q32 = _rope128(q_ref[:, qs].astype(F32), c, s_)
        k32 = _rope128(k_ref[:, qs].astype(F32), ck, sk)
        q = q32.astype(BF16)
        k = k32.astype(BF16)
        v = v_ref[:, vs]
        s = lax.dot_general(q, k, (((1,), (1,)), ((), ())), preferred_element_type=F32) * dec_ref[h]
        qd = (q32 * qdec_ref[h]).astype(BF16)
        st = st_ref[h]
        o = (jnp.dot(s.astype(BF16), v, preferred_element_type=F32)
             + jnp.dot(qd, st.astype(BF16), preferred_element_type=F32))
        kd = (k32 * kdec_ref[h]).astype(BF16)
        c_dec = jnp.exp(jnp.full((1, RET_V_DIM), lg_ref[h] * bc, F32))
        st_ref[h] = st * c_dec + lax.dot_general(
            kd, v, (((0,), (0,)), ((), ())), preferred_element_type=F32)
        mu = jnp.mean(o, axis=-1, keepdims=True)
        oc = o - mu
        var = jnp.mean(oc * oc, axis=-1, keepdims=True)
        y = oc * lax.rsqrt(var + EPS) * g_ref[:, vs]
        rg = rg_ref[:, vs].astype(F32)
        o_ref[:, vs] = (y * (rg * _sigmoid(rg))).astype(BF16)


def _retention(proj, cs, sn, lg, g, batch, seq, bc=256):
    nb = seq // bc
    row = lambda b, i, lg: (b * nb + i, 0)
    grid_spec = pltpu.PrefetchScalarGridSpec(
        num_scalar_prefetch=1,
        grid=(batch, nb),
        in_specs=[
            pl.BlockSpec((bc, RET_QK_W), row),
            pl.BlockSpec((bc, RET_QK_W), lambda b, i, lg: (b * nb + i, 1)),
            pl.BlockSpec((bc, RET_V_W), lambda b, i, lg: (b * nb + i, COL_RV // RET_V_W)),
            pl.BlockSpec((bc, RET_V_W), lambda b, i, lg: (b * nb + i, COL_RG // RET_V_W)),
            pl.BlockSpec((bc, LANES), row),
            pl.BlockSpec((bc, LANES), row),
            pl.BlockSpec((1, RET_V_W), lambda b, i, lg: (0, 0)),
        ],
        out_specs=pl.BlockSpec((bc, RET_V_W), row),
        scratch_shapes=[pltpu.VMEM((RET_HEADS, RET_QK_DIM, RET_V_DIM), F32),
                        pltpu.VMEM((RET_HEADS, bc, bc), F32),
                        pltpu.VMEM((RET_HEADS, bc, LANES), F32),
                        pltpu.VMEM((RET_HEADS, bc, LANES), F32)],
    )
    return pl.pallas_call(
        functools.partial(_retention_body, bc=bc),
        grid_spec=grid_spec,
        out_shape=jax.ShapeDtypeStruct((batch * seq, RET_V_W), BF16),
        compiler_params=_params(("parallel", "arbitrary")),
        name="retention",
    )(lg, proj, proj, proj, proj, cs, sn, g)


def _merge_body(a_ref, b_ref, wa_ref, wb_ref, ga_ref, gb_ref, o_ref, wab_ref, wbb_ref):
    _cast_weight(wa_ref, wab_ref)
    _cast_weight(wb_ref, wbb_ref)
    ya = jnp.dot(a_ref[...], wab_ref[...], preferred_element_type=F32)
    yb = jnp.dot(b_ref[...], wbb_ref[...], preferred_element_type=F32)
    ga = _sigmoid(ga_ref[...].astype(F32))
    gb = _sigmoid(gb_ref[...].astype(F32))
    o_ref[...] = (ga * ya + gb * yb).astype(BF16)


def _merge(ry, my, w_ret_o, w_mla_o, proj, gate_col, tm=512, tn=1024):
    T, K = ry.shape
    N = w_ret_o.shape[1]
    nj = N // tn
    g0 = gate_col // tn
    wspec = pl.BlockSpec((K, tn), lambda j, i: (0, j), pipeline_mode=pl.Buffered(1))
    return pl.pallas_call(
        _merge_body,
        grid=(nj, T // tm),
        in_specs=[
            pl.BlockSpec((tm, K), lambda j, i: (i, 0)),
            pl.BlockSpec((tm, K), lambda j, i: (i, 0)),
            wspec, wspec,
            pl.BlockSpec((tm, tn), lambda j, i: (i, g0 + j)),
            pl.BlockSpec((tm, tn), lambda j, i: (i, g0 + nj + j)),
        ],
        out_specs=pl.BlockSpec((tm, tn), lambda j, i: (i, j)),
        out_shape=jax.ShapeDtypeStruct((T, N), BF16),
        scratch_shapes=[pltpu.VMEM((K, tn), BF16), pltpu.VMEM((K, tn), BF16)],
        compiler_params=_params(("parallel", "arbitrary")),
        name="gated_merge",
    )(ry, my, w_ret_o, w_mla_o, proj, proj)


def _outproj_body(a_ref, w_ref, x_ref, o_ref, wb_ref):
    _cast_weight(w_ref, wb_ref)
    o_ref[...] = x_ref[...] + jnp.dot(a_ref[...], wb_ref[...], preferred_element_type=F32)


def _outproj(a, w, x, tm=512, tn=1024):
    T, K = a.shape
    N = w.shape[1]
    return pl.pallas_call(
        _outproj_body,
        grid=(N // tn, T // tm),
        in_specs=[
            pl.BlockSpec((tm, K), lambda j, i: (i, 0)),
            pl.BlockSpec((K, tn), lambda j, i: (0, j), pipeline_mode=pl.Buffered(1)),
            pl.BlockSpec((tm, tn), lambda j, i: (i, j)),
        ],
        out_specs=pl.BlockSpec((tm, tn), lambda j, i: (i, j)),
        out_shape=jax.ShapeDtypeStruct((T, N), F32),
        scratch_shapes=[pltpu.VMEM((K, tn), BF16)],
        compiler_params=_params(("parallel", "arbitrary")),
        name="out_proj",
    )(a, w, x)


def _mlp_body(h_ref, g_ref, wu_ref, wd_ref, gf_ref, o_ref, hn_ref):
    f = pl.program_id(1)

    @pl.when(f == 0)
    def _init():
        hn_ref[...] = _rms(h_ref[...], g_ref[...]).astype(BF16)
        o_ref[...] = jnp.zeros_like(o_ref)

    z = jnp.dot(hn_ref[...], wu_ref[...].astype(BF16), preferred_element_type=F32)
    a = jnp.square(jnp.maximum(z, 0.0)).astype(BF16)
    o_ref[...] += jnp.dot(a, wd_ref[...].astype(BF16), preferred_element_type=F32)

    @pl.when(f == pl.num_programs(1) - 1)
    def _finish():
        o_ref[...] = _rms(h_ref[...] + o_ref[...], gf_ref[...])


def _mlp(h, g, w_up, w_down, gf, tm=1024, tf=512):
    T, D = h.shape
    FF = w_up.shape[1]
    return pl.pallas_call(
        _mlp_body,
        grid=(T // tm, FF // tf),
        in_specs=[
            pl.BlockSpec((tm, D), lambda i, f: (i, 0), pipeline_mode=pl.Buffered(1)),
            pl.BlockSpec((1, D), lambda i, f: (0, 0)),
            pl.BlockSpec((D, tf), lambda i, f: (0, f)),
            pl.BlockSpec((tf, D), lambda i, f: (f, 0)),
            pl.BlockSpec((1, D), lambda i, f: (0, 0)),
        ],
        out_specs=pl.BlockSpec((tm, D), lambda i, f: (i, 0)),
        out_shape=jax.ShapeDtypeStruct((T, D), F32),
        scratch_shapes=[pltpu.VMEM((tm, D), BF16)],
        compiler_params=_params(("parallel", "arbitrary")),
        name="mlp_final_norm",
    )(h, g, w_up, w_down, gf)


def _prep_q_weights(w_q_b):
    w = w_q_b.reshape(Q_LORA, MLA_HEADS, MLA_QK_DIM)
    w = jnp.pad(w, ((0, 0), (0, 0), (0, MLA_HEAD_PAD - MLA_QK_DIM)))
    return w.reshape(Q_LORA, MLA_HEADS * MLA_HEAD_PAD).astype(BF16)


def _prep_kv_weights(w_kv_b):
    w = w_kv_b.reshape(KV_LORA, MLA_HEADS, QK_NOPE + V_HEAD)
    wk = w[..., :QK_NOPE].reshape(KV_LORA, MLA_HEADS * QK_NOPE).astype(BF16)
    wv = w[..., QK_NOPE:].reshape(KV_LORA, MLA_HEADS * V_HEAD).astype(BF16)
    return wk, wv


def kernel(x, positions, norm_mix_g, w_in, ret_norm_g, w_ret_o, q_a_norm_g, w_q_b, kv_a_norm_g,
           w_kv_b, w_mla_o, w_out, norm_mlp_g, w_up, w_down, norm_f_g):
    B, S, D = x.shape
    T = B * S
    assert w_in.shape[0] == 1, "single-layer block: the final norm is fused into the MLP kernel"
    log_gamma = jnp.log(1.0 - 2.0 ** (-5.0 - jnp.arange(RET_HEADS, dtype=F32)))
    h = x.reshape(T, D)
    w_in_t = jnp.swapaxes(w_in[0], 0, 1)
    wq = _prep_q_weights(w_q_b[0])
    wk, wv = _prep_kv_weights(w_kv_b[0])
    u, cs, sn = _norm_rope(h, norm_mix_g[0][None], positions.reshape(T, 1))
    proj = _inproj(u, w_in_t)
    q, k, v = _mla_proj(proj, u, w_in_t, q_a_norm_g[0][None], kv_a_norm_g[0][None],
                        wq, wk, wv, cs, sn)
    my = _attention(q, k, v, B, S)
    ry = _retention(proj, cs, sn, log_gamma, ret_norm_g[0][None], B, S)
    merged = _merge(ry, my, w_ret_o[0], w_mla_o[0], proj, COL_KPE)
    h1 = _outproj(merged, w_out[0], h)
    out = _mlp(h1, norm_mlp_g[0][None], w_up[0], w_down[0], norm_f_g[None])
    return out.reshape(B, S, D)
```

```python
import functools

import jax
import jax.numpy as jnp
from jax import lax
from jax.experimental import pallas as pl
from jax.experimental.pallas import tpu as pltpu

F32 = jnp.float32
BF16 = jnp.bfloat16

EPS = 1e-6
ROPE_THETA = 10000.0
CHUNK = 64

RET_HEADS = 8
RET_QK_DIM = 128
RET_V_DIM = 256
RET_QK_W = RET_HEADS * RET_QK_DIM
RET_V_W = RET_HEADS * RET_V_DIM

MLA_HEADS = 16
Q_LORA = 512
KV_LORA = 512
QK_NOPE = 128
QK_ROPE = 64
V_HEAD = 128
MLA_QK_DIM = QK_NOPE + QK_ROPE
MLA_HEAD_PAD = 256

COL_RV = 2 * RET_QK_W
COL_RG = COL_RV + RET_V_W
COL_CQ = COL_RG + RET_V_W
COL_CKV = COL_CQ + Q_LORA
COL_KPE = COL_CKV + KV_LORA
COL_GRET = COL_KPE + QK_ROPE

LOG2_E = 1.4426950408889634
LANES = 128
VMEM_LIMIT = 56 * 1024 * 1024
VMEM_LIMIT_BIG = 62 * 1024 * 1024


def _params(sem, vmem_limit=VMEM_LIMIT):
    return pltpu.CompilerParams(dimension_semantics=sem, vmem_limit_bytes=vmem_limit)


def _rms(x, g):
    return x * lax.rsqrt(jnp.mean(x * x, axis=-1, keepdims=True) + EPS) * g


def _rope128(t, c, s):
    return t * c + pltpu.roll(t, LANES // 2, 1) * s


def _rope64(t, c, s_lo, s_hi):
    return t * c + pltpu.roll(t, LANES - QK_ROPE // 2, 1) * s_lo + pltpu.roll(t, QK_ROPE // 2, 1) * s_hi


def _sigmoid(x):
    return 1.0 / (1.0 + jnp.exp(-x))


def _norm_rope_body(x_ref, g_ref, pos_ref, inv_ref, u_ref, cs_ref, sn_ref):
    u_ref[...] = _rms(x_ref[...], g_ref[...]).astype(BF16)
    ang = pos_ref[...].astype(F32) * inv_ref[...]
    cs_ref[...] = jnp.cos(ang)
    sn_ref[...] = jnp.sin(ang)


def _norm_rope(x, g, pos, rows=512):
    T, D = x.shape
    half_r = RET_QK_DIM // 2
    half_m = QK_ROPE // 2
    inv_r = ROPE_THETA ** (-jnp.arange(half_r, dtype=F32) / half_r)
    inv_m = ROPE_THETA ** (-jnp.arange(half_m, dtype=F32) / half_m)
    inv = jnp.concatenate([inv_r, inv_m, inv_m])[None]
    tab = pl.BlockSpec((rows, LANES), lambda i: (i, 0))
    return pl.pallas_call(
        _norm_rope_body,
        grid=(T // rows,),
        in_specs=[pl.BlockSpec((rows, D), lambda i: (i, 0)), pl.BlockSpec((1, D), lambda i: (0, 0)),
                  pl.BlockSpec((rows, 1), lambda i: (i, 0)), pl.BlockSpec((1, LANES), lambda i: (0, 0))],
        out_specs=[pl.BlockSpec((rows, D), lambda i: (i, 0)), tab, tab],
        out_shape=[jax.ShapeDtypeStruct((T, D), BF16), jax.ShapeDtypeStruct((T, LANES), F32),
                   jax.ShapeDtypeStruct((T, LANES), F32)],
        compiler_params=_params(("parallel",)),
        name="norm_rope",
    )(x, g, pos, inv)


def _ret_rope_tables(cs, sn):
    lo = lax.broadcasted_iota(jnp.int32, cs.shape, 1) < LANES // 2
    return (jnp.where(lo, cs, pltpu.roll(cs, LANES // 2, 1)),
            jnp.where(lo, -sn, pltpu.roll(sn, LANES // 2, 1)))


def _mla_rope_tables(cs, sn):
    lane = lax.broadcasted_iota(jnp.int32, cs.shape, 1)
    half = QK_ROPE // 2
    c = pltpu.roll(cs, LANES // 2, 1)
    s = pltpu.roll(sn, LANES // 2, 1)
    return (jnp.where(lane < QK_ROPE, c, 0.0), jnp.where(lane < half, -s, 0.0),
            jnp.where((lane >= half) & (lane < QK_ROPE), s, 0.0))


def _cast_weight(w_ref, wb_ref):
    @pl.when(pl.program_id(1) == 0)
    def _cast():
        wb_ref[...] = w_ref[...].astype(BF16)


def _inproj_body(u_ref, wt_ref, o_ref, wb_ref):
    @pl.when(pl.program_id(1) == 0)
    def _cast():
        wb_ref[...] = wt_ref[...].astype(BF16)

    o_ref[...] = lax.dot_general(u_ref[...], wb_ref[...], (((1,), (1,)), ((), ())),
                                 preferred_element_type=F32).astype(BF16)


def _inproj(u, w_in_t, tm=1024, tn=1024):
    T, D = u.shape
    n_head = COL_KPE // tn
    n_gate = (w_in_t.shape[0] - COL_GRET) // tn

    def w_rows(j, i):
        row = jnp.where(j < n_head, j * tn, COL_GRET + (j - n_head) * tn)
        return (pl.multiple_of(row, QK_ROPE), 0)

    return pl.pallas_call(
        _inproj_body,
        grid=(n_head + n_gate, T // tm),
        in_specs=[
            pl.BlockSpec((tm, D), lambda j, i: (i, 0)),
            pl.BlockSpec((pl.Element(tn), pl.Element(D)), w_rows),
        ],
        out_specs=pl.BlockSpec((tm, tn), lambda j, i: (i, j)),
        out_shape=jax.ShapeDtypeStruct((T, (n_head + n_gate) * tn), BF16),
        scratch_shapes=[pltpu.VMEM((tn, D), BF16)],
        compiler_params=_params(("parallel", "arbitrary")),
        name="inproj",
    )(u, w_in_t)


def _mla_part(cq_ref, ckv_ref, u_ref, wpe_ref, gq_ref, gkv_ref, wq_ref, wk_ref, wv_ref, cs, sn,
              q_ref, kn_ref, kpe_ref, v_ref):
    qa = _rms(cq_ref[...].astype(F32), gq_ref[...]).astype(BF16)
    kva = _rms(ckv_ref[...].astype(F32), gkv_ref[...]).astype(BF16)
    c, s_lo, s_hi = _mla_rope_tables(cs, sn)
    kpe_raw = lax.dot_general(u_ref[...], wpe_ref[...].astype(BF16), (((1,), (1,)), ((), ())),
                              preferred_element_type=F32)
    kpe_ref[...] = _rope64(kpe_raw, c, s_lo, s_hi).astype(BF16)
    scale = MLA_QK_DIM ** -0.5 * LOG2_E
    cq = c * scale
    sq_lo = s_lo * scale
    sq_hi = s_hi * scale
    q = jnp.dot(qa, wq_ref[...], preferred_element_type=F32)
    for h in range(MLA_HEADS):
        b = h * MLA_HEAD_PAD
        q_ref[:, b:b + QK_NOPE] = (q[:, b:b + QK_NOPE] * scale).astype(BF16)
        q_ref[:, b + QK_NOPE:b + MLA_HEAD_PAD] = _rope64(
            q[:, b + QK_NOPE:b + MLA_HEAD_PAD], cq, sq_lo, sq_hi).astype(BF16)
    kn_ref[...] = jnp.dot(kva, wk_ref[...], preferred_element_type=F32).astype(BF16)
    v_ref[...] = jnp.dot(kva, wv_ref[...], preferred_element_type=F32).astype(BF16)


def _ret_tables_init(lg_ref, st_ref, dec_ref, qdec_ref, kdec_ref, bc):
    st_ref[...] = jnp.zeros_like(st_ref)
    n = lax.broadcasted_iota(jnp.int32, (bc, bc), 0)
    m = lax.broadcasted_iota(jnp.int32, (bc, bc), 1)
    dist = jnp.abs(n - m).astype(F32)
    visible = (m // CHUNK) <= (n // CHUNK)
    nl = lax.broadcasted_iota(jnp.int32, (bc, LANES), 0).astype(F32)
    for h in range(RET_HEADS):
        lg = lg_ref[h]
        dec_ref[h] = jnp.where(visible, jnp.exp(lg * dist), 0.0)
        qdec_ref[h] = jnp.exp(lg * (nl + 1.0))
        kdec_ref[h] = jnp.exp(lg * (bc - 1.0 - nl))


def _ret_block(rows, lg_ref, q_ref, k_ref, v_ref, rg_ref, cs, sn, g_ref, o_ref,
               st_ref, dec_ref, qdec_ref, kdec_ref, bc):
    c, s_ = _ret_rope_tables(cs, sn)
    k_scale = RET_QK_DIM ** -0.5
    ck = c * k_scale
    sk = s_ * k_scale
    for h in range(RET_HEADS):
        qs = slice(h * RET_QK_DIM, (h + 1) * RET_QK_DIM)
        vs = slice(h * RET_V_DIM, (h + 1) * RET_V_DIM)
        q32 = _rope128(q_ref[rows, qs].astype(F32), c, s_)
        k32 = _rope128(k_ref[rows, qs].astype(F32), ck, sk)
        q = q32.astype(BF16)
        k = k32.astype(BF16)
        v = v_ref[rows, vs]
        s = lax.dot_general(q, k, (((1,), (1,)), ((), ())), preferred_element_type=F32) * dec_ref[h]
        qd = (q32 * qdec_ref[h]).astype(BF16)
        st = st_ref[h]
        o = (jnp.dot(s.astype(BF16), v, preferred_element_type=F32)
             + jnp.dot(qd, st.astype(BF16), preferred_element_type=F32))
        kd = (k32 * kdec_ref[h]).astype(BF16)
        c_dec = jnp.exp(jnp.full((1, RET_V_DIM), lg_ref[h] * bc, F32))
        st_ref[h] = st * c_dec + lax.dot_general(
            kd, v, (((0,), (0,)), ((), ())), preferred_element_type=F32)
        mu = jnp.mean(o, axis=-1, keepdims=True)
        oc = o - mu
        var = jnp.mean(oc * oc, axis=-1, keepdims=True)
        y = oc * lax.rsqrt(var + EPS) * g_ref[:, vs]
        rg = rg_ref[rows, vs].astype(F32)
        o_ref[rows, vs] = (y * (rg * _sigmoid(rg))).astype(BF16)


def _branches_body(lg_ref, cq_ref, ckv_ref, u_ref, wpe_ref, gq_ref, gkv_ref, wq_ref, wk_ref, wv_ref,
                   cs_ref, sn_ref, rq_ref, rk_ref, rv_ref, rg_ref, gr_ref,
                   q_ref, kn_ref, kpe_ref, v_ref, ry_ref, st_ref, dec_ref, qdec_ref, kdec_ref,
                   *, tm, bc, tiles_per_seq):
    @pl.when(pl.program_id(0) % tiles_per_seq == 0)
    def _init():
        _ret_tables_init(lg_ref, st_ref, dec_ref, qdec_ref, kdec_ref, bc)

    _mla_part(cq_ref, ckv_ref, u_ref, wpe_ref, gq_ref, gkv_ref, wq_ref, wk_ref, wv_ref,
              cs_ref[...], sn_ref[...], q_ref, kn_ref, kpe_ref, v_ref)
    for blk in range(tm // bc):
        rows = slice(blk * bc, (blk + 1) * bc)
        _ret_block(rows, lg_ref, rq_ref, rk_ref, rv_ref, rg_ref, cs_ref[rows, :], sn_ref[rows, :],
                   gr_ref, ry_ref, st_ref, dec_ref, qdec_ref, kdec_ref, bc)


def _branches(proj, u, w_in_t, gq, gkv, wq, wk, wv, cs, sn, lg, g_ret, seq, tm=512, bc=256):
    T, D = u.shape
    const = lambda shape: pl.BlockSpec(shape, lambda i, lg: (0, 0), pipeline_mode=pl.Buffered(1))
    rows = lambda w, col=0: pl.BlockSpec((tm, w), lambda i, lg: (i, col))
    QW = MLA_HEADS * MLA_HEAD_PAD
    KW = MLA_HEADS * QK_NOPE
    VW = MLA_HEADS * V_HEAD
    grid_spec = pltpu.PrefetchScalarGridSpec(
        num_scalar_prefetch=1,
        grid=(T // tm,),
        in_specs=[rows(Q_LORA, COL_CQ // Q_LORA), rows(KV_LORA, COL_CKV // KV_LORA), rows(D),
                  pl.BlockSpec((LANES, D), lambda i, lg: (COL_KPE // LANES, 0),
                               pipeline_mode=pl.Buffered(1)),
                  const((1, Q_LORA)), const((1, KV_LORA)),
                  const(wq.shape), const(wk.shape), const(wv.shape),
                  rows(LANES), rows(LANES),
                  rows(RET_QK_W, 0), rows(RET_QK_W, 1),
                  rows(RET_V_W, COL_RV // RET_V_W), rows(RET_V_W, COL_RG // RET_V_W),
                  const((1, RET_V_W))],
        out_specs=[rows(QW), rows(KW), rows(LANES), rows(VW), rows(RET_V_W)],
        scratch_shapes=[pltpu.VMEM((RET_HEADS, RET_QK_DIM, RET_V_DIM), F32),
                        pltpu.VMEM((RET_HEADS, bc, bc), F32),
                        pltpu.VMEM((RET_HEADS, bc, LANES), F32),
                        pltpu.VMEM((RET_HEADS, bc, LANES), F32)],
    )
    return pl.pallas_call(
        functools.partial(_branches_body, tm=tm, bc=bc, tiles_per_seq=seq // tm),
        grid_spec=grid_spec,
        out_shape=[jax.ShapeDtypeStruct((T, QW), BF16), jax.ShapeDtypeStruct((T, KW), BF16),
                   jax.ShapeDtypeStruct((T, LANES), BF16), jax.ShapeDtypeStruct((T, VW), BF16),
                   jax.ShapeDtypeStruct((T, RET_V_W), BF16)],
        compiler_params=_params(("arbitrary",), VMEM_LIMIT_BIG),
        name="mla_proj_retention",
    )(lg, proj, proj, u, w_in_t, gq, gkv, wq, wk, wv, cs, sn, proj, proj, proj, proj, g_ret)


def _attn_body(q_ref, kn_ref, kpe_ref, v_ref, o_ref, m_ref, acc_ref, *, tq, tk, nh):
    qi = pl.program_id(2)
    m_ref[...] = jnp.full_like(m_ref, -jnp.inf)
    acc_ref[...] = jnp.zeros_like(acc_ref)
    reps = tk // LANES

    def step(start, diag_off):
        r0 = 0 if diag_off is None else diag_off
        rows = tq - r0
        for hh in range(nh):
            q = q_ref[r0:, hh * MLA_HEAD_PAD:(hh + 1) * MLA_HEAD_PAD]
            k = jnp.concatenate([kn_ref[pl.ds(start, tk), hh * QK_NOPE:(hh + 1) * QK_NOPE],
                                 kpe_ref[pl.ds(start, tk), :]], axis=1)
            v = v_ref[pl.ds(start, tk), hh * V_HEAD:(hh + 1) * V_HEAD]
            s = lax.dot_general(q, k, (((1,), (1,)), ((), ())), preferred_element_type=F32)
            if diag_off is not None:
                rq = lax.broadcasted_iota(jnp.int32, (rows, tk), 0) // CHUNK
                ck = lax.broadcasted_iota(jnp.int32, (rows, tk), 1) // CHUNK
                s = jnp.where(ck <= rq, s, -jnp.inf)
            m_old = m_ref[hh, r0:, :]
            m_new = jnp.maximum(m_old, jnp.max(s, axis=-1, keepdims=True))
            alpha = jnp.exp2(m_old - m_new)
            p = jnp.exp2(s - jnp.concatenate([m_new] * reps, axis=1))
            pv = jnp.dot(p.astype(BF16), jnp.concatenate([v, jnp.ones_like(v)], axis=1),
                         preferred_element_type=F32)
            acc_ref[hh, r0:, :] = jnp.concatenate([alpha, alpha], axis=1) * acc_ref[hh, r0:, :] + pv
            m_ref[hh, r0:, :] = m_new

    def full_steps(kb, carry):
        for j in range(tq // tk):
            step(pl.multiple_of(kb * tq + j * tk, tk), None)
        return carry

    lax.fori_loop(0, qi, full_steps, 0)
    for j in range(tq // tk):
        step(pl.multiple_of(qi * tq + j * tk, tk), j * tk)
    for hh in range(nh):
        o_ref[:, hh * V_HEAD:(hh + 1) * V_HEAD] = (
            acc_ref[hh, :, :V_HEAD] / acc_ref[hh, :, V_HEAD:]).astype(BF16)


def _attention(q, kn, kpe, v, batch, seq, tq=2048, tk=512, nh=2):
    nq = seq // tq
    return pl.pallas_call(
        functools.partial(_attn_body, tq=tq, tk=tk, nh=nh),
        grid=(batch, MLA_HEADS // nh, nq),
        in_specs=[
            pl.BlockSpec((tq, nh * MLA_HEAD_PAD), lambda b, h, i: (b * nq + i, h)),
            pl.BlockSpec((seq, nh * QK_NOPE), lambda b, h, i: (b, h)),
            pl.BlockSpec((seq, LANES), lambda b, h, i: (b, 0)),
            pl.BlockSpec((seq, nh * V_HEAD), lambda b, h, i: (b, h)),
        ],
        out_specs=pl.BlockSpec((tq, nh * V_HEAD), lambda b, h, i: (b * nq + i, h)),
        out_shape=jax.ShapeDtypeStruct((batch * seq, MLA_HEADS * V_HEAD), BF16),
        scratch_shapes=[pltpu.VMEM((nh, tq, LANES), F32), pltpu.VMEM((nh, tq, 2 * V_HEAD), F32)],
        compiler_params=_params(("parallel", "parallel", "arbitrary")),
        name="mla_attention",
    )(q, kn, kpe, v)


def _merge_body(a_ref, b_ref, wa_ref, wb_ref, ga_ref, gb_ref, o_ref, wab_ref, wbb_ref):
    _cast_weight(wa_ref, wab_ref)
    _cast_weight(wb_ref, wbb_ref)
    ya = jnp.dot(a_ref[...], wab_ref[...], preferred_element_type=F32)
    yb = jnp.dot(b_ref[...], wbb_ref[...], preferred_element_type=F32)
    ga = _sigmoid(ga_ref[...].astype(F32))
    gb = _sigmoid(gb_ref[...].astype(F32))
    o_ref[...] = (ga * ya + gb * yb).astype(BF16)


def _merge(ry, my, w_ret_o, w_mla_o, proj, gate_col, tm=512, tn=1024):
    T, K = ry.shape
    N = w_ret_o.shape[1]
    nj = N // tn
    g0 = gate_col // tn
    wspec = pl.BlockSpec((K, tn), lambda j, i: (0, j), pipeline_mode=pl.Buffered(1))
    return pl.pallas_call(
        _merge_body,
        grid=(nj, T // tm),
        in_specs=[
            pl.BlockSpec((tm, K), lambda j, i: (i, 0)),
            pl.BlockSpec((tm, K), lambda j, i: (i, 0)),
            wspec, wspec,
            pl.BlockSpec((tm, tn), lambda j, i: (i, g0 + j)),
            pl.BlockSpec((tm, tn), lambda j, i: (i, g0 + nj + j)),
        ],
        out_specs=pl.BlockSpec((tm, tn), lambda j, i: (i, j)),
        out_shape=jax.ShapeDtypeStruct((T, N), BF16),
        scratch_shapes=[pltpu.VMEM((K, tn), BF16), pltpu.VMEM((K, tn), BF16)],
        compiler_params=_params(("parallel", "arbitrary")),
        name="gated_merge",
    )(ry, my, w_ret_o, w_mla_o, proj, proj)


def _outproj_body(a_ref, w_ref, x_ref, o_ref, wb_ref):
    _cast_weight(w_ref, wb_ref)
    o_ref[...] = x_ref[...] + jnp.dot(a_ref[...], wb_ref[...], preferred_element_type=F32)


def _outproj(a, w, x, tm=512, tn=1024):
    T, K = a.shape
    N = w.shape[1]
    return pl.pallas_call(
        _outproj_body,
        grid=(N // tn, T // tm),
        in_specs=[
            pl.BlockSpec((tm, K), lambda j, i: (i, 0)),
            pl.BlockSpec((K, tn), lambda j, i: (0, j), pipeline_mode=pl.Buffered(1)),
            pl.BlockSpec((tm, tn), lambda j, i: (i, j)),
        ],
        out_specs=pl.BlockSpec((tm, tn), lambda j, i: (i, j)),
        out_shape=jax.ShapeDtypeStruct((T, N), F32),
        scratch_shapes=[pltpu.VMEM((K, tn), BF16)],
        compiler_params=_params(("parallel", "arbitrary")),
        name="out_proj",
    )(a, w, x)


def _mlp_body(h_ref, g_ref, wu_ref, wd_ref, gf_ref, o_ref, hn_ref):
    f = pl.program_id(1)

    @pl.when(f == 0)
    def _init():
        hn_ref[...] = _rms(h_ref[...], g_ref[...]).astype(BF16)
        o_ref[...] = jnp.zeros_like(o_ref)

    z = jnp.dot(hn_ref[...], wu_ref[...].astype(BF16), preferred_element_type=F32)
    a = jnp.square(jnp.maximum(z, 0.0)).astype(BF16)
    o_ref[...] += jnp.dot(a, wd_ref[...].astype(BF16), preferred_element_type=F32)

    @pl.when(f == pl.num_programs(1) - 1)
    def _finish():
        o_ref[...] = _rms(h_ref[...] + o_ref[...], gf_ref[...])


def _mlp(h, g, w_up, w_down, gf, tm=1024, tf=512):
    T, D = h.shape
    FF = w_up.shape[1]
    return pl.pallas_call(
        _mlp_body,
        grid=(T // tm, FF // tf),
        in_specs=[
            pl.BlockSpec((tm, D), lambda i, f: (i, 0)),
            pl.BlockSpec((1, D), lambda i, f: (0, 0)),
            pl.BlockSpec((D, tf), lambda i, f: (0, f)),
            pl.BlockSpec((tf, D), lambda i, f: (f, 0)),
            pl.BlockSpec((1, D), lambda i, f: (0, 0)),
        ],
        out_specs=pl.BlockSpec((tm, D), lambda i, f: (i, 0)),
        out_shape=jax.ShapeDtypeStruct((T, D), F32),
        scratch_shapes=[pltpu.VMEM((tm, D), BF16)],
        compiler_params=_params(("parallel", "arbitrary"), VMEM_LIMIT_BIG),
        name="mlp_final_norm",
    )(h, g, w_up, w_down, gf)


def _prep_q_weights(w_q_b):
    w = w_q_b.reshape(Q_LORA, MLA_HEADS, MLA_QK_DIM)
    w = jnp.pad(w, ((0, 0), (0, 0), (0, MLA_HEAD_PAD - MLA_QK_DIM)))
    return w.reshape(Q_LORA, MLA_HEADS * MLA_HEAD_PAD).astype(BF16)


def _prep_kv_weights(w_kv_b):
    w = w_kv_b.reshape(KV_LORA, MLA_HEADS, QK_NOPE + V_HEAD)
    wk = w[..., :QK_NOPE].reshape(KV_LORA, MLA_HEADS * QK_NOPE).astype(BF16)
    wv = w[..., QK_NOPE:].reshape(KV_LORA, MLA_HEADS * V_HEAD).astype(BF16)
    return wk, wv


def kernel(x, positions, norm_mix_g, w_in, ret_norm_g, w_ret_o, q_a_norm_g, w_q_b, kv_a_norm_g,
           w_kv_b, w_mla_o, w_out, norm_mlp_g, w_up, w_down, norm_f_g):
    B, S, D = x.shape
    T = B * S
    assert w_in.shape[0] == 1, "single-layer block: the final norm is fused into the MLP kernel"
    log_gamma = jnp.log(1.0 - 2.0 ** (-5.0 - jnp.arange(RET_HEADS, dtype=F32)))
    h = x.reshape(T, D)
    w_in_t = jnp.swapaxes(w_in[0], 0, 1)
    wq = _prep_q_weights(w_q_b[0])
    wk, wv = _prep_kv_weights(w_kv_b[0])
    u, cs, sn = _norm_rope(h, norm_mix_g[0][None], positions.reshape(T, 1))
    proj = _inproj(u, w_in_t)
    q, kn, kpe, v, ry = _branches(proj, u, w_in_t, q_a_norm_g[0][None], kv_a_norm_g[0][None],
                            wq, wk, wv, cs, sn, log_gamma, ret_norm_g[0][None], S)
    my = _attention(q, kn, kpe, v, B, S)
    merged = _merge(ry, my, w_ret_o[0], w_mla_o[0], proj, COL_KPE)
    h1 = _outproj(merged, w_out[0], h)
    out = _mlp(h1, norm_mlp_g[0][None], w_up[0], w_down[0], norm_f_g[None])
    return out.reshape(B, S, D)
```

```python
import functools

import jax
import jax.numpy as jnp
from jax import lax
from jax.experimental import pallas as pl
from jax.experimental.pallas import tpu as pltpu

F32 = jnp.float32
BF16 = jnp.bfloat16

EPS = 1e-6
ROPE_THETA = 10000.0
CHUNK = 64

RET_HEADS = 8
RET_QK_DIM = 128
RET_V_DIM = 256
RET_QK_W = RET_HEADS * RET_QK_DIM
RET_V_W = RET_HEADS * RET_V_DIM

MLA_HEADS = 16
Q_LORA = 512
KV_LORA = 512
QK_NOPE = 128
QK_ROPE = 64
V_HEAD = 128
MLA_QK_DIM = QK_NOPE + QK_ROPE
MLA_HEAD_PAD = 256

COL_RV = 2 * RET_QK_W
COL_RG = COL_RV + RET_V_W
COL_CQ = COL_RG + RET_V_W
COL_CKV = COL_CQ + Q_LORA
COL_KPE = COL_CKV + KV_LORA
COL_GRET = COL_KPE + QK_ROPE

LOG2_E = 1.4426950408889634
LANES = 128
VMEM_LIMIT = 56 * 1024 * 1024
VMEM_LIMIT_BIG = 62 * 1024 * 1024


def _params(sem, vmem_limit=VMEM_LIMIT):
    return pltpu.CompilerParams(dimension_semantics=sem, vmem_limit_bytes=vmem_limit)


def _rms(x, g):
    return x * lax.rsqrt(jnp.mean(x * x, axis=-1, keepdims=True) + EPS) * g


def _rope128(t, c, s):
    return t * c + pltpu.roll(t, LANES // 2, 1) * s


def _rope64(t, c, s_lo, s_hi):
    return t * c + pltpu.roll(t, LANES - QK_ROPE // 2, 1) * s_lo + pltpu.roll(t, QK_ROPE // 2, 1) * s_hi


def _sigmoid(x):
    return 1.0 / (1.0 + jnp.exp(-x))


def _norm_rope_body(x_ref, g_ref, pos_ref, inv_ref, wt_ref, u_ref, cs_ref, sn_ref, lat_ref, wb_ref):
    @pl.when(pl.program_id(0) == 0)
    def _cast():
        wb_ref[...] = wt_ref[...].astype(BF16)

    u = _rms(x_ref[...], g_ref[...]).astype(BF16)
    u_ref[...] = u
    ang = pos_ref[...].astype(F32) * inv_ref[...]
    cs_ref[...] = jnp.cos(ang)
    sn_ref[...] = jnp.sin(ang)
    lat_ref[...] = lax.dot_general(u, wb_ref[...], (((1,), (1,)), ((), ())),
                                   preferred_element_type=F32).astype(BF16)


def _norm_rope(x, g, pos, w_in_t, rows=512):
    T, D = x.shape
    LW = Q_LORA + KV_LORA
    half_r = RET_QK_DIM // 2
    half_m = QK_ROPE // 2
    inv_r = ROPE_THETA ** (-jnp.arange(half_r, dtype=F32) / half_r)
    inv_m = ROPE_THETA ** (-jnp.arange(half_m, dtype=F32) / half_m)
    inv = jnp.concatenate([inv_r, inv_m, inv_m])[None]
    tab = pl.BlockSpec((rows, LANES), lambda i: (i, 0))
    return pl.pallas_call(
        _norm_rope_body,
        grid=(T // rows,),
        in_specs=[pl.BlockSpec((rows, D), lambda i: (i, 0)), pl.BlockSpec((1, D), lambda i: (0, 0)),
                  pl.BlockSpec((rows, 1), lambda i: (i, 0)), pl.BlockSpec((1, LANES), lambda i: (0, 0)),
                  pl.BlockSpec((LW, D), lambda i: (COL_CQ // LW, 0), pipeline_mode=pl.Buffered(1))],
        out_specs=[pl.BlockSpec((rows, D), lambda i: (i, 0)), tab, tab,
                   pl.BlockSpec((rows, LW), lambda i: (i, 0))],
        out_shape=[jax.ShapeDtypeStruct((T, D), BF16), jax.ShapeDtypeStruct((T, LANES), F32),
                   jax.ShapeDtypeStruct((T, LANES), F32), jax.ShapeDtypeStruct((T, LW), BF16)],
        scratch_shapes=[pltpu.VMEM((LW, D), BF16)],
        compiler_params=_params(("arbitrary",)),
        name="norm_rope_latent",
    )(x, g, pos, inv, w_in_t)


def _ret_rope_tables(cs, sn):
    lo = lax.broadcasted_iota(jnp.int32, cs.shape, 1) < LANES // 2
    return (jnp.where(lo, cs, pltpu.roll(cs, LANES // 2, 1)),
            jnp.where(lo, -sn, pltpu.roll(sn, LANES // 2, 1)))


def _mla_rope_tables(cs, sn):
    lane = lax.broadcasted_iota(jnp.int32, cs.shape, 1)
    half = QK_ROPE // 2
    c = pltpu.roll(cs, LANES // 2, 1)
    s = pltpu.roll(sn, LANES // 2, 1)
    return (jnp.where(lane < QK_ROPE, c, 0.0), jnp.where(lane < half, -s, 0.0),
            jnp.where((lane >= half) & (lane < QK_ROPE), s, 0.0))


def _cast_weight(w_ref, wb_ref):
    @pl.when(pl.program_id(1) == 0)
    def _cast():
        wb_ref[...] = w_ref[...].astype(BF16)


def _inproj_body(u_ref, wt_ref, o_ref, wb_ref):
    @pl.when(pl.program_id(1) == 0)
    def _cast():
        wb_ref[...] = wt_ref[...].astype(BF16)

    o_ref[...] = lax.dot_general(u_ref[...], wb_ref[...], (((1,), (1,)), ((), ())),
                                 preferred_element_type=F32).astype(BF16)


def _inproj(u, w_in_t, tm=1024, tn=1024):
    T, D = u.shape
    n_head = COL_CQ // tn
    n_gate = (w_in_t.shape[0] - COL_GRET) // tn

    def w_rows(j, i):
        row = jnp.where(j < n_head, j * tn, COL_GRET + (j - n_head) * tn)
        return (pl.multiple_of(row, QK_ROPE), 0)

    return pl.pallas_call(
        _inproj_body,
        grid=(n_head + n_gate, T // tm),
        in_specs=[
            pl.BlockSpec((tm, D), lambda j, i: (i, 0)),
            pl.BlockSpec((pl.Element(tn), pl.Element(D)), w_rows),
        ],
        out_specs=pl.BlockSpec((tm, tn), lambda j, i: (i, j)),
        out_shape=jax.ShapeDtypeStruct((T, (n_head + n_gate) * tn), BF16),
        scratch_shapes=[pltpu.VMEM((tn, D), BF16)],
        compiler_params=_params(("parallel", "arbitrary")),
        name="inproj",
    )(u, w_in_t)


def _mla_part(cq_ref, ckv_ref, u_ref, wpe_ref, gq_ref, gkv_ref, wq_ref, wk_ref, wv_ref, cs, sn,
              q_ref, kn_ref, kpe_ref, v_ref):
    qa = _rms(cq_ref[...].astype(F32), gq_ref[...]).astype(BF16)
    kva = _rms(ckv_ref[...].astype(F32), gkv_ref[...]).astype(BF16)
    c, s_lo, s_hi = _mla_rope_tables(cs, sn)
    kpe_raw = lax.dot_general(u_ref[...], wpe_ref[...].astype(BF16), (((1,), (1,)), ((), ())),
                              preferred_element_type=F32)
    kpe_ref[...] = _rope64(kpe_raw, c, s_lo, s_hi).astype(BF16)
    scale = MLA_QK_DIM ** -0.5 * LOG2_E
    cq = c * scale
    sq_lo = s_lo * scale
    sq_hi = s_hi * scale
    q = jnp.dot(qa, wq_ref[...], preferred_element_type=F32)
    for h in range(MLA_HEADS):
        b = h * MLA_HEAD_PAD
        q_ref[:, b:b + QK_NOPE] = (q[:, b:b + QK_NOPE] * scale).astype(BF16)
        q_ref[:, b + QK_NOPE:b + MLA_HEAD_PAD] = _rope64(
            q[:, b + QK_NOPE:b + MLA_HEAD_PAD], cq, sq_lo, sq_hi).astype(BF16)
    kn_ref[...] = jnp.dot(kva, wk_ref[...], preferred_element_type=F32).astype(BF16)
    v_ref[...] = jnp.dot(kva, wv_ref[...], preferred_element_type=F32).astype(BF16)


def _ret_tables_init(lg_ref, st_ref, dec_ref, qdec_ref, kdec_ref, bc):
    st_ref[...] = jnp.zeros_like(st_ref)
    n = lax.broadcasted_iota(jnp.int32, (bc, bc), 0)
    m = lax.broadcasted_iota(jnp.int32, (bc, bc), 1)
    dist = jnp.abs(n - m).astype(F32)
    visible = (m // CHUNK) <= (n // CHUNK)
    nl = lax.broadcasted_iota(jnp.int32, (bc, LANES), 0).astype(F32)
    for h in range(RET_HEADS):
        lg = lg_ref[h]
        dec_ref[h] = jnp.where(visible, jnp.exp(lg * dist), 0.0)
        qdec_ref[h] = jnp.exp(lg * (nl + 1.0))
        kdec_ref[h] = jnp.exp(lg * (bc - 1.0 - nl))


def _ret_block(rows, lg_ref, q_ref, k_ref, v_ref, rg_ref, cs, sn, g_ref, o_ref,
               st_ref, dec_ref, qdec_ref, kdec_ref, bc):
    c, s_ = _ret_rope_tables(cs, sn)
    k_scale = RET_QK_DIM ** -0.5
    ck = c * k_scale
    sk = s_ * k_scale
    for h in range(RET_HEADS):
        qs = slice(h * RET_QK_DIM, (h + 1) * RET_QK_DIM)
        vs = slice(h * RET_V_DIM, (h + 1) * RET_V_DIM)
        q32 = _rope128(q_ref[rows, qs].astype(F32), c, s_)
        k32 = _rope128(k_ref[rows, qs].astype(F32), ck, sk)
        q = q32.astype(BF16)
        k = k32.astype(BF16)
        v = v_ref[rows, vs]
        s = lax.dot_general(q, k, (((1,), (1,)), ((), ())), preferred_element_type=F32) * dec_ref[h]
        qd = (q32 * qdec_ref[h]).astype(BF16)
        st = st_ref[h]
        o = (jnp.dot(s.astype(BF16), v, preferred_element_type=F32)
             + jnp.dot(qd, st.astype(BF16), preferred_element_type=F32))
        kd = (k32 * kdec_ref[h]).astype(BF16)
        c_dec = jnp.exp(jnp.full((1, RET_V_DIM), lg_ref[h] * bc, F32))
        st_ref[h] = st * c_dec + lax.dot_general(
            kd, v, (((0,), (0,)), ((), ())), preferred_element_type=F32)
        mu = jnp.mean(o, axis=-1, keepdims=True)
        oc = o - mu
        var = jnp.mean(oc * oc, axis=-1, keepdims=True)
        y = oc * lax.rsqrt(var + EPS) * g_ref[:, vs]
        rg = rg_ref[rows, vs].astype(F32)
        o_ref[rows, vs] = (y * (rg * _sigmoid(rg))).astype(BF16)


def _branches_body(lg_ref, cq_ref, ckv_ref, u_ref, wpe_ref, gq_ref, gkv_ref, wq_ref, wk_ref, wv_ref,
                   cs_ref, sn_ref, rq_ref, rk_ref, rv_ref, rg_ref, gr_ref,
                   q_ref, kn_ref, kpe_ref, v_ref, ry_ref, st_ref, dec_ref, qdec_ref, kdec_ref,
                   *, tm, bc, tiles_per_seq):
    @pl.when(pl.program_id(0) % tiles_per_seq == 0)
    def _init():
        _ret_tables_init(lg_ref, st_ref, dec_ref, qdec_ref, kdec_ref, bc)

    _mla_part(cq_ref, ckv_ref, u_ref, wpe_ref, gq_ref, gkv_ref, wq_ref, wk_ref, wv_ref,
              cs_ref[...], sn_ref[...], q_ref, kn_ref, kpe_ref, v_ref)
    for blk in range(tm // bc):
        rows = slice(blk * bc, (blk + 1) * bc)
        _ret_block(rows, lg_ref, rq_ref, rk_ref, rv_ref, rg_ref, cs_ref[rows, :], sn_ref[rows, :],
                   gr_ref, ry_ref, st_ref, dec_ref, qdec_ref, kdec_ref, bc)


def _branches(proj, lat, u, w_in_t, gq, gkv, wq, wk, wv, cs, sn, lg, g_ret, seq, tm=512, bc=256):
    T, D = u.shape
    const = lambda shape: pl.BlockSpec(shape, lambda i, lg: (0, 0), pipeline_mode=pl.Buffered(1))
    rows = lambda w, col=0: pl.BlockSpec((tm, w), lambda i, lg: (i, col))
    QW = MLA_HEADS * MLA_HEAD_PAD
    KW = MLA_HEADS * QK_NOPE
    VW = MLA_HEADS * V_HEAD
    grid_spec = pltpu.PrefetchScalarGridSpec(
        num_scalar_prefetch=1,
        grid=(T // tm,),
        in_specs=[rows(Q_LORA, 0), rows(KV_LORA, 1), rows(D),
                  pl.BlockSpec((LANES, D), lambda i, lg: (COL_KPE // LANES, 0),
                               pipeline_mode=pl.Buffered(1)),
                  const((1, Q_LORA)), const((1, KV_LORA)),
                  const(wq.shape), const(wk.shape), const(wv.shape),
                  rows(LANES), rows(LANES),
                  rows(RET_QK_W, 0), rows(RET_QK_W, 1),
                  rows(RET_V_W, COL_RV // RET_V_W), rows(RET_V_W, COL_RG // RET_V_W),
                  const((1, RET_V_W))],
        out_specs=[rows(QW), rows(KW), rows(LANES), rows(VW), rows(RET_V_W)],
        scratch_shapes=[pltpu.VMEM((RET_HEADS, RET_QK_DIM, RET_V_DIM), F32),
                        pltpu.VMEM((RET_HEADS, bc, bc), F32),
                        pltpu.VMEM((RET_HEADS, bc, LANES), F32),
                        pltpu.VMEM((RET_HEADS, bc, LANES), F32)],
    )
    return pl.pallas_call(
        functools.partial(_branches_body, tm=tm, bc=bc, tiles_per_seq=seq // tm),
        grid_spec=grid_spec,
        out_shape=[jax.ShapeDtypeStruct((T, QW), BF16), jax.ShapeDtypeStruct((T, KW), BF16),
                   jax.ShapeDtypeStruct((T, LANES), BF16), jax.ShapeDtypeStruct((T, VW), BF16),
                   jax.ShapeDtypeStruct((T, RET_V_W), BF16)],
        compiler_params=_params(("arbitrary",), VMEM_LIMIT_BIG),
        name="mla_proj_retention",
    )(lg, lat, lat, u, w_in_t, gq, gkv, wq, wk, wv, cs, sn, proj, proj, proj, proj, g_ret)


def _attn_body(q_ref, kn_ref, kpe_ref, v_ref, o_ref, m_ref, acc_ref, *, tq, tk, nh):
    qi = pl.program_id(2)
    m_ref[...] = jnp.full_like(m_ref, -jnp.inf)
    acc_ref[...] = jnp.zeros_like(acc_ref)
    reps = tk // LANES

    def step(start, diag_off):
        r0 = 0 if diag_off is None else diag_off
        rows = tq - r0
        for hh in range(nh):
            q = q_ref[r0:, hh * MLA_HEAD_PAD:(hh + 1) * MLA_HEAD_PAD]
            k = jnp.concatenate([kn_ref[pl.ds(start, tk), hh * QK_NOPE:(hh + 1) * QK_NOPE],
                                 kpe_ref[pl.ds(start, tk), :]], axis=1)
            v = v_ref[pl.ds(start, tk), hh * V_HEAD:(hh + 1) * V_HEAD]
            s = lax.dot_general(q, k, (((1,), (1,)), ((), ())), preferred_element_type=F32)
            if diag_off is not None:
                rq = lax.broadcasted_iota(jnp.int32, (rows, tk), 0) // CHUNK
                ck = lax.broadcasted_iota(jnp.int32, (rows, tk), 1) // CHUNK
                s = jnp.where(ck <= rq, s, -jnp.inf)
            m_old = m_ref[hh, r0:, :]
            m_new = jnp.maximum(m_old, jnp.max(s, axis=-1, keepdims=True))
            alpha = jnp.exp2(m_old - m_new)
            p = jnp.exp2(s - jnp.concatenate([m_new] * reps, axis=1))
            pv = jnp.dot(p.astype(BF16), jnp.concatenate([v, jnp.ones_like(v)], axis=1),
                         preferred_element_type=F32)
            acc_ref[hh, r0:, :] = jnp.concatenate([alpha, alpha], axis=1) * acc_ref[hh, r0:, :] + pv
            m_ref[hh, r0:, :] = m_new

    def full_steps(kb, carry):
        for j in range(tq // tk):
            step(pl.multiple_of(kb * tq + j * tk, tk), None)
        return carry

    lax.fori_loop(0, qi, full_steps, 0)
    for j in range(tq // tk):
        step(pl.multiple_of(qi * tq + j * tk, tk), j * tk)
    for hh in range(nh):
        o_ref[:, hh * V_HEAD:(hh + 1) * V_HEAD] = (
            acc_ref[hh, :, :V_HEAD] / acc_ref[hh, :, V_HEAD:]).astype(BF16)


def _attention(q, kn, kpe, v, batch, seq, tq=2048, tk=512, nh=2):
    nq = seq // tq
    return pl.pallas_call(
        functools.partial(_attn_body, tq=tq, tk=tk, nh=nh),
        grid=(batch, MLA_HEADS // nh, nq),
        in_specs=[
            pl.BlockSpec((tq, nh * MLA_HEAD_PAD), lambda b, h, i: (b * nq + i, h)),
            pl.BlockSpec((seq, nh * QK_NOPE), lambda b, h, i: (b, h)),
            pl.BlockSpec((seq, LANES), lambda b, h, i: (b, 0)),
            pl.BlockSpec((seq, nh * V_HEAD), lambda b, h, i: (b, h)),
        ],
        out_specs=pl.BlockSpec((tq, nh * V_HEAD), lambda b, h, i: (b * nq + i, h)),
        out_shape=jax.ShapeDtypeStruct((batch * seq, MLA_HEADS * V_HEAD), BF16),
        scratch_shapes=[pltpu.VMEM((nh, tq, LANES), F32), pltpu.VMEM((nh, tq, 2 * V_HEAD), F32)],
        compiler_params=_params(("parallel", "parallel", "arbitrary")),
        name="mla_attention",
    )(q, kn, kpe, v)


def _merge_body(a_ref, b_ref, wa_ref, wb_ref, ga_ref, gb_ref, o_ref, wab_ref, wbb_ref):
    _cast_weight(wa_ref, wab_ref)
    _cast_weight(wb_ref, wbb_ref)
    ya = jnp.dot(a_ref[...], wab_ref[...], preferred_element_type=F32)
    yb = jnp.dot(b_ref[...], wbb_ref[...], preferred_element_type=F32)
    ga = _sigmoid(ga_ref[...].astype(F32))
    gb = _sigmoid(gb_ref[...].astype(F32))
    o_ref[...] = (ga * ya + gb * yb).astype(BF16)


def _merge(ry, my, w_ret_o, w_mla_o, proj, gate_col, tm=512, tn=1024):
    T, K = ry.shape
    N = w_ret_o.shape[1]
    nj = N // tn
    g0 = gate_col // tn
    wspec = pl.BlockSpec((K, tn), lambda j, i: (0, j), pipeline_mode=pl.Buffered(1))
    return pl.pallas_call(
        _merge_body,
        grid=(nj, T // tm),
        in_specs=[
            pl.BlockSpec((tm, K), lambda j, i: (i, 0)),
            pl.BlockSpec((tm, K), lambda j, i: (i, 0)),
            wspec, wspec,
            pl.BlockSpec((tm, tn), lambda j, i: (i, g0 + j)),
            pl.BlockSpec((tm, tn), lambda j, i: (i, g0 + nj + j)),
        ],
        out_specs=pl.BlockSpec((tm, tn), lambda j, i: (i, j)),
        out_shape=jax.ShapeDtypeStruct((T, N), BF16),
        scratch_shapes=[pltpu.VMEM((K, tn), BF16), pltpu.VMEM((K, tn), BF16)],
        compiler_params=_params(("parallel", "arbitrary")),
        name="gated_merge",
    )(ry, my, w_ret_o, w_mla_o, proj, proj)


def _outproj_body(a_ref, w_ref, x_ref, o_ref, wb_ref):
    _cast_weight(w_ref, wb_ref)
    o_ref[...] = x_ref[...] + jnp.dot(a_ref[...], wb_ref[...], preferred_element_type=F32)


def _outproj(a, w, x, tm=512, tn=1024):
    T, K = a.shape
    N = w.shape[1]
    return pl.pallas_call(
        _outproj_body,
        grid=(N // tn, T // tm),
        in_specs=[
            pl.BlockSpec((tm, K), lambda j, i: (i, 0)),
            pl.BlockSpec((K, tn), lambda j, i: (0, j), pipeline_mode=pl.Buffered(1)),
            pl.BlockSpec((tm, tn), lambda j, i: (i, j)),
        ],
        out_specs=pl.BlockSpec((tm, tn), lambda j, i: (i, j)),
        out_shape=jax.ShapeDtypeStruct((T, N), F32),
        scratch_shapes=[pltpu.VMEM((K, tn), BF16)],
        compiler_params=_params(("parallel", "arbitrary")),
        name="out_proj",
    )(a, w, x)


def _mlp_body(h_ref, g_ref, wu_ref, wd_ref, gf_ref, o_ref, hn_ref):
    f = pl.program_id(1)

    @pl.when(f == 0)
    def _init():
        hn_ref[...] = _rms(h_ref[...], g_ref[...]).astype(BF16)
        o_ref[...] = jnp.zeros_like(o_ref)

    z = jnp.dot(hn_ref[...], wu_ref[...].astype(BF16), preferred_element_type=F32)
    a = jnp.square(jnp.maximum(z, 0.0)).astype(BF16)
    o_ref[...] += jnp.dot(a, wd_ref[...].astype(BF16), preferred_element_type=F32)

    @pl.when(f == pl.num_programs(1) - 1)
    def _finish():
        o_ref[...] = _rms(h_ref[...] + o_ref[...], gf_ref[...])


def _mlp(h, g, w_up, w_down, gf, tm=1024, tf=512):
    T, D = h.shape
    FF = w_up.shape[1]
    return pl.pallas_call(
        _mlp_body,
        grid=(T // tm, FF // tf),
        in_specs=[
            pl.BlockSpec((tm, D), lambda i, f: (i, 0)),
            pl.BlockSpec((1, D), lambda i, f: (0, 0)),
            pl.BlockSpec((D, tf), lambda i, f: (0, f)),
            pl.BlockSpec((tf, D), lambda i, f: (f, 0)),
            pl.BlockSpec((1, D), lambda i, f: (0, 0)),
        ],
        out_specs=pl.BlockSpec((tm, D), lambda i, f: (i, 0)),
        out_shape=jax.ShapeDtypeStruct((T, D), F32),
        scratch_shapes=[pltpu.VMEM((tm, D), BF16)],
        compiler_params=_params(("parallel", "arbitrary"), VMEM_LIMIT_BIG),
        name="mlp_final_norm",
    )(h, g, w_up, w_down, gf)


def _prep_q_weights(w_q_b):
    w = w_q_b.reshape(Q_LORA, MLA_HEADS, MLA_QK_DIM)
    w = jnp.pad(w, ((0, 0), (0, 0), (0, MLA_HEAD_PAD - MLA_QK_DIM)))
    return w.reshape(Q_LORA, MLA_HEADS * MLA_HEAD_PAD).astype(BF16)


def _prep_kv_weights(w_kv_b):
    w = w_kv_b.reshape(KV_LORA, MLA_HEADS, QK_NOPE + V_HEAD)
    wk = w[..., :QK_NOPE].reshape(KV_LORA, MLA_HEADS * QK_NOPE).astype(BF16)
    wv = w[..., QK_NOPE:].reshape(KV_LORA, MLA_HEADS * V_HEAD).astype(BF16)
    return wk, wv


def kernel(x, positions, norm_mix_g, w_in, ret_norm_g, w_ret_o, q_a_norm_g, w_q_b, kv_a_norm_g,
           w_kv_b, w_mla_o, w_out, norm_mlp_g, w_up, w_down, norm_f_g):
    B, S, D = x.shape
    T = B * S
    assert w_in.shape[0] == 1, "single-layer block: the final norm is fused into the MLP kernel"
    log_gamma = jnp.log(1.0 - 2.0 ** (-5.0 - jnp.arange(RET_HEADS, dtype=F32)))
    h = x.reshape(T, D)
    w_in_t = jnp.swapaxes(w_in[0], 0, 1)
    wq = _prep_q_weights(w_q_b[0])
    wk, wv = _prep_kv_weights(w_kv_b[0])
    u, cs, sn, lat = _norm_rope(h, norm_mix_g[0][None], positions.reshape(T, 1), w_in_t)
    proj = _inproj(u, w_in_t)
    q, kn, kpe, v, ry = _branches(proj, lat, u, w_in_t, q_a_norm_g[0][None], kv_a_norm_g[0][None],
                            wq, wk, wv, cs, sn, log_gamma, ret_norm_g[0][None], S)
    my = _attention(q, kn, kpe, v, B, S)
    merged = _merge(ry, my, w_ret_o[0], w_mla_o[0], proj, COL_CQ)
    h1 = _outproj(merged, w_out[0], h)
    out = _mlp(h1, norm_mlp_g[0][None], w_up[0], w_down[0], norm_f_g[None])
    return out.reshape(B, S, D)
```

```python
import functools

import jax
import jax.numpy as jnp
from jax import lax
from jax.experimental import pallas as pl
from jax.experimental.pallas import tpu as pltpu

F32 = jnp.float32
BF16 = jnp.bfloat16

EPS = 1e-6
ROPE_THETA = 10000.0
CHUNK = 64

RET_HEADS = 8
RET_QK_DIM = 128
RET_V_DIM = 256
RET_QK_W = RET_HEADS * RET_QK_DIM
RET_V_W = RET_HEADS * RET_V_DIM

MLA_HEADS = 16
Q_LORA = 512
KV_LORA = 512
QK_NOPE = 128
QK_ROPE = 64
V_HEAD = 128
MLA_QK_DIM = QK_NOPE + QK_ROPE
MLA_HEAD_PAD = 256

COL_RV = 2 * RET_QK_W
COL_RG = COL_RV + RET_V_W
COL_CQ = COL_RG + RET_V_W
COL_CKV = COL_CQ + Q_LORA
COL_KPE = COL_CKV + KV_LORA
COL_GRET = COL_KPE + QK_ROPE

LOG2_E = 1.4426950408889634
LANES = 128
VMEM_LIMIT = 56 * 1024 * 1024
VMEM_LIMIT_BIG = 62 * 1024 * 1024


def _params(sem, vmem_limit=VMEM_LIMIT):
    return pltpu.CompilerParams(dimension_semantics=sem, vmem_limit_bytes=vmem_limit)


def _rms(x, g):
    return x * lax.rsqrt(jnp.mean(x * x, axis=-1, keepdims=True) + EPS) * g


def _rope128(t, c, s):
    return t * c + pltpu.roll(t, LANES // 2, 1) * s


def _rope64(t, c, s_lo, s_hi):
    return t * c + pltpu.roll(t, LANES - QK_ROPE // 2, 1) * s_lo + pltpu.roll(t, QK_ROPE // 2, 1) * s_hi


def _sigmoid(x):
    return 1.0 / (1.0 + jnp.exp(-x))


def _norm_rope_body(x_ref, g_ref, pos_ref, inv_ref, wt_ref, u_ref, cs_ref, sn_ref, lat_ref, wb_ref, up_ref):
    i = pl.program_id(0)

    @pl.when(i == 0)
    def _first():
        wb_ref[...] = wt_ref[...].astype(BF16)
        up_ref[...] = jnp.zeros_like(up_ref)

    lat_ref[...] = lax.dot_general(up_ref[(i + 1) % 2], wb_ref[...], (((1,), (1,)), ((), ())),
                                   preferred_element_type=F32).astype(BF16)
    u = _rms(x_ref[...], g_ref[...]).astype(BF16)
    u_ref[...] = u
    up_ref[i % 2] = u
    ang = pos_ref[...].astype(F32) * inv_ref[...]
    cs_ref[...] = jnp.cos(ang)
    sn_ref[...] = jnp.sin(ang)


def _norm_rope(x, g, pos, w_in_t, rows=512):
    T, D = x.shape
    LW = Q_LORA + KV_LORA
    n = T // rows
    half_r = RET_QK_DIM // 2
    half_m = QK_ROPE // 2
    inv_r = ROPE_THETA ** (-jnp.arange(half_r, dtype=F32) / half_r)
    inv_m = ROPE_THETA ** (-jnp.arange(half_m, dtype=F32) / half_m)
    inv = jnp.concatenate([inv_r, inv_m, inv_m])[None]
    cur = lambda i: (jnp.minimum(i, n - 1), 0)
    prev = lambda i: (jnp.maximum(i - 1, 0), 0)
    tab = pl.BlockSpec((rows, LANES), cur)
    return pl.pallas_call(
        _norm_rope_body,
        grid=(n + 1,),
        in_specs=[pl.BlockSpec((rows, D), cur), pl.BlockSpec((1, D), lambda i: (0, 0)),
                  pl.BlockSpec((rows, 1), cur), pl.BlockSpec((1, LANES), lambda i: (0, 0)),
                  pl.BlockSpec((LW, D), lambda i: (COL_CQ // LW, 0), pipeline_mode=pl.Buffered(1))],
        out_specs=[pl.BlockSpec((rows, D), cur), tab, tab, pl.BlockSpec((rows, LW), prev)],
        out_shape=[jax.ShapeDtypeStruct((T, D), BF16), jax.ShapeDtypeStruct((T, LANES), F32),
                   jax.ShapeDtypeStruct((T, LANES), F32), jax.ShapeDtypeStruct((T, LW), BF16)],
        scratch_shapes=[pltpu.VMEM((LW, D), BF16), pltpu.VMEM((2, rows, D), BF16)],
        compiler_params=_params(("arbitrary",)),
        name="norm_rope_latent",
    )(x, g, pos, inv, w_in_t)


def _ret_rope_tables(cs, sn):
    lo = lax.broadcasted_iota(jnp.int32, cs.shape, 1) < LANES // 2
    return (jnp.where(lo, cs, pltpu.roll(cs, LANES // 2, 1)),
            jnp.where(lo, -sn, pltpu.roll(sn, LANES // 2, 1)))


def _mla_rope_tables(cs, sn):
    lane = lax.broadcasted_iota(jnp.int32, cs.shape, 1)
    half = QK_ROPE // 2
    c = pltpu.roll(cs, LANES // 2, 1)
    s = pltpu.roll(sn, LANES // 2, 1)
    return (jnp.where(lane < QK_ROPE, c, 0.0), jnp.where(lane < half, -s, 0.0),
            jnp.where((lane >= half) & (lane < QK_ROPE), s, 0.0))


def _cast_weight(w_ref, wb_ref):
    @pl.when(pl.program_id(1) == 0)
    def _cast():
        wb_ref[...] = w_ref[...].astype(BF16)


def _inproj_body(u_ref, wt_ref, o_ref, wb_ref):
    @pl.when(pl.program_id(1) == 0)
    def _cast():
        wb_ref[...] = wt_ref[...].astype(BF16)

    o_ref[...] = lax.dot_general(u_ref[...], wb_ref[...], (((1,), (1,)), ((), ())),
                                 preferred_element_type=F32).astype(BF16)


def _inproj(u, w_in_t, tm=2048, tn=1024):
    T, D = u.shape
    n_head = COL_CQ // tn
    n_gate = (w_in_t.shape[0] - COL_GRET) // tn

    def w_rows(j, i):
        row = jnp.where(j < n_head, j * tn, COL_GRET + (j - n_head) * tn)
        return (pl.multiple_of(row, QK_ROPE), 0)

    return pl.pallas_call(
        _inproj_body,
        grid=(n_head + n_gate, T // tm),
        in_specs=[
            pl.BlockSpec((tm, D), lambda j, i: (i, 0)),
            pl.BlockSpec((pl.Element(tn), pl.Element(D)), w_rows),
        ],
        out_specs=pl.BlockSpec((tm, tn), lambda j, i: (i, j)),
        out_shape=jax.ShapeDtypeStruct((T, (n_head + n_gate) * tn), BF16),
        scratch_shapes=[pltpu.VMEM((tn, D), BF16)],
        compiler_params=_params(("parallel", "arbitrary")),
        name="inproj",
    )(u, w_in_t)


def _mla_part(cq_ref, ckv_ref, u_ref, wpe_ref, gq_ref, gkv_ref, wq_ref, wk_ref, wv_ref, cs, sn,
              q_ref, kn_ref, kpe_ref, v_ref):
    qa = _rms(cq_ref[...].astype(F32), gq_ref[...]).astype(BF16)
    kva = _rms(ckv_ref[...].astype(F32), gkv_ref[...]).astype(BF16)
    c, s_lo, s_hi = _mla_rope_tables(cs, sn)
    kpe_raw = lax.dot_general(u_ref[...], wpe_ref[...].astype(BF16), (((1,), (1,)), ((), ())),
                              preferred_element_type=F32)
    kpe_ref[...] = _rope64(kpe_raw, c, s_lo, s_hi).astype(BF16)
    scale = MLA_QK_DIM ** -0.5 * LOG2_E
    cq = c * scale
    sq_lo = s_lo * scale
    sq_hi = s_hi * scale
    q = jnp.dot(qa, wq_ref[...], preferred_element_type=F32)
    for h in range(MLA_HEADS):
        b = h * MLA_HEAD_PAD
        q_ref[:, b:b + QK_NOPE] = (q[:, b:b + QK_NOPE] * scale).astype(BF16)
        q_ref[:, b + QK_NOPE:b + MLA_HEAD_PAD] = _rope64(
            q[:, b + QK_NOPE:b + MLA_HEAD_PAD], cq, sq_lo, sq_hi).astype(BF16)
    kn_ref[...] = jnp.dot(kva, wk_ref[...], preferred_element_type=F32).astype(BF16)
    v_ref[...] = jnp.dot(kva, wv_ref[...], preferred_element_type=F32).astype(BF16)


def _ret_tables_init(lg_ref, st_ref, dec_ref, qdec_ref, kdec_ref, bc):
    st_ref[...] = jnp.zeros_like(st_ref)
    n = lax.broadcasted_iota(jnp.int32, (bc, bc), 0)
    m = lax.broadcasted_iota(jnp.int32, (bc, bc), 1)
    dist = jnp.abs(n - m).astype(F32)
    visible = (m // CHUNK) <= (n // CHUNK)
    nl = lax.broadcasted_iota(jnp.int32, (bc, LANES), 0).astype(F32)
    for h in range(RET_HEADS):
        lg = lg_ref[h]
        dec_ref[h] = jnp.where(visible, jnp.exp(lg * dist), 0.0)
        qdec_ref[h] = jnp.exp(lg * (nl + 1.0))
        kdec_ref[h] = jnp.exp(lg * (bc - 1.0 - nl))


def _ret_block(rows, lg_ref, q_ref, k_ref, v_ref, rg_ref, cs, sn, g_ref, o_ref,
               st_ref, dec_ref, qdec_ref, kdec_ref, bc):
    c, s_ = _ret_rope_tables(cs, sn)
    k_scale = RET_QK_DIM ** -0.5
    ck = c * k_scale
    sk = s_ * k_scale
    for h in range(RET_HEADS):
        qs = slice(h * RET_QK_DIM, (h + 1) * RET_QK_DIM)
        vs = slice(h * RET_V_DIM, (h + 1) * RET_V_DIM)
        q32 = _rope128(q_ref[rows, qs].astype(F32), c, s_)
        k32 = _rope128(k_ref[rows, qs].astype(F32), ck, sk)
        q = q32.astype(BF16)
        k = k32.astype(BF16)
        v = v_ref[rows, vs]
        s = lax.dot_general(q, k, (((1,), (1,)), ((), ())), preferred_element_type=F32) * dec_ref[h]
        qd = (q32 * qdec_ref[h]).astype(BF16)
        st = st_ref[h]
        o = (jnp.dot(s.astype(BF16), v, preferred_element_type=F32)
             + jnp.dot(qd, st.astype(BF16), preferred_element_type=F32))
        kd = (k32 * kdec_ref[h]).astype(BF16)
        c_dec = jnp.exp(jnp.full((1, RET_V_DIM), lg_ref[h] * bc, F32))
        st_ref[h] = st * c_dec + lax.dot_general(
            kd, v, (((0,), (0,)), ((), ())), preferred_element_type=F32)
        mu = jnp.mean(o, axis=-1, keepdims=True)
        oc = o - mu
        var = jnp.mean(oc * oc, axis=-1, keepdims=True)
        y = oc * lax.rsqrt(var + EPS) * g_ref[:, vs]
        rg = rg_ref[rows, vs].astype(F32)
        o_ref[rows, vs] = (y * (rg * _sigmoid(rg))).astype(BF16)


def _branches_body(lg_ref, cq_ref, ckv_ref, u_ref, wpe_ref, gq_ref, gkv_ref, wq_ref, wk_ref, wv_ref,
                   cs_ref, sn_ref, rq_ref, rk_ref, rv_ref, rg_ref, gr_ref,
                   q_ref, kn_ref, kpe_ref, v_ref, ry_ref, st_ref, dec_ref, qdec_ref, kdec_ref,
                   *, tm, bc, tiles_per_seq):
    @pl.when(pl.program_id(0) % tiles_per_seq == 0)
    def _init():
        _ret_tables_init(lg_ref, st_ref, dec_ref, qdec_ref, kdec_ref, bc)

    _mla_part(cq_ref, ckv_ref, u_ref, wpe_ref, gq_ref, gkv_ref, wq_ref, wk_ref, wv_ref,
              cs_ref[...], sn_ref[...], q_ref, kn_ref, kpe_ref, v_ref)
    for blk in range(tm // bc):
        rows = slice(blk * bc, (blk + 1) * bc)
        _ret_block(rows, lg_ref, rq_ref, rk_ref, rv_ref, rg_ref, cs_ref[rows, :], sn_ref[rows, :],
                   gr_ref, ry_ref, st_ref, dec_ref, qdec_ref, kdec_ref, bc)


def _branches(proj, lat, u, w_in_t, gq, gkv, wq, wk, wv, cs, sn, lg, g_ret, seq, tm=512, bc=256):
    T, D = u.shape
    const = lambda shape: pl.BlockSpec(shape, lambda i, lg: (0, 0), pipeline_mode=pl.Buffered(1))
    rows = lambda w, col=0: pl.BlockSpec((tm, w), lambda i, lg: (i, col))
    QW = MLA_HEADS * MLA_HEAD_PAD
    KW = MLA_HEADS * QK_NOPE
    VW = MLA_HEADS * V_HEAD
    grid_spec = pltpu.PrefetchScalarGridSpec(
        num_scalar_prefetch=1,
        grid=(T // tm,),
        in_specs=[rows(Q_LORA, 0), rows(KV_LORA, 1), rows(D),
                  pl.BlockSpec((LANES, D), lambda i, lg: (COL_KPE // LANES, 0),
                               pipeline_mode=pl.Buffered(1)),
                  const((1, Q_LORA)), const((1, KV_LORA)),
                  const(wq.shape), const(wk.shape), const(wv.shape),
                  rows(LANES), rows(LANES),
                  rows(RET_QK_W, 0), rows(RET_QK_W, 1),
                  rows(RET_V_W, COL_RV // RET_V_W), rows(RET_V_W, COL_RG // RET_V_W),
                  const((1, RET_V_W))],
        out_specs=[rows(QW), rows(KW), rows(LANES), rows(VW), rows(RET_V_W)],
        scratch_shapes=[pltpu.VMEM((RET_HEADS, RET_QK_DIM, RET_V_DIM), F32),
                        pltpu.VMEM((RET_HEADS, bc, bc), F32),
                        pltpu.VMEM((RET_HEADS, bc, LANES), F32),
                        pltpu.VMEM((RET_HEADS, bc, LANES), F32)],
    )
    return pl.pallas_call(
        functools.partial(_branches_body, tm=tm, bc=bc, tiles_per_seq=seq // tm),
        grid_spec=grid_spec,
        out_shape=[jax.ShapeDtypeStruct((T, QW), BF16), jax.ShapeDtypeStruct((T, KW), BF16),
                   jax.ShapeDtypeStruct((T, LANES), BF16), jax.ShapeDtypeStruct((T, VW), BF16),
                   jax.ShapeDtypeStruct((T, RET_V_W), BF16)],
        compiler_params=_params(("arbitrary",), VMEM_LIMIT_BIG),
        name="mla_proj_retention",
    )(lg, lat, lat, u, w_in_t, gq, gkv, wq, wk, wv, cs, sn, proj, proj, proj, proj, g_ret)


def _attn_body(q_ref, kn_ref, kpe_ref, v_ref, o_ref, m_ref, acc_ref, *, tq, tk, nh):
    qi = pl.program_id(2)
    m_ref[...] = jnp.full_like(m_ref, -jnp.inf)
    acc_ref[...] = jnp.zeros_like(acc_ref)
    reps = tk // LANES

    def step(start, diag_off):
        r0 = 0 if diag_off is None else diag_off
        rows = tq - r0
        for hh in range(nh):
            q = q_ref[r0:, hh * MLA_HEAD_PAD:(hh + 1) * MLA_HEAD_PAD]
            k = jnp.concatenate([kn_ref[pl.ds(start, tk), hh * QK_NOPE:(hh + 1) * QK_NOPE],
                                 kpe_ref[pl.ds(start, tk), :]], axis=1)
            v = v_ref[pl.ds(start, tk), hh * V_HEAD:(hh + 1) * V_HEAD]
            s = lax.dot_general(q, k, (((1,), (1,)), ((), ())), preferred_element_type=F32)
            if diag_off is not None:
                rq = lax.broadcasted_iota(jnp.int32, (rows, tk), 0) // CHUNK
                ck = lax.broadcasted_iota(jnp.int32, (rows, tk), 1) // CHUNK
                s = jnp.where(ck <= rq, s, -jnp.inf)
            m_old = m_ref[hh, r0:, :]
            m_new = jnp.maximum(m_old, jnp.max(s, axis=-1, keepdims=True))
            alpha = jnp.exp2(m_old - m_new)
            p = jnp.exp2(s - jnp.concatenate([m_new] * reps, axis=1))
            pv = jnp.dot(p.astype(BF16), jnp.concatenate([v, jnp.ones_like(v)], axis=1),
                         preferred_element_type=F32)
            acc_ref[hh, r0:, :] = jnp.concatenate([alpha, alpha], axis=1) * acc_ref[hh, r0:, :] + pv
            m_ref[hh, r0:, :] = m_new

    def full_steps(kb, carry):
        for j in range(tq // tk):
            step(pl.multiple_of(kb * tq + j * tk, tk), None)
        return carry

    lax.fori_loop(0, qi, full_steps, 0)
    for j in range(tq // tk):
        step(pl.multiple_of(qi * tq + j * tk, tk), j * tk)
    for hh in range(nh):
        o_ref[:, hh * V_HEAD:(hh + 1) * V_HEAD] = (
            acc_ref[hh, :, :V_HEAD] / acc_ref[hh, :, V_HEAD:]).astype(BF16)


def _attention(q, kn, kpe, v, batch, seq, tq=2048, tk=512, nh=2):
    nq = seq // tq
    return pl.pallas_call(
        functools.partial(_attn_body, tq=tq, tk=tk, nh=nh),
        grid=(batch, MLA_HEADS // nh, nq),
        in_specs=[
            pl.BlockSpec((tq, nh * MLA_HEAD_PAD), lambda b, h, i: (b * nq + i, h)),
            pl.BlockSpec((seq, nh * QK_NOPE), lambda b, h, i: (b, h)),
            pl.BlockSpec((seq, LANES), lambda b, h, i: (b, 0)),
            pl.BlockSpec((seq, nh * V_HEAD), lambda b, h, i: (b, h)),
        ],
        out_specs=pl.BlockSpec((tq, nh * V_HEAD), lambda b, h, i: (b * nq + i, h)),
        out_shape=jax.ShapeDtypeStruct((batch * seq, MLA_HEADS * V_HEAD), BF16),
        scratch_shapes=[pltpu.VMEM((nh, tq, LANES), F32), pltpu.VMEM((nh, tq, 2 * V_HEAD), F32)],
        compiler_params=_params(("parallel", "parallel", "arbitrary")),
        name="mla_attention",
    )(q, kn, kpe, v)


def _merge_body(a_ref, b_ref, wa_ref, wb_ref, ga_ref, gb_ref, o_ref, wab_ref, wbb_ref):
    _cast_weight(wa_ref, wab_ref)
    _cast_weight(wb_ref, wbb_ref)
    ya = jnp.dot(a_ref[...], wab_ref[...], preferred_element_type=F32)
    yb = jnp.dot(b_ref[...], wbb_ref[...], preferred_element_type=F32)
    ga = _sigmoid(ga_ref[...].astype(F32))
    gb = _sigmoid(gb_ref[...].astype(F32))
    o_ref[...] = (ga * ya + gb * yb).astype(BF16)


def _merge(ry, my, w_ret_o, w_mla_o, proj, gate_col, tm=512, tn=1024):
    T, K = ry.shape
    N = w_ret_o.shape[1]
    nj = N // tn
    g0 = gate_col // tn
    wspec = pl.BlockSpec((K, tn), lambda j, i: (0, j), pipeline_mode=pl.Buffered(1))
    return pl.pallas_call(
        _merge_body,
        grid=(nj, T // tm),
        in_specs=[
            pl.BlockSpec((tm, K), lambda j, i: (i, 0)),
            pl.BlockSpec((tm, K), lambda j, i: (i, 0)),
            wspec, wspec,
            pl.BlockSpec((tm, tn), lambda j, i: (i, g0 + j)),
            pl.BlockSpec((tm, tn), lambda j, i: (i, g0 + nj + j)),
        ],
        out_specs=pl.BlockSpec((tm, tn), lambda j, i: (i, j)),
        out_shape=jax.ShapeDtypeStruct((T, N), BF16),
        scratch_shapes=[pltpu.VMEM((K, tn), BF16), pltpu.VMEM((K, tn), BF16)],
        compiler_params=_params(("parallel", "arbitrary")),
        name="gated_merge",
    )(ry, my, w_ret_o, w_mla_o, proj, proj)


def _outproj_body(a_ref, w_ref, x_ref, o_ref, wb_ref):
    _cast_weight(w_ref, wb_ref)
    o_ref[...] = x_ref[...] + jnp.dot(a_ref[...], wb_ref[...], preferred_element_type=F32)


def _outproj(a, w, x, tm=512, tn=1024):
    T, K = a.shape
    N = w.shape[1]
    return pl.pallas_call(
        _outproj_body,
        grid=(N // tn, T // tm),
        in_specs=[
            pl.BlockSpec((tm, K), lambda j, i: (i, 0)),
            pl.BlockSpec((K, tn), lambda j, i: (0, j), pipeline_mode=pl.Buffered(1)),
            pl.BlockSpec((tm, tn), lambda j, i: (i, j)),
        ],
        out_specs=pl.BlockSpec((tm, tn), lambda j, i: (i, j)),
        out_shape=jax.ShapeDtypeStruct((T, N), F32),
        scratch_shapes=[pltpu.VMEM((K, tn), BF16)],
        compiler_params=_params(("parallel", "arbitrary")),
        name="out_proj",
    )(a, w, x)


def _mlp_body(h_ref, g_ref, wu_ref, wd_ref, gf_ref, o_ref, hn_ref):
    f = pl.program_id(1)

    @pl.when(f == 0)
    def _init():
        hn_ref[...] = _rms(h_ref[...], g_ref[...]).astype(BF16)
        o_ref[...] = jnp.zeros_like(o_ref)

    z = jnp.dot(hn_ref[...], wu_ref[...].astype(BF16), preferred_element_type=F32)
    a = jnp.square(jnp.maximum(z, 0.0)).astype(BF16)
    o_ref[...] += jnp.dot(a, wd_ref[...].astype(BF16), preferred_element_type=F32)

    @pl.when(f == pl.num_programs(1) - 1)
    def _finish():
        o_ref[...] = _rms(h_ref[...] + o_ref[...], gf_ref[...])


def _mlp(h, g, w_up, w_down, gf, tm=1024, tf=512):
    T, D = h.shape
    FF = w_up.shape[1]
    return pl.pallas_call(
        _mlp_body,
        grid=(T // tm, FF // tf),
        in_specs=[
            pl.BlockSpec((tm, D), lambda i, f: (i, 0)),
            pl.BlockSpec((1, D), lambda i, f: (0, 0)),
            pl.BlockSpec((D, tf), lambda i, f: (0, f)),
            pl.BlockSpec((tf, D), lambda i, f: (f, 0)),
            pl.BlockSpec((1, D), lambda i, f: (0, 0)),
        ],
        out_specs=pl.BlockSpec((tm, D), lambda i, f: (i, 0)),
        out_shape=jax.ShapeDtypeStruct((T, D), F32),
        scratch_shapes=[pltpu.VMEM((tm, D), BF16)],
        compiler_params=_params(("parallel", "arbitrary"), VMEM_LIMIT_BIG),
        name="mlp_final_norm",
    )(h, g, w_up, w_down, gf)


def _prep_q_weights(w_q_b):
    w = w_q_b.reshape(Q_LORA, MLA_HEADS, MLA_QK_DIM)
    w = jnp.pad(w, ((0, 0), (0, 0), (0, MLA_HEAD_PAD - MLA_QK_DIM)))
    return w.reshape(Q_LORA, MLA_HEADS * MLA_HEAD_PAD).astype(BF16)


def _prep_kv_weights(w_kv_b):
    w = w_kv_b.reshape(KV_LORA, MLA_HEADS, QK_NOPE + V_HEAD)
    wk = w[..., :QK_NOPE].reshape(KV_LORA, MLA_HEADS * QK_NOPE).astype(BF16)
    wv = w[..., QK_NOPE:].reshape(KV_LORA, MLA_HEADS * V_HEAD).astype(BF16)
    return wk, wv


def kernel(x, positions, norm_mix_g, w_in, ret_norm_g, w_ret_o, q_a_norm_g, w_q_b, kv_a_norm_g,
           w_kv_b, w_mla_o, w_out, norm_mlp_g, w_up, w_down, norm_f_g):
    B, S, D = x.shape
    T = B * S
    assert w_in.shape[0] == 1, "single-layer block: the final norm is fused into the MLP kernel"
    log_gamma = jnp.log(1.0 - 2.0 ** (-5.0 - jnp.arange(RET_HEADS, dtype=F32)))
    h = x.reshape(T, D)
    w_in_t = jnp.swapaxes(w_in[0], 0, 1)
    wq = _prep_q_weights(w_q_b[0])
    wk, wv = _prep_kv_weights(w_kv_b[0])
    u, cs, sn, lat = _norm_rope(h, norm_mix_g[0][None], positions.reshape(T, 1), w_in_t)
    proj = _inproj(u, w_in_t)
    q, kn, kpe, v, ry = _branches(proj, lat, u, w_in_t, q_a_norm_g[0][None], kv_a_norm_g[0][None],
                            wq, wk, wv, cs, sn, log_gamma, ret_norm_g[0][None], S)
    my = _attention(q, kn, kpe, v, B, S)
    merged = _merge(ry, my, w_ret_o[0], w_mla_o[0], proj, COL_CQ)
    h1 = _outproj(merged, w_out[0], h)
    out = _mlp(h1, norm_mlp_g[0][None], w_up[0], w_down[0], norm_f_g[None])
    return out.reshape(B, S, D)
```

```python
import functools

import jax
import jax.numpy as jnp
from jax import lax
from jax.experimental import pallas as pl
from jax.experimental.pallas import tpu as pltpu

F32 = jnp.float32
BF16 = jnp.bfloat16

EPS = 1e-6
ROPE_THETA = 10000.0
CHUNK = 64

RET_HEADS = 8
RET_QK_DIM = 128
RET_V_DIM = 256
RET_QK_W = RET_HEADS * RET_QK_DIM
RET_V_W = RET_HEADS * RET_V_DIM

MLA_HEADS = 16
Q_LORA = 512
KV_LORA = 512
QK_NOPE = 128
QK_ROPE = 64
V_HEAD = 128
MLA_QK_DIM = QK_NOPE + QK_ROPE
MLA_HEAD_PAD = 256

COL_RV = 2 * RET_QK_W
COL_RG = COL_RV + RET_V_W
COL_CQ = COL_RG + RET_V_W
COL_CKV = COL_CQ + Q_LORA
COL_KPE = COL_CKV + KV_LORA
COL_GRET = COL_KPE + QK_ROPE

LOG2_E = 1.4426950408889634
LANES = 128
VMEM_LIMIT = 56 * 1024 * 1024
VMEM_LIMIT_BIG = 62 * 1024 * 1024


def _params(sem, vmem_limit=VMEM_LIMIT):
    return pltpu.CompilerParams(dimension_semantics=sem, vmem_limit_bytes=vmem_limit)


def _rms(x, g):
    return x * lax.rsqrt(jnp.mean(x * x, axis=-1, keepdims=True) + EPS) * g


def _rope128(t, c, s):
    return t * c + pltpu.roll(t, LANES // 2, 1) * s


def _rope64(t, c, s_lo, s_hi):
    return t * c + pltpu.roll(t, LANES - QK_ROPE // 2, 1) * s_lo + pltpu.roll(t, QK_ROPE // 2, 1) * s_hi


def _sigmoid(x):
    return 1.0 / (1.0 + jnp.exp(-x))


def _norm_rope_body(x_ref, g_ref, pos_ref, inv_ref, wt_ref, u_ref, cs_ref, sn_ref, lat_ref, wb_ref):
    @pl.when(pl.program_id(0) == 0)
    def _cast():
        wb_ref[...] = wt_ref[...].astype(BF16)

    u = _rms(x_ref[...], g_ref[...]).astype(BF16)
    u_ref[...] = u
    ang = pos_ref[...].astype(F32) * inv_ref[...]
    cs_ref[...] = jnp.cos(ang)
    sn_ref[...] = jnp.sin(ang)
    lat_ref[...] = lax.dot_general(u, wb_ref[...], (((1,), (1,)), ((), ())),
                                   preferred_element_type=F32).astype(BF16)


def _norm_rope(x, g, pos, w_in_t, rows=512):
    T, D = x.shape
    LW = Q_LORA + KV_LORA
    half_r = RET_QK_DIM // 2
    half_m = QK_ROPE // 2
    inv_r = ROPE_THETA ** (-jnp.arange(half_r, dtype=F32) / half_r)
    inv_m = ROPE_THETA ** (-jnp.arange(half_m, dtype=F32) / half_m)
    inv = jnp.concatenate([inv_r, inv_m, inv_m])[None]
    tab = pl.BlockSpec((rows, LANES), lambda i: (i, 0))
    return pl.pallas_call(
        _norm_rope_body,
        grid=(T // rows,),
        in_specs=[pl.BlockSpec((rows, D), lambda i: (i, 0)), pl.BlockSpec((1, D), lambda i: (0, 0)),
                  pl.BlockSpec((rows, 1), lambda i: (i, 0)), pl.BlockSpec((1, LANES), lambda i: (0, 0)),
                  pl.BlockSpec((LW, D), lambda i: (COL_CQ // LW, 0), pipeline_mode=pl.Buffered(1))],
        out_specs=[pl.BlockSpec((rows, D), lambda i: (i, 0)), tab, tab,
                   pl.BlockSpec((rows, LW), lambda i: (i, 0))],
        out_shape=[jax.ShapeDtypeStruct((T, D), BF16), jax.ShapeDtypeStruct((T, LANES), F32),
                   jax.ShapeDtypeStruct((T, LANES), F32), jax.ShapeDtypeStruct((T, LW), BF16)],
        scratch_shapes=[pltpu.VMEM((LW, D), BF16)],
        compiler_params=_params(("arbitrary",)),
        name="norm_rope_latent",
    )(x, g, pos, inv, w_in_t)


def _ret_rope_tables(cs, sn):
    lo = lax.broadcasted_iota(jnp.int32, cs.shape, 1) < LANES // 2
    return (jnp.where(lo, cs, pltpu.roll(cs, LANES // 2, 1)),
            jnp.where(lo, -sn, pltpu.roll(sn, LANES // 2, 1)))


def _mla_rope_tables(cs, sn):
    lane = lax.broadcasted_iota(jnp.int32, cs.shape, 1)
    half = QK_ROPE // 2
    c = pltpu.roll(cs, LANES // 2, 1)
    s = pltpu.roll(sn, LANES // 2, 1)
    return (jnp.where(lane < QK_ROPE, c, 0.0), jnp.where(lane < half, -s, 0.0),
            jnp.where((lane >= half) & (lane < QK_ROPE), s, 0.0))


def _cast_weight(w_ref, wb_ref):
    @pl.when(pl.program_id(1) == 0)
    def _cast():
        wb_ref[...] = w_ref[...].astype(BF16)


def _inproj_body(u_ref, wt_ref, o_ref, wb_ref):
    @pl.when(pl.program_id(1) == 0)
    def _cast():
        wb_ref[...] = wt_ref[...].astype(BF16)

    o_ref[...] = lax.dot_general(u_ref[...], wb_ref[...], (((1,), (1,)), ((), ())),
                                 preferred_element_type=F32).astype(BF16)


def _inproj(u, w_in_t, tm=2048, tn=1024):
    T, D = u.shape
    n_head = COL_CQ // tn
    n_gate = (w_in_t.shape[0] - COL_GRET) // tn

    def w_rows(j, i):
        row = jnp.where(j < n_head, j * tn, COL_GRET + (j - n_head) * tn)
        return (pl.multiple_of(row, QK_ROPE), 0)

    return pl.pallas_call(
        _inproj_body,
        grid=(n_head + n_gate, T // tm),
        in_specs=[
            pl.BlockSpec((tm, D), lambda j, i: (i, 0)),
            pl.BlockSpec((pl.Element(tn), pl.Element(D)), w_rows),
        ],
        out_specs=pl.BlockSpec((tm, tn), lambda j, i: (i, j)),
        out_shape=jax.ShapeDtypeStruct((T, (n_head + n_gate) * tn), BF16),
        scratch_shapes=[pltpu.VMEM((tn, D), BF16)],
        compiler_params=_params(("parallel", "arbitrary")),
        name="inproj",
    )(u, w_in_t)


def _mla_part(cq_ref, ckv_ref, u_ref, wpe_ref, gq_ref, gkv_ref, wq_ref, wk_ref, wv_ref, cs, sn,
              q_ref, kn_ref, kpe_ref, v_ref):
    qa = _rms(cq_ref[...].astype(F32), gq_ref[...]).astype(BF16)
    kva = _rms(ckv_ref[...].astype(F32), gkv_ref[...]).astype(BF16)
    c, s_lo, s_hi = _mla_rope_tables(cs, sn)
    kpe_raw = lax.dot_general(u_ref[...], wpe_ref[...].astype(BF16), (((1,), (1,)), ((), ())),
                              preferred_element_type=F32)
    kpe_ref[...] = _rope64(kpe_raw, c, s_lo, s_hi).astype(BF16)
    scale = MLA_QK_DIM ** -0.5 * LOG2_E
    cq = c * scale
    sq_lo = s_lo * scale
    sq_hi = s_hi * scale
    q = jnp.dot(qa, wq_ref[...], preferred_element_type=F32)
    for h in range(MLA_HEADS):
        b = h * MLA_HEAD_PAD
        q_ref[:, b:b + QK_NOPE] = (q[:, b:b + QK_NOPE] * scale).astype(BF16)
        q_ref[:, b + QK_NOPE:b + MLA_HEAD_PAD] = _rope64(
            q[:, b + QK_NOPE:b + MLA_HEAD_PAD], cq, sq_lo, sq_hi).astype(BF16)
    kn_ref[...] = jnp.dot(kva, wk_ref[...], preferred_element_type=F32).astype(BF16)
    v_ref[...] = jnp.dot(kva, wv_ref[...], preferred_element_type=F32).astype(BF16)


def _ret_tables_init(lg_ref, st_ref, dec_ref, qdec_ref, kdec_ref, bc):
    st_ref[...] = jnp.zeros_like(st_ref)
    n = lax.broadcasted_iota(jnp.int32, (bc, bc), 0)
    m = lax.broadcasted_iota(jnp.int32, (bc, bc), 1)
    dist = jnp.abs(n - m).astype(F32)
    visible = (m // CHUNK) <= (n // CHUNK)
    nl = lax.broadcasted_iota(jnp.int32, (bc, LANES), 0).astype(F32)
    for h in range(RET_HEADS):
        lg = lg_ref[h]
        dec_ref[h] = jnp.where(visible, jnp.exp(lg * dist), 0.0)
        qdec_ref[h] = jnp.exp(lg * (nl + 1.0))
        kdec_ref[h] = jnp.exp(lg * (bc - 1.0 - nl))


def _ret_block(rows, lg_ref, q_ref, k_ref, v_ref, rg_ref, cs, sn, g_ref, o_ref,
               st_ref, dec_ref, qdec_ref, kdec_ref, bc):
    c, s_ = _ret_rope_tables(cs, sn)
    k_scale = RET_QK_DIM ** -0.5
    ck = c * k_scale
    sk = s_ * k_scale
    for h in range(RET_HEADS):
        qs = slice(h * RET_QK_DIM, (h + 1) * RET_QK_DIM)
        vs = slice(h * RET_V_DIM, (h + 1) * RET_V_DIM)
        q32 = _rope128(q_ref[rows, qs].astype(F32), c, s_)
        k32 = _rope128(k_ref[rows, qs].astype(F32), ck, sk)
        q = q32.astype(BF16)
        k = k32.astype(BF16)
        v = v_ref[rows, vs]
        s = lax.dot_general(q, k, (((1,), (1,)), ((), ())), preferred_element_type=F32) * dec_ref[h]
        qd = (q32 * qdec_ref[h]).astype(BF16)
        st = st_ref[h]
        o = (jnp.dot(s.astype(BF16), v, preferred_element_type=F32)
             + jnp.dot(qd, st.astype(BF16), preferred_element_type=F32))
        kd = (k32 * kdec_ref[h]).astype(BF16)
        c_dec = jnp.exp(jnp.full((1, RET_V_DIM), lg_ref[h] * bc, F32))
        st_ref[h] = st * c_dec + lax.dot_general(
            kd, v, (((0,), (0,)), ((), ())), preferred_element_type=F32)
        mu = jnp.mean(o, axis=-1, keepdims=True)
        oc = o - mu
        var = jnp.mean(oc * oc, axis=-1, keepdims=True)
        y = oc * lax.rsqrt(var + EPS) * g_ref[:, vs]
        rg = rg_ref[rows, vs].astype(F32)
        o_ref[rows, vs] = (y * (rg * _sigmoid(rg))).astype(BF16)


def _branches_body(lg_ref, cq_ref, ckv_ref, u_ref, wpe_ref, gq_ref, gkv_ref, wq_ref, wk_ref, wv_ref,
                   cs_ref, sn_ref, rq_ref, rk_ref, rv_ref, rg_ref, gr_ref,
                   q_ref, kn_ref, kpe_ref, v_ref, ry_ref, st_ref, dec_ref, qdec_ref, kdec_ref,
                   *, tm, bc, tiles_per_seq):
    @pl.when(pl.program_id(0) % tiles_per_seq == 0)
    def _init():
        _ret_tables_init(lg_ref, st_ref, dec_ref, qdec_ref, kdec_ref, bc)

    _mla_part(cq_ref, ckv_ref, u_ref, wpe_ref, gq_ref, gkv_ref, wq_ref, wk_ref, wv_ref,
              cs_ref[...], sn_ref[...], q_ref, kn_ref, kpe_ref, v_ref)
    for blk in range(tm // bc):
        rows = slice(blk * bc, (blk + 1) * bc)
        _ret_block(rows, lg_ref, rq_ref, rk_ref, rv_ref, rg_ref, cs_ref[rows, :], sn_ref[rows, :],
                   gr_ref, ry_ref, st_ref, dec_ref, qdec_ref, kdec_ref, bc)


def _branches(proj, lat, u, w_in_t, gq, gkv, wq, wk, wv, cs, sn, lg, g_ret, seq, tm=512, bc=256):
    T, D = u.shape
    const = lambda shape: pl.BlockSpec(shape, lambda i, lg: (0, 0), pipeline_mode=pl.Buffered(1))
    rows = lambda w, col=0: pl.BlockSpec((tm, w), lambda i, lg: (i, col))
    QW = MLA_HEADS * MLA_HEAD_PAD
    KW = MLA_HEADS * QK_NOPE
    VW = MLA_HEADS * V_HEAD
    grid_spec = pltpu.PrefetchScalarGridSpec(
        num_scalar_prefetch=1,
        grid=(T // tm,),
        in_specs=[rows(Q_LORA, 0), rows(KV_LORA, 1), rows(D),
                  pl.BlockSpec((LANES, D), lambda i, lg: (COL_KPE // LANES, 0),
                               pipeline_mode=pl.Buffered(1)),
                  const((1, Q_LORA)), const((1, KV_LORA)),
                  const(wq.shape), const(wk.shape), const(wv.shape),
                  rows(LANES), rows(LANES),
                  rows(RET_QK_W, 0), rows(RET_QK_W, 1),
                  rows(RET_V_W, COL_RV // RET_V_W), rows(RET_V_W, COL_RG // RET_V_W),
                  const((1, RET_V_W))],
        out_specs=[rows(QW), rows(KW), rows(LANES), rows(VW), rows(RET_V_W)],
        scratch_shapes=[pltpu.VMEM((RET_HEADS, RET_QK_DIM, RET_V_DIM), F32),
                        pltpu.VMEM((RET_HEADS, bc, bc), F32),
                        pltpu.VMEM((RET_HEADS, bc, LANES), F32),
                        pltpu.VMEM((RET_HEADS, bc, LANES), F32)],
    )
    return pl.pallas_call(
        functools.partial(_branches_body, tm=tm, bc=bc, tiles_per_seq=seq // tm),
        grid_spec=grid_spec,
        out_shape=[jax.ShapeDtypeStruct((T, QW), BF16), jax.ShapeDtypeStruct((T, KW), BF16),
                   jax.ShapeDtypeStruct((T, LANES), BF16), jax.ShapeDtypeStruct((T, VW), BF16),
                   jax.ShapeDtypeStruct((T, RET_V_W), BF16)],
        compiler_params=_params(("arbitrary",), VMEM_LIMIT_BIG),
        name="mla_proj_retention",
    )(lg, lat, lat, u, w_in_t, gq, gkv, wq, wk, wv, cs, sn, proj, proj, proj, proj, g_ret)


def _attn_body(q_ref, kn_ref, kpe_ref, v_ref, o_ref, m_ref, acc_ref, *, tq, tk, nh):
    qi = pl.program_id(2)
    m_ref[...] = jnp.full_like(m_ref, -jnp.inf)
    acc_ref[...] = jnp.zeros_like(acc_ref)
    reps = tk // LANES

    def step(start, diag_off):
        r0 = 0 if diag_off is None else diag_off
        rows = tq - r0
        for hh in range(nh):
            q = q_ref[r0:, hh * MLA_HEAD_PAD:(hh + 1) * MLA_HEAD_PAD]
            k = jnp.concatenate([kn_ref[pl.ds(start, tk), hh * QK_NOPE:(hh + 1) * QK_NOPE],
                                 kpe_ref[pl.ds(start, tk), :]], axis=1)
            v = v_ref[pl.ds(start, tk), hh * V_HEAD:(hh + 1) * V_HEAD]
            s = lax.dot_general(q, k, (((1,), (1,)), ((), ())), preferred_element_type=F32)
            if diag_off is not None:
                rq = lax.broadcasted_iota(jnp.int32, (rows, tk), 0) // CHUNK
                ck = lax.broadcasted_iota(jnp.int32, (rows, tk), 1) // CHUNK
                s = jnp.where(ck <= rq, s, -jnp.inf)
            m_old = m_ref[hh, r0:, :]
            m_new = jnp.maximum(m_old, jnp.max(s, axis=-1, keepdims=True))
            alpha = jnp.exp2(m_old - m_new)
            p = jnp.exp2(s - jnp.concatenate([m_new] * reps, axis=1))
            pv = jnp.dot(p.astype(BF16), jnp.concatenate([v, jnp.ones_like(v)], axis=1),
                         preferred_element_type=F32)
            acc_ref[hh, r0:, :] = jnp.concatenate([alpha, alpha], axis=1) * acc_ref[hh, r0:, :] + pv
            m_ref[hh, r0:, :] = m_new

    def full_steps(kb, carry):
        for j in range(tq // tk):
            step(pl.multiple_of(kb * tq + j * tk, tk), None)
        return carry

    lax.fori_loop(0, qi, full_steps, 0)
    for j in range(tq // tk):
        step(pl.multiple_of(qi * tq + j * tk, tk), j * tk)
    for hh in range(nh):
        o_ref[:, hh * V_HEAD:(hh + 1) * V_HEAD] = (
            acc_ref[hh, :, :V_HEAD] / acc_ref[hh, :, V_HEAD:]).astype(BF16)


def _attention(q, kn, kpe, v, batch, seq, tq=2048, tk=512, nh=2):
    nq = seq // tq
    return pl.pallas_call(
        functools.partial(_attn_body, tq=tq, tk=tk, nh=nh),
        grid=(batch, MLA_HEADS // nh, nq),
        in_specs=[
            pl.BlockSpec((tq, nh * MLA_HEAD_PAD), lambda b, h, i: (b * nq + i, h)),
            pl.BlockSpec((seq, nh * QK_NOPE), lambda b, h, i: (b, h)),
            pl.BlockSpec((seq, LANES), lambda b, h, i: (b, 0)),
            pl.BlockSpec((seq, nh * V_HEAD), lambda b, h, i: (b, h)),
        ],
        out_specs=pl.BlockSpec((tq, nh * V_HEAD), lambda b, h, i: (b * nq + i, h)),
        out_shape=jax.ShapeDtypeStruct((batch * seq, MLA_HEADS * V_HEAD), BF16),
        scratch_shapes=[pltpu.VMEM((nh, tq, LANES), F32), pltpu.VMEM((nh, tq, 2 * V_HEAD), F32)],
        compiler_params=_params(("parallel", "parallel", "arbitrary")),
        name="mla_attention",
    )(q, kn, kpe, v)


def _merge_body(a_ref, b_ref, wa_ref, wb_ref, ga_ref, gb_ref, o_ref, wab_ref, wbb_ref):
    _cast_weight(wa_ref, wab_ref)
    _cast_weight(wb_ref, wbb_ref)
    ya = jnp.dot(a_ref[...], wab_ref[...], preferred_element_type=F32)
    yb = jnp.dot(b_ref[...], wbb_ref[...], preferred_element_type=F32)
    ga = _sigmoid(ga_ref[...].astype(F32))
    gb = _sigmoid(gb_ref[...].astype(F32))
    o_ref[...] = (ga * ya + gb * yb).astype(BF16)


def _merge(ry, my, w_ret_o, w_mla_o, proj, gate_col, tm=512, tn=1024):
    T, K = ry.shape
    N = w_ret_o.shape[1]
    nj = N // tn
    g0 = gate_col // tn
    wspec = pl.BlockSpec((K, tn), lambda j, i: (0, j), pipeline_mode=pl.Buffered(1))
    return pl.pallas_call(
        _merge_body,
        grid=(nj, T // tm),
        in_specs=[
            pl.BlockSpec((tm, K), lambda j, i: (i, 0)),
            pl.BlockSpec((tm, K), lambda j, i: (i, 0)),
            wspec, wspec,
            pl.BlockSpec((tm, tn), lambda j, i: (i, g0 + j)),
            pl.BlockSpec((tm, tn), lambda j, i: (i, g0 + nj + j)),
        ],
        out_specs=pl.BlockSpec((tm, tn), lambda j, i: (i, j)),
        out_shape=jax.ShapeDtypeStruct((T, N), BF16),
        scratch_shapes=[pltpu.VMEM((K, tn), BF16), pltpu.VMEM((K, tn), BF16)],
        compiler_params=_params(("parallel", "arbitrary")),
        name="gated_merge",
    )(ry, my, w_ret_o, w_mla_o, proj, proj)


def _outproj_body(a_ref, w_ref, x_ref, o_ref, wb_ref):
    _cast_weight(w_ref, wb_ref)
    o_ref[...] = x_ref[...] + jnp.dot(a_ref[...], wb_ref[...], preferred_element_type=F32)


def _outproj(a, w, x, tm=1024, tn=1024):
    T, K = a.shape
    N = w.shape[1]
    return pl.pallas_call(
        _outproj_body,
        grid=(N // tn, T // tm),
        in_specs=[
            pl.BlockSpec((tm, K), lambda j, i: (i, 0)),
            pl.BlockSpec((K, tn), lambda j, i: (0, j), pipeline_mode=pl.Buffered(1)),
            pl.BlockSpec((tm, tn), lambda j, i: (i, j)),
        ],
        out_specs=pl.BlockSpec((tm, tn), lambda j, i: (i, j)),
        out_shape=jax.ShapeDtypeStruct((T, N), F32),
        scratch_shapes=[pltpu.VMEM((K, tn), BF16)],
        compiler_params=_params(("parallel", "arbitrary")),
        name="out_proj",
    )(a, w, x)


def _mlp_body(h_ref, g_ref, wu_ref, wd_ref, gf_ref, o_ref, hn_ref):
    f = pl.program_id(1)

    @pl.when(f == 0)
    def _init():
        hn_ref[...] = _rms(h_ref[...], g_ref[...]).astype(BF16)
        o_ref[...] = jnp.zeros_like(o_ref)

    z = jnp.dot(hn_ref[...], wu_ref[...].astype(BF16), preferred_element_type=F32)
    a = jnp.square(jnp.maximum(z, 0.0)).astype(BF16)
    o_ref[...] += jnp.dot(a, wd_ref[...].astype(BF16), preferred_element_type=F32)

    @pl.when(f == pl.num_programs(1) - 1)
    def _finish():
        o_ref[...] = _rms(h_ref[...] + o_ref[...], gf_ref[...])


def _mlp(h, g, w_up, w_down, gf, tm=1024, tf=512):
    T, D = h.shape
    FF = w_up.shape[1]
    return pl.pallas_call(
        _mlp_body,
        grid=(T // tm, FF // tf),
        in_specs=[
            pl.BlockSpec((tm, D), lambda i, f: (i, 0)),
            pl.BlockSpec((1, D), lambda i, f: (0, 0)),
            pl.BlockSpec((D, tf), lambda i, f: (0, f)),
            pl.BlockSpec((tf, D), lambda i, f: (f, 0)),
            pl.BlockSpec((1, D), lambda i, f: (0, 0)),
        ],
        out_specs=pl.BlockSpec((tm, D), lambda i, f: (i, 0)),
        out_shape=jax.ShapeDtypeStruct((T, D), F32),
        scratch_shapes=[pltpu.VMEM((tm, D), BF16)],
        compiler_params=_params(("parallel", "arbitrary"), VMEM_LIMIT_BIG),
        name="mlp_final_norm",
    )(h, g, w_up, w_down, gf)


def _prep_q_weights(w_q_b):
    w = w_q_b.reshape(Q_LORA, MLA_HEADS, MLA_QK_DIM)
    w = jnp.pad(w, ((0, 0), (0, 0), (0, MLA_HEAD_PAD - MLA_QK_DIM)))
    return w.reshape(Q_LORA, MLA_HEADS * MLA_HEAD_PAD).astype(BF16)


def _prep_kv_weights(w_kv_b):
    w = w_kv_b.reshape(KV_LORA, MLA_HEADS, QK_NOPE + V_HEAD)
    wk = w[..., :QK_NOPE].reshape(KV_LORA, MLA_HEADS * QK_NOPE).astype(BF16)
    wv = w[..., QK_NOPE:].reshape(KV_LORA, MLA_HEADS * V_HEAD).astype(BF16)
    return wk, wv


def kernel(x, positions, norm_mix_g, w_in, ret_norm_g, w_ret_o, q_a_norm_g, w_q_b, kv_a_norm_g,
           w_kv_b, w_mla_o, w_out, norm_mlp_g, w_up, w_down, norm_f_g):
    B, S, D = x.shape
    T = B * S
    assert w_in.shape[0] == 1, "single-layer block: the final norm is fused into the MLP kernel"
    log_gamma = jnp.log(1.0 - 2.0 ** (-5.0 - jnp.arange(RET_HEADS, dtype=F32)))
    h = x.reshape(T, D)
    w_in_t = jnp.swapaxes(w_in[0], 0, 1)
    wq = _prep_q_weights(w_q_b[0])
    wk, wv = _prep_kv_weights(w_kv_b[0])
    u, cs, sn, lat = _norm_rope(h, norm_mix_g[0][None], positions.reshape(T, 1), w_in_t)
    proj = _inproj(u, w_in_t)
    q, kn, kpe, v, ry = _branches(proj, lat, u, w_in_t, q_a_norm_g[0][None], kv_a_norm_g[0][None],
                            wq, wk, wv, cs, sn, log_gamma, ret_norm_g[0][None], S)
    my = _attention(q, kn, kpe, v, B, S)
    merged = _merge(ry, my, w_ret_o[0], w_mla_o[0], proj, COL_CQ)
    h1 = _outproj(merged, w_out[0], h)
    out = _mlp(h1, norm_mlp_g[0][None], w_up[0], w_down[0], norm_f_g[None])
    return out.reshape(B, S, D)
```

```python
import functools

import jax
import jax.numpy as jnp
from jax import lax
from jax.experimental import pallas as pl
from jax.experimental.pallas import tpu as pltpu

F32 = jnp.float32
BF16 = jnp.bfloat16

EPS = 1e-6
ROPE_THETA = 10000.0
CHUNK = 64

RET_HEADS = 8
RET_QK_DIM = 128
RET_V_DIM = 256
RET_QK_W = RET_HEADS * RET_QK_DIM
RET_V_W = RET_HEADS * RET_V_DIM

MLA_HEADS = 16
Q_LORA = 512
KV_LORA = 512
QK_NOPE = 128
QK_ROPE = 64
V_HEAD = 128
MLA_QK_DIM = QK_NOPE + QK_ROPE
MLA_HEAD_PAD = 256

COL_RV = 2 * RET_QK_W
COL_RG = COL_RV + RET_V_W
COL_CQ = COL_RG + RET_V_W
COL_CKV = COL_CQ + Q_LORA
COL_KPE = COL_CKV + KV_LORA
COL_GRET = COL_KPE + QK_ROPE

LOG2_E = 1.4426950408889634
LANES = 128
VMEM_LIMIT = 56 * 1024 * 1024
VMEM_LIMIT_BIG = 62 * 1024 * 1024


def _params(sem, vmem_limit=VMEM_LIMIT):
    return pltpu.CompilerParams(dimension_semantics=sem, vmem_limit_bytes=vmem_limit)


def _rms(x, g):
    return x * lax.rsqrt(jnp.mean(x * x, axis=-1, keepdims=True) + EPS) * g


def _rope128(t, c, s):
    return t * c + pltpu.roll(t, LANES // 2, 1) * s


def _rope64(t, c, s_lo, s_hi):
    return t * c + pltpu.roll(t, LANES - QK_ROPE // 2, 1) * s_lo + pltpu.roll(t, QK_ROPE // 2, 1) * s_hi


def _sigmoid(x):
    return 1.0 / (1.0 + jnp.exp(-x))


def _norm_rope_body(x_ref, g_ref, pos_ref, inv_ref, wt_ref, u_ref, cs_ref, sn_ref, lat_ref, wb_ref):
    @pl.when(pl.program_id(0) == 0)
    def _cast():
        wb_ref[...] = wt_ref[...].astype(BF16)

    u = _rms(x_ref[...], g_ref[...]).astype(BF16)
    u_ref[...] = u
    ang = pos_ref[...].astype(F32) * inv_ref[...]
    cs_ref[...] = jnp.cos(ang)
    sn_ref[...] = jnp.sin(ang)
    lat_ref[...] = lax.dot_general(u, wb_ref[...], (((1,), (1,)), ((), ())),
                                   preferred_element_type=F32).astype(BF16)


def _norm_rope(x, g, pos, w_in_t, rows=512):
    T, D = x.shape
    LW = Q_LORA + KV_LORA
    half_r = RET_QK_DIM // 2
    half_m = QK_ROPE // 2
    inv_r = ROPE_THETA ** (-jnp.arange(half_r, dtype=F32) / half_r)
    inv_m = ROPE_THETA ** (-jnp.arange(half_m, dtype=F32) / half_m)
    inv = jnp.concatenate([inv_r, inv_m, inv_m])[None]
    tab = pl.BlockSpec((rows, LANES), lambda i: (i, 0))
    return pl.pallas_call(
        _norm_rope_body,
        grid=(T // rows,),
        in_specs=[pl.BlockSpec((rows, D), lambda i: (i, 0)), pl.BlockSpec((1, D), lambda i: (0, 0)),
                  pl.BlockSpec((rows, 1), lambda i: (i, 0)), pl.BlockSpec((1, LANES), lambda i: (0, 0)),
                  pl.BlockSpec((LW, D), lambda i: (COL_CQ // LW, 0), pipeline_mode=pl.Buffered(1))],
        out_specs=[pl.BlockSpec((rows, D), lambda i: (i, 0)), tab, tab,
                   pl.BlockSpec((rows, LW), lambda i: (i, 0))],
        out_shape=[jax.ShapeDtypeStruct((T, D), BF16), jax.ShapeDtypeStruct((T, LANES), F32),
                   jax.ShapeDtypeStruct((T, LANES), F32), jax.ShapeDtypeStruct((T, LW), BF16)],
        scratch_shapes=[pltpu.VMEM((LW, D), BF16)],
        compiler_params=_params(("arbitrary",)),
        name="norm_rope_latent",
    )(x, g, pos, inv, w_in_t)


def _ret_rope_tables(cs, sn):
    lo = lax.broadcasted_iota(jnp.int32, cs.shape, 1) < LANES // 2
    return (jnp.where(lo, cs, pltpu.roll(cs, LANES // 2, 1)),
            jnp.where(lo, -sn, pltpu.roll(sn, LANES // 2, 1)))


def _mla_rope_tables(cs, sn):
    lane = lax.broadcasted_iota(jnp.int32, cs.shape, 1)
    half = QK_ROPE // 2
    c = pltpu.roll(cs, LANES // 2, 1)
    s = pltpu.roll(sn, LANES // 2, 1)
    return (jnp.where(lane < QK_ROPE, c, 0.0), jnp.where(lane < half, -s, 0.0),
            jnp.where((lane >= half) & (lane < QK_ROPE), s, 0.0))


def _cast_weight(w_ref, wb_ref):
    @pl.when(pl.program_id(1) == 0)
    def _cast():
        wb_ref[...] = w_ref[...].astype(BF16)


def _inproj_body(u_ref, wt_ref, o_ref, wb_ref):
    @pl.when(pl.program_id(1) == 0)
    def _cast():
        wb_ref[...] = wt_ref[...].astype(BF16)

    o_ref[...] = lax.dot_general(u_ref[...], wb_ref[...], (((1,), (1,)), ((), ())),
                                 preferred_element_type=F32).astype(BF16)


def _inproj(u, w_in_t, tm=2048, tn=1024):
    T, D = u.shape
    n_head = COL_CQ // tn
    n_gate = (w_in_t.shape[0] - COL_GRET) // tn

    def w_rows(j, i):
        row = jnp.where(j < n_head, j * tn, COL_GRET + (j - n_head) * tn)
        return (pl.multiple_of(row, QK_ROPE), 0)

    return pl.pallas_call(
        _inproj_body,
        grid=(n_head + n_gate, T // tm),
        in_specs=[
            pl.BlockSpec((tm, D), lambda j, i: (i, 0)),
            pl.BlockSpec((pl.Element(tn), pl.Element(D)), w_rows),
        ],
        out_specs=pl.BlockSpec((tm, tn), lambda j, i: (i, j)),
        out_shape=jax.ShapeDtypeStruct((T, (n_head + n_gate) * tn), BF16),
        scratch_shapes=[pltpu.VMEM((tn, D), BF16)],
        compiler_params=_params(("parallel", "arbitrary")),
        name="inproj",
    )(u, w_in_t)


def _mla_part(cq_ref, ckv_ref, u_ref, wpe_ref, gq_ref, gkv_ref, wq_ref, wk_ref, wv_ref, cs, sn,
              q_ref, kn_ref, kpe_ref, v_ref):
    qa = _rms(cq_ref[...].astype(F32), gq_ref[...]).astype(BF16)
    kva = _rms(ckv_ref[...].astype(F32), gkv_ref[...]).astype(BF16)
    c, s_lo, s_hi = _mla_rope_tables(cs, sn)
    kpe_raw = lax.dot_general(u_ref[...], wpe_ref[...].astype(BF16), (((1,), (1,)), ((), ())),
                              preferred_element_type=F32)
    kpe_ref[...] = _rope64(kpe_raw, c, s_lo, s_hi).astype(BF16)
    scale = MLA_QK_DIM ** -0.5 * LOG2_E
    cq = c * scale
    sq_lo = s_lo * scale
    sq_hi = s_hi * scale
    q = jnp.dot(qa, wq_ref[...], preferred_element_type=F32)
    for h in range(MLA_HEADS):
        b = h * MLA_HEAD_PAD
        q_ref[:, b:b + QK_NOPE] = (q[:, b:b + QK_NOPE] * scale).astype(BF16)
        q_ref[:, b + QK_NOPE:b + MLA_HEAD_PAD] = _rope64(
            q[:, b + QK_NOPE:b + MLA_HEAD_PAD], cq, sq_lo, sq_hi).astype(BF16)
    kn_ref[...] = jnp.dot(kva, wk_ref[...], preferred_element_type=F32).astype(BF16)
    v_ref[...] = jnp.dot(kva, wv_ref[...], preferred_element_type=F32).astype(BF16)


def _ret_tables_init(lg_ref, st_ref, dec_ref, qdec_ref, kdec_ref, bc):
    st_ref[...] = jnp.zeros_like(st_ref)
    n = lax.broadcasted_iota(jnp.int32, (bc, bc), 0)
    m = lax.broadcasted_iota(jnp.int32, (bc, bc), 1)
    dist = jnp.abs(n - m).astype(F32)
    visible = (m // CHUNK) <= (n // CHUNK)
    nl = lax.broadcasted_iota(jnp.int32, (bc, LANES), 0).astype(F32)
    for h in range(RET_HEADS):
        lg = lg_ref[h]
        dec_ref[h] = jnp.where(visible, jnp.exp(lg * dist), 0.0)
        qdec_ref[h] = jnp.exp(lg * (nl + 1.0))
        kdec_ref[h] = jnp.exp(lg * (bc - 1.0 - nl))


def _ret_block(rows, lg_ref, q_ref, k_ref, v_ref, rg_ref, cs, sn, g_ref, o_ref,
               st_ref, dec_ref, qdec_ref, kdec_ref, bc):
    c, s_ = _ret_rope_tables(cs, sn)
    k_scale = RET_QK_DIM ** -0.5
    ck = c * k_scale
    sk = s_ * k_scale
    for h in range(RET_HEADS):
        qs = slice(h * RET_QK_DIM, (h + 1) * RET_QK_DIM)
        vs = slice(h * RET_V_DIM, (h + 1) * RET_V_DIM)
        q32 = _rope128(q_ref[rows, qs].astype(F32), c, s_)
        k32 = _rope128(k_ref[rows, qs].astype(F32), ck, sk)
        q = q32.astype(BF16)
        k = k32.astype(BF16)
        v = v_ref[rows, vs]
        s = lax.dot_general(q, k, (((1,), (1,)), ((), ())), preferred_element_type=F32) * dec_ref[h]
        qd = (q32 * qdec_ref[h]).astype(BF16)
        st = st_ref[h]
        o = (jnp.dot(s.astype(BF16), v, preferred_element_type=F32)
             + jnp.dot(qd, st.astype(BF16), preferred_element_type=F32))
        kd = (k32 * kdec_ref[h]).astype(BF16)
        c_dec = jnp.exp(jnp.full((1, RET_V_DIM), lg_ref[h] * bc, F32))
        st_ref[h] = st * c_dec + lax.dot_general(
            kd, v, (((0,), (0,)), ((), ())), preferred_element_type=F32)
        mu = jnp.mean(o, axis=-1, keepdims=True)
        oc = o - mu
        var = jnp.mean(oc * oc, axis=-1, keepdims=True)
        y = oc * lax.rsqrt(var + EPS) * g_ref[:, vs]
        rg = rg_ref[rows, vs].astype(F32)
        o_ref[rows, vs] = (y * (rg * _sigmoid(rg))).astype(BF16)


def _branches_body(lg_ref, cq_ref, ckv_ref, u_ref, wpe_ref, gq_ref, gkv_ref, wq_ref, wk_ref, wv_ref,
                   cs_ref, sn_ref, rq_ref, rk_ref, rv_ref, rg_ref, gr_ref,
                   q_ref, kn_ref, kpe_ref, v_ref, ry_ref, st_ref, dec_ref, qdec_ref, kdec_ref,
                   *, tm, bc, tiles_per_seq):
    @pl.when(pl.program_id(0) % tiles_per_seq == 0)
    def _init():
        _ret_tables_init(lg_ref, st_ref, dec_ref, qdec_ref, kdec_ref, bc)

    _mla_part(cq_ref, ckv_ref, u_ref, wpe_ref, gq_ref, gkv_ref, wq_ref, wk_ref, wv_ref,
              cs_ref[...], sn_ref[...], q_ref, kn_ref, kpe_ref, v_ref)
    for blk in range(tm // bc):
        rows = slice(blk * bc, (blk + 1) * bc)
        _ret_block(rows, lg_ref, rq_ref, rk_ref, rv_ref, rg_ref, cs_ref[rows, :], sn_ref[rows, :],
                   gr_ref, ry_ref, st_ref, dec_ref, qdec_ref, kdec_ref, bc)


def _branches(proj, lat, u, w_in_t, gq, gkv, wq, wk, wv, cs, sn, lg, g_ret, seq, tm=512, bc=256):
    T, D = u.shape
    const = lambda shape: pl.BlockSpec(shape, lambda i, lg: (0, 0), pipeline_mode=pl.Buffered(1))
    rows = lambda w, col=0: pl.BlockSpec((tm, w), lambda i, lg: (i, col))
    QW = MLA_HEADS * MLA_HEAD_PAD
    KW = MLA_HEADS * QK_NOPE
    VW = MLA_HEADS * V_HEAD
    grid_spec = pltpu.PrefetchScalarGridSpec(
        num_scalar_prefetch=1,
        grid=(T // tm,),
        in_specs=[rows(Q_LORA, 0), rows(KV_LORA, 1), rows(D),
                  pl.BlockSpec((LANES, D), lambda i, lg: (COL_KPE // LANES, 0),
                               pipeline_mode=pl.Buffered(1)),
                  const((1, Q_LORA)), const((1, KV_LORA)),
                  const(wq.shape), const(wk.shape), const(wv.shape),
                  rows(LANES), rows(LANES),
                  rows(RET_QK_W, 0), rows(RET_QK_W, 1),
                  rows(RET_V_W, COL_RV // RET_V_W), rows(RET_V_W, COL_RG // RET_V_W),
                  const((1, RET_V_W))],
        out_specs=[rows(QW), rows(KW), rows(LANES), rows(VW), rows(RET_V_W)],
        scratch_shapes=[pltpu.VMEM((RET_HEADS, RET_QK_DIM, RET_V_DIM), F32),
                        pltpu.VMEM((RET_HEADS, bc, bc), F32),
                        pltpu.VMEM((RET_HEADS, bc, LANES), F32),
                        pltpu.VMEM((RET_HEADS, bc, LANES), F32)],
    )
    return pl.pallas_call(
        functools.partial(_branches_body, tm=tm, bc=bc, tiles_per_seq=seq // tm),
        grid_spec=grid_spec,
        out_shape=[jax.ShapeDtypeStruct((T, QW), BF16), jax.ShapeDtypeStruct((T, KW), BF16),
                   jax.ShapeDtypeStruct((T, LANES), BF16), jax.ShapeDtypeStruct((T, VW), BF16),
                   jax.ShapeDtypeStruct((T, RET_V_W), BF16)],
        compiler_params=_params(("arbitrary",), VMEM_LIMIT_BIG),
        name="mla_proj_retention",
    )(lg, lat, lat, u, w_in_t, gq, gkv, wq, wk, wv, cs, sn, proj, proj, proj, proj, g_ret)


def _attn_body(q_ref, kn_ref, kpe_ref, v_ref, wu_ref, wd_ref, o_ref, wub_ref, wdb_ref, m_ref, acc_ref,
               *, tq, tk, nh):
    wub_ref[...] = wu_ref[...].astype(BF16)
    wdb_ref[...] = wd_ref[...].astype(BF16)
    qi = pl.program_id(2)
    m_ref[...] = jnp.full_like(m_ref, -jnp.inf)
    acc_ref[...] = jnp.zeros_like(acc_ref)
    reps = tk // LANES

    def step(start, diag_off):
        r0 = 0 if diag_off is None else diag_off
        rows = tq - r0
        for hh in range(nh):
            q = q_ref[r0:, hh * MLA_HEAD_PAD:(hh + 1) * MLA_HEAD_PAD]
            k = jnp.concatenate([kn_ref[pl.ds(start, tk), hh * QK_NOPE:(hh + 1) * QK_NOPE],
                                 kpe_ref[pl.ds(start, tk), :]], axis=1)
            v = v_ref[pl.ds(start, tk), hh * V_HEAD:(hh + 1) * V_HEAD]
            s = lax.dot_general(q, k, (((1,), (1,)), ((), ())), preferred_element_type=F32)
            if diag_off is not None:
                rq = lax.broadcasted_iota(jnp.int32, (rows, tk), 0) // CHUNK
                ck = lax.broadcasted_iota(jnp.int32, (rows, tk), 1) // CHUNK
                s = jnp.where(ck <= rq, s, -jnp.inf)
            m_old = m_ref[hh, r0:, :]
            m_new = jnp.maximum(m_old, jnp.max(s, axis=-1, keepdims=True))
            alpha = jnp.exp2(m_old - m_new)
            p = jnp.exp2(s - jnp.concatenate([m_new] * reps, axis=1))
            pv = jnp.dot(p.astype(BF16), jnp.concatenate([v, jnp.ones_like(v)], axis=1),
                         preferred_element_type=F32)
            acc_ref[hh, r0:, :] = jnp.concatenate([alpha, alpha], axis=1) * acc_ref[hh, r0:, :] + pv
            m_ref[hh, r0:, :] = m_new

    def full_steps(kb, carry):
        for j in range(tq // tk):
            step(pl.multiple_of(kb * tq + j * tk, tk), None)
        return carry

    lax.fori_loop(0, qi, full_steps, 0)
    for j in range(tq // tk):
        step(pl.multiple_of(qi * tq + j * tk, tk), j * tk)
    for hh in range(nh):
        o_ref[:, hh * V_HEAD:(hh + 1) * V_HEAD] = (
            acc_ref[hh, :, :V_HEAD] / acc_ref[hh, :, V_HEAD:]).astype(BF16)


def _attention(q, kn, kpe, v, w_up, w_down, batch, seq, tq=2048, tk=512, nh=2):
    nq = seq // tq
    nhp = MLA_HEADS // nh
    steps = batch * nhp * nq
    D, FF = w_up.shape
    fs = FF // steps
    assert FF % steps == 0 and fs % LANES == 0
    step = lambda b, h, i: (b * nhp + h) * nq + i
    return pl.pallas_call(
        functools.partial(_attn_body, tq=tq, tk=tk, nh=nh),
        grid=(batch, nhp, nq),
        in_specs=[
            pl.BlockSpec((tq, nh * MLA_HEAD_PAD), lambda b, h, i: (b * nq + i, h)),
            pl.BlockSpec((seq, nh * QK_NOPE), lambda b, h, i: (b, h)),
            pl.BlockSpec((seq, LANES), lambda b, h, i: (b, 0)),
            pl.BlockSpec((seq, nh * V_HEAD), lambda b, h, i: (b, h)),
            pl.BlockSpec((D, fs), lambda b, h, i: (0, step(b, h, i))),
            pl.BlockSpec((fs, D), lambda b, h, i: (step(b, h, i), 0)),
        ],
        out_specs=[pl.BlockSpec((tq, nh * V_HEAD), lambda b, h, i: (b * nq + i, h)),
                   pl.BlockSpec((D, fs), lambda b, h, i: (0, step(b, h, i))),
                   pl.BlockSpec((fs, D), lambda b, h, i: (step(b, h, i), 0))],
        out_shape=[jax.ShapeDtypeStruct((batch * seq, MLA_HEADS * V_HEAD), BF16),
                   jax.ShapeDtypeStruct((D, FF), BF16), jax.ShapeDtypeStruct((FF, D), BF16)],
        scratch_shapes=[pltpu.VMEM((nh, tq, LANES), F32), pltpu.VMEM((nh, tq, 2 * V_HEAD), F32)],
        compiler_params=_params(("parallel", "parallel", "arbitrary")),
        name="mla_attention",
    )(q, kn, kpe, v, w_up, w_down)


def _merge_body(a_ref, b_ref, wa_ref, wb_ref, ga_ref, gb_ref, o_ref, wab_ref, wbb_ref):
    _cast_weight(wa_ref, wab_ref)
    _cast_weight(wb_ref, wbb_ref)
    ya = jnp.dot(a_ref[...], wab_ref[...], preferred_element_type=F32)
    yb = jnp.dot(b_ref[...], wbb_ref[...], preferred_element_type=F32)
    ga = _sigmoid(ga_ref[...].astype(F32))
    gb = _sigmoid(gb_ref[...].astype(F32))
    o_ref[...] = (ga * ya + gb * yb).astype(BF16)


def _merge(ry, my, w_ret_o, w_mla_o, proj, gate_col, tm=512, tn=1024):
    T, K = ry.shape
    N = w_ret_o.shape[1]
    nj = N // tn
    g0 = gate_col // tn
    wspec = pl.BlockSpec((K, tn), lambda j, i: (0, j), pipeline_mode=pl.Buffered(1))
    return pl.pallas_call(
        _merge_body,
        grid=(nj, T // tm),
        in_specs=[
            pl.BlockSpec((tm, K), lambda j, i: (i, 0)),
            pl.BlockSpec((tm, K), lambda j, i: (i, 0)),
            wspec, wspec,
            pl.BlockSpec((tm, tn), lambda j, i: (i, g0 + j)),
            pl.BlockSpec((tm, tn), lambda j, i: (i, g0 + nj + j)),
        ],
        out_specs=pl.BlockSpec((tm, tn), lambda j, i: (i, j)),
        out_shape=jax.ShapeDtypeStruct((T, N), BF16),
        scratch_shapes=[pltpu.VMEM((K, tn), BF16), pltpu.VMEM((K, tn), BF16)],
        compiler_params=_params(("parallel", "arbitrary")),
        name="gated_merge",
    )(ry, my, w_ret_o, w_mla_o, proj, proj)


def _outproj_body(a_ref, w_ref, x_ref, o_ref, wb_ref):
    _cast_weight(w_ref, wb_ref)
    o_ref[...] = x_ref[...] + jnp.dot(a_ref[...], wb_ref[...], preferred_element_type=F32)


def _outproj(a, w, x, tm=1024, tn=1024):
    T, K = a.shape
    N = w.shape[1]
    return pl.pallas_call(
        _outproj_body,
        grid=(N // tn, T // tm),
        in_specs=[
            pl.BlockSpec((tm, K), lambda j, i: (i, 0)),
            pl.BlockSpec((K, tn), lambda j, i: (0, j), pipeline_mode=pl.Buffered(1)),
            pl.BlockSpec((tm, tn), lambda j, i: (i, j)),
        ],
        out_specs=pl.BlockSpec((tm, tn), lambda j, i: (i, j)),
        out_shape=jax.ShapeDtypeStruct((T, N), F32),
        scratch_shapes=[pltpu.VMEM((K, tn), BF16)],
        compiler_params=_params(("parallel", "arbitrary")),
        name="out_proj",
    )(a, w, x)


def _mlp_body(h_ref, g_ref, wu_ref, wd_ref, gf_ref, o_ref, hn_ref):
    f = pl.program_id(1)

    @pl.when(f == 0)
    def _init():
        hn_ref[...] = _rms(h_ref[...], g_ref[...]).astype(BF16)
        o_ref[...] = jnp.zeros_like(o_ref)

    z = jnp.dot(hn_ref[...], wu_ref[...], preferred_element_type=F32)
    a = jnp.square(jnp.maximum(z, 0.0)).astype(BF16)
    o_ref[...] += jnp.dot(a, wd_ref[...], preferred_element_type=F32)

    @pl.when(f == pl.num_programs(1) - 1)
    def _finish():
        o_ref[...] = _rms(h_ref[...] + o_ref[...], gf_ref[...])


def _mlp(h, g, w_up, w_down, gf, tm=1024, tf=512):
    T, D = h.shape
    FF = w_up.shape[1]
    return pl.pallas_call(
        _mlp_body,
        grid=(T // tm, FF // tf),
        in_specs=[
            pl.BlockSpec((tm, D), lambda i, f: (i, 0)),
            pl.BlockSpec((1, D), lambda i, f: (0, 0)),
            pl.BlockSpec((D, tf), lambda i, f: (0, f)),
            pl.BlockSpec((tf, D), lambda i, f: (f, 0)),
            pl.BlockSpec((1, D), lambda i, f: (0, 0)),
        ],
        out_specs=pl.BlockSpec((tm, D), lambda i, f: (i, 0)),
        out_shape=jax.ShapeDtypeStruct((T, D), F32),
        scratch_shapes=[pltpu.VMEM((tm, D), BF16)],
        compiler_params=_params(("parallel", "arbitrary"), VMEM_LIMIT_BIG),
        name="mlp_final_norm",
    )(h, g, w_up, w_down, gf)


def _prep_q_weights(w_q_b):
    w = w_q_b.reshape(Q_LORA, MLA_HEADS, MLA_QK_DIM)
    w = jnp.pad(w, ((0, 0), (0, 0), (0, MLA_HEAD_PAD - MLA_QK_DIM)))
    return w.reshape(Q_LORA, MLA_HEADS * MLA_HEAD_PAD).astype(BF16)


def _prep_kv_weights(w_kv_b):
    w = w_kv_b.reshape(KV_LORA, MLA_HEADS, QK_NOPE + V_HEAD)
    wk = w[..., :QK_NOPE].reshape(KV_LORA, MLA_HEADS * QK_NOPE).astype(BF16)
    wv = w[..., QK_NOPE:].reshape(KV_LORA, MLA_HEADS * V_HEAD).astype(BF16)
    return wk, wv


def kernel(x, positions, norm_mix_g, w_in, ret_norm_g, w_ret_o, q_a_norm_g, w_q_b, kv_a_norm_g,
           w_kv_b, w_mla_o, w_out, norm_mlp_g, w_up, w_down, norm_f_g):
    B, S, D = x.shape
    T = B * S
    assert w_in.shape[0] == 1, "single-layer block: the final norm is fused into the MLP kernel"
    log_gamma = jnp.log(1.0 - 2.0 ** (-5.0 - jnp.arange(RET_HEADS, dtype=F32)))
    h = x.reshape(T, D)
    w_in_t = jnp.swapaxes(w_in[0], 0, 1)
    wq = _prep_q_weights(w_q_b[0])
    wk, wv = _prep_kv_weights(w_kv_b[0])
    u, cs, sn, lat = _norm_rope(h, norm_mix_g[0][None], positions.reshape(T, 1), w_in_t)
    proj = _inproj(u, w_in_t)
    q, kn, kpe, v, ry = _branches(proj, lat, u, w_in_t, q_a_norm_g[0][None], kv_a_norm_g[0][None],
                            wq, wk, wv, cs, sn, log_gamma, ret_norm_g[0][None], S)
    my, w_up_b, w_down_b = _attention(q, kn, kpe, v, w_up[0], w_down[0], B, S)
    merged = _merge(ry, my, w_ret_o[0], w_mla_o[0], proj, COL_CQ)
    h1 = _outproj(merged, w_out[0], h)
    out = _mlp(h1, norm_mlp_g[0][None], w_up_b, w_down_b, norm_f_g[None])
    return out.reshape(B, S, D)
```

```python
import functools

import jax
import jax.numpy as jnp
from jax import lax
from jax.experimental import pallas as pl
from jax.experimental.pallas import tpu as pltpu

F32 = jnp.float32
BF16 = jnp.bfloat16

EPS = 1e-6
ROPE_THETA = 10000.0
CHUNK = 64

RET_HEADS = 8
RET_QK_DIM = 128
RET_V_DIM = 256
RET_QK_W = RET_HEADS * RET_QK_DIM
RET_V_W = RET_HEADS * RET_V_DIM

MLA_HEADS = 16
Q_LORA = 512
KV_LORA = 512
QK_NOPE = 128
QK_ROPE = 64
V_HEAD = 128
MLA_QK_DIM = QK_NOPE + QK_ROPE
MLA_HEAD_PAD = 256

COL_RV = 2 * RET_QK_W
COL_RG = COL_RV + RET_V_W
COL_CQ = COL_RG + RET_V_W
COL_CKV = COL_CQ + Q_LORA
COL_KPE = COL_CKV + KV_LORA
COL_GRET = COL_KPE + QK_ROPE

LOG2_E = 1.4426950408889634
LANES = 128
VMEM_LIMIT = 56 * 1024 * 1024
VMEM_LIMIT_BIG = 62 * 1024 * 1024


def _params(sem, vmem_limit=VMEM_LIMIT):
    return pltpu.CompilerParams(dimension_semantics=sem, vmem_limit_bytes=vmem_limit)


def _rms(x, g):
    return x * lax.rsqrt(jnp.mean(x * x, axis=-1, keepdims=True) + EPS) * g


def _rope128(t, c, s):
    return t * c + pltpu.roll(t, LANES // 2, 1) * s


def _rope64(t, c, s_lo, s_hi):
    return t * c + pltpu.roll(t, LANES - QK_ROPE // 2, 1) * s_lo + pltpu.roll(t, QK_ROPE // 2, 1) * s_hi


def _sigmoid(x):
    return 1.0 / (1.0 + jnp.exp(-x))


def _norm_rope_body(x_ref, g_ref, pos_ref, inv_ref, wt_ref, u_ref, cs_ref, sn_ref, lat_ref, wb_ref):
    @pl.when(pl.program_id(0) == 0)
    def _cast():
        wb_ref[...] = wt_ref[...].astype(BF16)

    u = _rms(x_ref[...], g_ref[...]).astype(BF16)
    u_ref[...] = u
    ang = pos_ref[...].astype(F32) * inv_ref[...]
    cs_ref[...] = jnp.cos(ang)
    sn_ref[...] = jnp.sin(ang)
    lat_ref[...] = lax.dot_general(u, wb_ref[...], (((1,), (1,)), ((), ())),
                                   preferred_element_type=F32).astype(BF16)


def _norm_rope(x, g, pos, w_in_t, rows=512):
    T, D = x.shape
    LW = Q_LORA + KV_LORA
    half_r = RET_QK_DIM // 2
    half_m = QK_ROPE // 2
    inv_r = ROPE_THETA ** (-jnp.arange(half_r, dtype=F32) / half_r)
    inv_m = ROPE_THETA ** (-jnp.arange(half_m, dtype=F32) / half_m)
    inv = jnp.concatenate([inv_r, inv_m, inv_m])[None]
    tab = pl.BlockSpec((rows, LANES), lambda i: (i, 0))
    return pl.pallas_call(
        _norm_rope_body,
        grid=(T // rows,),
        in_specs=[pl.BlockSpec((rows, D), lambda i: (i, 0)), pl.BlockSpec((1, D), lambda i: (0, 0)),
                  pl.BlockSpec((rows, 1), lambda i: (i, 0)), pl.BlockSpec((1, LANES), lambda i: (0, 0)),
                  pl.BlockSpec((LW, D), lambda i: (COL_CQ // LW, 0), pipeline_mode=pl.Buffered(1))],
        out_specs=[pl.BlockSpec((rows, D), lambda i: (i, 0)), tab, tab,
                   pl.BlockSpec((rows, LW), lambda i: (i, 0))],
        out_shape=[jax.ShapeDtypeStruct((T, D), BF16), jax.ShapeDtypeStruct((T, LANES), F32),
                   jax.ShapeDtypeStruct((T, LANES), F32), jax.ShapeDtypeStruct((T, LW), BF16)],
        scratch_shapes=[pltpu.VMEM((LW, D), BF16)],
        compiler_params=_params(("arbitrary",)),
        name="norm_rope_latent",
    )(x, g, pos, inv, w_in_t)


def _ret_rope_tables(cs, sn):
    lo = lax.broadcasted_iota(jnp.int32, cs.shape, 1) < LANES // 2
    return (jnp.where(lo, cs, pltpu.roll(cs, LANES // 2, 1)),
            jnp.where(lo, -sn, pltpu.roll(sn, LANES // 2, 1)))


def _mla_rope_tables(cs, sn):
    lane = lax.broadcasted_iota(jnp.int32, cs.shape, 1)
    half = QK_ROPE // 2
    c = pltpu.roll(cs, LANES // 2, 1)
    s = pltpu.roll(sn, LANES // 2, 1)
    return (jnp.where(lane < QK_ROPE, c, 0.0), jnp.where(lane < half, -s, 0.0),
            jnp.where((lane >= half) & (lane < QK_ROPE), s, 0.0))


def _cast_weight(w_ref, wb_ref):
    @pl.when(pl.program_id(1) == 0)
    def _cast():
        wb_ref[...] = w_ref[...].astype(BF16)


def _inproj_body(u_ref, wt_ref, o_ref, wb_ref):
    @pl.when(pl.program_id(1) == 0)
    def _cast():
        wb_ref[...] = wt_ref[...].astype(BF16)

    o_ref[...] = lax.dot_general(u_ref[...], wb_ref[...], (((1,), (1,)), ((), ())),
                                 preferred_element_type=F32).astype(BF16)


def _inproj(u, w_in_t, tm=2048, tn=1024):
    T, D = u.shape
    n_head = COL_CQ // tn
    n_gate = (w_in_t.shape[0] - COL_GRET) // tn

    def w_rows(j, i):
        row = jnp.where(j < n_head, j * tn, COL_GRET + (j - n_head) * tn)
        return (pl.multiple_of(row, QK_ROPE), 0)

    return pl.pallas_call(
        _inproj_body,
        grid=(n_head + n_gate, T // tm),
        in_specs=[
            pl.BlockSpec((tm, D), lambda j, i: (i, 0)),
            pl.BlockSpec((pl.Element(tn), pl.Element(D)), w_rows),
        ],
        out_specs=pl.BlockSpec((tm, tn), lambda j, i: (i, j)),
        out_shape=jax.ShapeDtypeStruct((T, (n_head + n_gate) * tn), BF16),
        scratch_shapes=[pltpu.VMEM((tn, D), BF16)],
        compiler_params=_params(("parallel", "arbitrary")),
        name="inproj",
    )(u, w_in_t)


def _mla_part(cq_ref, ckv_ref, u_ref, wpe_ref, gq_ref, gkv_ref, wq_ref, wk_ref, wv_ref, cs, sn,
              q_ref, kn_ref, kpe_ref, v_ref):
    qa = _rms(cq_ref[...].astype(F32), gq_ref[...]).astype(BF16)
    kva = _rms(ckv_ref[...].astype(F32), gkv_ref[...]).astype(BF16)
    c, s_lo, s_hi = _mla_rope_tables(cs, sn)
    kpe_raw = lax.dot_general(u_ref[...], wpe_ref[...].astype(BF16), (((1,), (1,)), ((), ())),
                              preferred_element_type=F32)
    kpe_ref[...] = _rope64(kpe_raw, c, s_lo, s_hi).astype(BF16)
    scale = MLA_QK_DIM ** -0.5 * LOG2_E
    cq = c * scale
    sq_lo = s_lo * scale
    sq_hi = s_hi * scale
    q = jnp.dot(qa, wq_ref[...], preferred_element_type=F32)
    for h in range(MLA_HEADS):
        b = h * MLA_HEAD_PAD
        q_ref[:, b:b + QK_NOPE] = (q[:, b:b + QK_NOPE] * scale).astype(BF16)
        q_ref[:, b + QK_NOPE:b + MLA_HEAD_PAD] = _rope64(
            q[:, b + QK_NOPE:b + MLA_HEAD_PAD], cq, sq_lo, sq_hi).astype(BF16)
    kn_ref[...] = jnp.dot(kva, wk_ref[...], preferred_element_type=F32).astype(BF16)
    v_ref[...] = jnp.dot(kva, wv_ref[...], preferred_element_type=F32).astype(BF16)


def _ret_tables_init(lg_ref, st_ref, dec_ref, qdec_ref, kdec_ref, bc):
    st_ref[...] = jnp.zeros_like(st_ref)
    n = lax.broadcasted_iota(jnp.int32, (bc, bc), 0)
    m = lax.broadcasted_iota(jnp.int32, (bc, bc), 1)
    dist = jnp.abs(n - m).astype(F32)
    visible = (m // CHUNK) <= (n // CHUNK)
    nl = lax.broadcasted_iota(jnp.int32, (bc, LANES), 0).astype(F32)
    for h in range(RET_HEADS):
        lg = lg_ref[h]
        dec_ref[h] = jnp.where(visible, jnp.exp(lg * dist), 0.0)
        qdec_ref[h] = jnp.exp(lg * (nl + 1.0))
        kdec_ref[h] = jnp.exp(lg * (bc - 1.0 - nl))


def _ret_block(rows, lg_ref, q_ref, k_ref, v_ref, rg_ref, cs, sn, g_ref, o_ref,
               st_ref, dec_ref, qdec_ref, kdec_ref, bc):
    c, s_ = _ret_rope_tables(cs, sn)
    k_scale = RET_QK_DIM ** -0.5
    ck = c * k_scale
    sk = s_ * k_scale
    for h in range(RET_HEADS):
        qs = slice(h * RET_QK_DIM, (h + 1) * RET_QK_DIM)
        vs = slice(h * RET_V_DIM, (h + 1) * RET_V_DIM)
        q32 = _rope128(q_ref[rows, qs].astype(F32), c, s_)
        k32 = _rope128(k_ref[rows, qs].astype(F32), ck, sk)
        q = q32.astype(BF16)
        k = k32.astype(BF16)
        v = v_ref[rows, vs]
        s = lax.dot_general(q, k, (((1,), (1,)), ((), ())), preferred_element_type=F32) * dec_ref[h]
        qd = (q32 * qdec_ref[h]).astype(BF16)
        st = st_ref[h]
        o = (jnp.dot(s.astype(BF16), v, preferred_element_type=F32)
             + jnp.dot(qd, st.astype(BF16), preferred_element_type=F32))
        kd = (k32 * kdec_ref[h]).astype(BF16)
        c_dec = jnp.exp(jnp.full((1, RET_V_DIM), lg_ref[h] * bc, F32))
        st_ref[h] = st * c_dec + lax.dot_general(
            kd, v, (((0,), (0,)), ((), ())), preferred_element_type=F32)
        mu = jnp.mean(o, axis=-1, keepdims=True)
        oc = o - mu
        var = jnp.mean(oc * oc, axis=-1, keepdims=True)
        y = oc * lax.rsqrt(var + EPS) * g_ref[:, vs]
        rg = rg_ref[rows, vs].astype(F32)
        o_ref[rows, vs] = (y * (rg * _sigmoid(rg))).astype(BF16)


def _branches_body(lg_ref, cq_ref, ckv_ref, u_ref, wpe_ref, gq_ref, gkv_ref, wq_ref, wk_ref, wv_ref,
                   cs_ref, sn_ref, rq_ref, rk_ref, rv_ref, rg_ref, gr_ref,
                   q_ref, kn_ref, kpe_ref, v_ref, ry_ref, st_ref, dec_ref, qdec_ref, kdec_ref,
                   *, tm, bc, tiles_per_seq):
    @pl.when(pl.program_id(0) % tiles_per_seq == 0)
    def _init():
        _ret_tables_init(lg_ref, st_ref, dec_ref, qdec_ref, kdec_ref, bc)

    _mla_part(cq_ref, ckv_ref, u_ref, wpe_ref, gq_ref, gkv_ref, wq_ref, wk_ref, wv_ref,
              cs_ref[...], sn_ref[...], q_ref, kn_ref, kpe_ref, v_ref)
    for blk in range(tm // bc):
        rows = slice(blk * bc, (blk + 1) * bc)
        _ret_block(rows, lg_ref, rq_ref, rk_ref, rv_ref, rg_ref, cs_ref[rows, :], sn_ref[rows, :],
                   gr_ref, ry_ref, st_ref, dec_ref, qdec_ref, kdec_ref, bc)


def _branches(proj, lat, u, w_in_t, gq, gkv, wq, wk, wv, cs, sn, lg, g_ret, seq, tm=512, bc=256):
    T, D = u.shape
    const = lambda shape: pl.BlockSpec(shape, lambda i, lg: (0, 0), pipeline_mode=pl.Buffered(1))
    rows = lambda w, col=0: pl.BlockSpec((tm, w), lambda i, lg: (i, col))
    QW = MLA_HEADS * MLA_HEAD_PAD
    KW = MLA_HEADS * QK_NOPE
    VW = MLA_HEADS * V_HEAD
    grid_spec = pltpu.PrefetchScalarGridSpec(
        num_scalar_prefetch=1,
        grid=(T // tm,),
        in_specs=[rows(Q_LORA, 0), rows(KV_LORA, 1), rows(D),
                  pl.BlockSpec((LANES, D), lambda i, lg: (COL_KPE // LANES, 0),
                               pipeline_mode=pl.Buffered(1)),
                  const((1, Q_LORA)), const((1, KV_LORA)),
                  const(wq.shape), const(wk.shape), const(wv.shape),
                  rows(LANES), rows(LANES),
                  rows(RET_QK_W, 0), rows(RET_QK_W, 1),
                  rows(RET_V_W, COL_RV // RET_V_W), rows(RET_V_W, COL_RG // RET_V_W),
                  const((1, RET_V_W))],
        out_specs=[rows(QW), rows(KW), rows(LANES), rows(VW), rows(RET_V_W)],
        scratch_shapes=[pltpu.VMEM((RET_HEADS, RET_QK_DIM, RET_V_DIM), F32),
                        pltpu.VMEM((RET_HEADS, bc, bc), F32),
                        pltpu.VMEM((RET_HEADS, bc, LANES), F32),
                        pltpu.VMEM((RET_HEADS, bc, LANES), F32)],
    )
    return pl.pallas_call(
        functools.partial(_branches_body, tm=tm, bc=bc, tiles_per_seq=seq // tm),
        grid_spec=grid_spec,
        out_shape=[jax.ShapeDtypeStruct((T, QW), BF16), jax.ShapeDtypeStruct((T, KW), BF16),
                   jax.ShapeDtypeStruct((T, LANES), BF16), jax.ShapeDtypeStruct((T, VW), BF16),
                   jax.ShapeDtypeStruct((T, RET_V_W), BF16)],
        compiler_params=_params(("arbitrary",), VMEM_LIMIT_BIG),
        name="mla_proj_retention",
    )(lg, lat, lat, u, w_in_t, gq, gkv, wq, wk, wv, cs, sn, proj, proj, proj, proj, g_ret)


def _attn_body(q_ref, kn_ref, kpe_ref, v_ref, wr_ref, wm_ref, o_ref, wrb_ref, wmb_ref, m_ref, acc_ref,
               *, tq, tk, nh):
    wrb_ref[...] = wr_ref[...].astype(BF16)
    wmb_ref[...] = wm_ref[...].astype(BF16)
    qi = pl.program_id(2)
    m_ref[...] = jnp.full_like(m_ref, -jnp.inf)
    acc_ref[...] = jnp.zeros_like(acc_ref)
    reps = tk // LANES

    def step(start, diag_off):
        r0 = 0 if diag_off is None else diag_off
        rows = tq - r0
        for hh in range(nh):
            q = q_ref[r0:, hh * MLA_HEAD_PAD:(hh + 1) * MLA_HEAD_PAD]
            k = jnp.concatenate([kn_ref[pl.ds(start, tk), hh * QK_NOPE:(hh + 1) * QK_NOPE],
                                 kpe_ref[pl.ds(start, tk), :]], axis=1)
            v = v_ref[pl.ds(start, tk), hh * V_HEAD:(hh + 1) * V_HEAD]
            s = lax.dot_general(q, k, (((1,), (1,)), ((), ())), preferred_element_type=F32)
            if diag_off is not None:
                rq = lax.broadcasted_iota(jnp.int32, (rows, tk), 0) // CHUNK
                ck = lax.broadcasted_iota(jnp.int32, (rows, tk), 1) // CHUNK
                s = jnp.where(ck <= rq, s, -jnp.inf)
            m_old = m_ref[hh, r0:, :]
            m_new = jnp.maximum(m_old, jnp.max(s, axis=-1, keepdims=True))
            alpha = jnp.exp2(m_old - m_new)
            p = jnp.exp2(s - jnp.concatenate([m_new] * reps, axis=1))
            pv = jnp.dot(p.astype(BF16), jnp.concatenate([v, jnp.ones_like(v)], axis=1),
                         preferred_element_type=F32)
            acc_ref[hh, r0:, :] = jnp.concatenate([alpha, alpha], axis=1) * acc_ref[hh, r0:, :] + pv
            m_ref[hh, r0:, :] = m_new

    def full_steps(kb, carry):
        for j in range(tq // tk):
            step(pl.multiple_of(kb * tq + j * tk, tk), None)
        return carry

    lax.fori_loop(0, qi, full_steps, 0)
    for j in range(tq // tk):
        step(pl.multiple_of(qi * tq + j * tk, tk), j * tk)
    for hh in range(nh):
        o_ref[:, hh * V_HEAD:(hh + 1) * V_HEAD] = (
            acc_ref[hh, :, :V_HEAD] / acc_ref[hh, :, V_HEAD:]).astype(BF16)


def _attention(q, kn, kpe, v, w_ret_o, w_mla_o, batch, seq, tq=2048, tk=512, nh=2):
    nq = seq // tq
    nhp = MLA_HEADS // nh
    steps = batch * nhp * nq
    K, N = w_ret_o.shape
    rs = K // steps
    assert K % steps == 0 and rs % 16 == 0 and w_mla_o.shape == (K, N)
    wspec = pl.BlockSpec((rs, N), lambda b, h, i: ((b * nhp + h) * nq + i, 0))
    return pl.pallas_call(
        functools.partial(_attn_body, tq=tq, tk=tk, nh=nh),
        grid=(batch, nhp, nq),
        in_specs=[
            pl.BlockSpec((tq, nh * MLA_HEAD_PAD), lambda b, h, i: (b * nq + i, h)),
            pl.BlockSpec((seq, nh * QK_NOPE), lambda b, h, i: (b, h)),
            pl.BlockSpec((seq, LANES), lambda b, h, i: (b, 0)),
            pl.BlockSpec((seq, nh * V_HEAD), lambda b, h, i: (b, h)),
            wspec, wspec,
        ],
        out_specs=[pl.BlockSpec((tq, nh * V_HEAD), lambda b, h, i: (b * nq + i, h)), wspec, wspec],
        out_shape=[jax.ShapeDtypeStruct((batch * seq, MLA_HEADS * V_HEAD), BF16),
                   jax.ShapeDtypeStruct((K, N), BF16), jax.ShapeDtypeStruct((K, N), BF16)],
        scratch_shapes=[pltpu.VMEM((nh, tq, LANES), F32), pltpu.VMEM((nh, tq, 2 * V_HEAD), F32)],
        compiler_params=_params(("parallel", "parallel", "arbitrary")),
        name="mla_attention",
    )(q, kn, kpe, v, w_ret_o, w_mla_o)


def _merge_body(a_ref, b_ref, wa_ref, wb_ref, ga_ref, gb_ref, o_ref):
    ya = jnp.dot(a_ref[...], wa_ref[...], preferred_element_type=F32)
    yb = jnp.dot(b_ref[...], wb_ref[...], preferred_element_type=F32)
    ga = _sigmoid(ga_ref[...].astype(F32))
    gb = _sigmoid(gb_ref[...].astype(F32))
    o_ref[...] = (ga * ya + gb * yb).astype(BF16)


def _merge(ry, my, w_ret_o, w_mla_o, proj, gate_col, tm=512):
    T, K = ry.shape
    N = w_ret_o.shape[1]
    g0 = gate_col // N
    wspec = pl.BlockSpec((K, N), lambda i: (0, 0), pipeline_mode=pl.Buffered(1))
    return pl.pallas_call(
        _merge_body,
        grid=(T // tm,),
        in_specs=[
            pl.BlockSpec((tm, K), lambda i: (i, 0)),
            pl.BlockSpec((tm, K), lambda i: (i, 0)),
            wspec, wspec,
            pl.BlockSpec((tm, N), lambda i: (i, g0)),
            pl.BlockSpec((tm, N), lambda i: (i, g0 + 1)),
        ],
        out_specs=pl.BlockSpec((tm, N), lambda i: (i, 0)),
        out_shape=jax.ShapeDtypeStruct((T, N), BF16),
        compiler_params=_params(("parallel",), VMEM_LIMIT_BIG),
        name="gated_merge",
    )(ry, my, w_ret_o, w_mla_o, proj, proj)


def _outproj_body(a_ref, w_ref, x_ref, o_ref, wb_ref):
    _cast_weight(w_ref, wb_ref)
    o_ref[...] = x_ref[...] + jnp.dot(a_ref[...], wb_ref[...], preferred_element_type=F32)


def _outproj(a, w, x, tm=1024, tn=1024):
    T, K = a.shape
    N = w.shape[1]
    return pl.pallas_call(
        _outproj_body,
        grid=(N // tn, T // tm),
        in_specs=[
            pl.BlockSpec((tm, K), lambda j, i: (i, 0)),
            pl.BlockSpec((K, tn), lambda j, i: (0, j), pipeline_mode=pl.Buffered(1)),
            pl.BlockSpec((tm, tn), lambda j, i: (i, j)),
        ],
        out_specs=pl.BlockSpec((tm, tn), lambda j, i: (i, j)),
        out_shape=jax.ShapeDtypeStruct((T, N), F32),
        scratch_shapes=[pltpu.VMEM((K, tn), BF16)],
        compiler_params=_params(("parallel", "arbitrary")),
        name="out_proj",
    )(a, w, x)


def _mlp_body(h_ref, g_ref, wu_ref, wd_ref, gf_ref, o_ref, hn_ref):
    f = pl.program_id(1)

    @pl.when(f == 0)
    def _init():
        hn_ref[...] = _rms(h_ref[...], g_ref[...]).astype(BF16)
        o_ref[...] = jnp.zeros_like(o_ref)

    z = jnp.dot(hn_ref[...], wu_ref[...].astype(BF16), preferred_element_type=F32)
    a = jnp.square(jnp.maximum(z, 0.0)).astype(BF16)
    o_ref[...] += jnp.dot(a, wd_ref[...].astype(BF16), preferred_element_type=F32)

    @pl.when(f == pl.num_programs(1) - 1)
    def _finish():
        o_ref[...] = _rms(h_ref[...] + o_ref[...], gf_ref[...])


def _mlp(h, g, w_up, w_down, gf, tm=1024, tf=512):
    T, D = h.shape
    FF = w_up.shape[1]
    return pl.pallas_call(
        _mlp_body,
        grid=(T // tm, FF // tf),
        in_specs=[
            pl.BlockSpec((tm, D), lambda i, f: (i, 0)),
            pl.BlockSpec((1, D), lambda i, f: (0, 0)),
            pl.BlockSpec((D, tf), lambda i, f: (0, f)),
            pl.BlockSpec((tf, D), lambda i, f: (f, 0)),
            pl.BlockSpec((1, D), lambda i, f: (0, 0)),
        ],
        out_specs=pl.BlockSpec((tm, D), lambda i, f: (i, 0)),
        out_shape=jax.ShapeDtypeStruct((T, D), F32),
        scratch_shapes=[pltpu.VMEM((tm, D), BF16)],
        compiler_params=_params(("parallel", "arbitrary"), VMEM_LIMIT_BIG),
        name="mlp_final_norm",
    )(h, g, w_up, w_down, gf)


def _prep_q_weights(w_q_b):
    w = w_q_b.reshape(Q_LORA, MLA_HEADS, MLA_QK_DIM)
    w = jnp.pad(w, ((0, 0), (0, 0), (0, MLA_HEAD_PAD - MLA_QK_DIM)))
    return w.reshape(Q_LORA, MLA_HEADS * MLA_HEAD_PAD).astype(BF16)


def _prep_kv_weights(w_kv_b):
    w = w_kv_b.reshape(KV_LORA, MLA_HEADS, QK_NOPE + V_HEAD)
    wk = w[..., :QK_NOPE].reshape(KV_LORA, MLA_HEADS * QK_NOPE).astype(BF16)
    wv = w[..., QK_NOPE:].reshape(KV_LORA, MLA_HEADS * V_HEAD).astype(BF16)
    return wk, wv


def kernel(x, positions, norm_mix_g, w_in, ret_norm_g, w_ret_o, q_a_norm_g, w_q_b, kv_a_norm_g,
           w_kv_b, w_mla_o, w_out, norm_mlp_g, w_up, w_down, norm_f_g):
    B, S, D = x.shape
    T = B * S
    assert w_in.shape[0] == 1, "single-layer block: the final norm is fused into the MLP kernel"
    log_gamma = jnp.log(1.0 - 2.0 ** (-5.0 - jnp.arange(RET_HEADS, dtype=F32)))
    h = x.reshape(T, D)
    w_in_t = jnp.swapaxes(w_in[0], 0, 1)
    wq = _prep_q_weights(w_q_b[0])
    wk, wv = _prep_kv_weights(w_kv_b[0])
    u, cs, sn, lat = _norm_rope(h, norm_mix_g[0][None], positions.reshape(T, 1), w_in_t)
    proj = _inproj(u, w_in_t)
    q, kn, kpe, v, ry = _branches(proj, lat, u, w_in_t, q_a_norm_g[0][None], kv_a_norm_g[0][None],
                            wq, wk, wv, cs, sn, log_gamma, ret_norm_g[0][None], S)
    my, w_ret_b, w_mla_b = _attention(q, kn, kpe, v, w_ret_o[0], w_mla_o[0], B, S)
    merged = _merge(ry, my, w_ret_b, w_mla_b, proj, COL_CQ)
    h1 = _outproj(merged, w_out[0], h)
    out = _mlp(h1, norm_mlp_g[0][None], w_up[0], w_down[0], norm_f_g[None])
    return out.reshape(B, S, D)
```

```python
import functools

import jax
import jax.numpy as jnp
from jax import lax
from jax.experimental import pallas as pl
from jax.experimental.pallas import tpu as pltpu

F32 = jnp.float32
BF16 = jnp.bfloat16

EPS = 1e-6
ROPE_THETA = 10000.0
CHUNK = 64

RET_HEADS = 8
RET_QK_DIM = 128
RET_V_DIM = 256
RET_QK_W = RET_HEADS * RET_QK_DIM
RET_V_W = RET_HEADS * RET_V_DIM

MLA_HEADS = 16
Q_LORA = 512
KV_LORA = 512
QK_NOPE = 128
QK_ROPE = 64
V_HEAD = 128
MLA_QK_DIM = QK_NOPE + QK_ROPE
MLA_HEAD_PAD = 256

COL_RV = 2 * RET_QK_W
COL_RG = COL_RV + RET_V_W
COL_CQ = COL_RG + RET_V_W
COL_CKV = COL_CQ + Q_LORA
COL_KPE = COL_CKV + KV_LORA
COL_GRET = COL_KPE + QK_ROPE

LOG2_E = 1.4426950408889634
LANES = 128
VMEM_LIMIT = 56 * 1024 * 1024
VMEM_LIMIT_BIG = 62 * 1024 * 1024


def _params(sem, vmem_limit=VMEM_LIMIT):
    return pltpu.CompilerParams(dimension_semantics=sem, vmem_limit_bytes=vmem_limit)


def _rms(x, g):
    return x * lax.rsqrt(jnp.mean(x * x, axis=-1, keepdims=True) + EPS) * g


def _rope128(t, c, s):
    return t * c + pltpu.roll(t, LANES // 2, 1) * s


def _rope64(t, c, s_lo, s_hi):
    return t * c + pltpu.roll(t, LANES - QK_ROPE // 2, 1) * s_lo + pltpu.roll(t, QK_ROPE // 2, 1) * s_hi


def _sigmoid(x):
    return 1.0 / (1.0 + jnp.exp(-x))


def _norm_rope_body(x_ref, g_ref, pos_ref, inv_ref, wt_ref, u_ref, cs_ref, sn_ref, lat_ref, wb_ref):
    @pl.when(pl.program_id(0) == 0)
    def _cast():
        wb_ref[...] = wt_ref[...].astype(BF16)

    u = _rms(x_ref[...], g_ref[...]).astype(BF16)
    u_ref[...] = u
    ang = pos_ref[...].astype(F32) * inv_ref[...]
    cs_ref[...] = jnp.cos(ang)
    sn_ref[...] = jnp.sin(ang)
    lat_ref[...] = lax.dot_general(u, wb_ref[...], (((1,), (1,)), ((), ())),
                                   preferred_element_type=F32).astype(BF16)


def _norm_rope(x, g, pos, w_in_t, rows=512):
    T, D = x.shape
    LW = Q_LORA + KV_LORA
    half_r = RET_QK_DIM // 2
    half_m = QK_ROPE // 2
    inv_r = ROPE_THETA ** (-jnp.arange(half_r, dtype=F32) / half_r)
    inv_m = ROPE_THETA ** (-jnp.arange(half_m, dtype=F32) / half_m)
    inv = jnp.concatenate([inv_r, inv_m, inv_m])[None]
    tab = pl.BlockSpec((rows, LANES), lambda i: (i, 0))
    return pl.pallas_call(
        _norm_rope_body,
        grid=(T // rows,),
        in_specs=[pl.BlockSpec((rows, D), lambda i: (i, 0)), pl.BlockSpec((1, D), lambda i: (0, 0)),
                  pl.BlockSpec((rows, 1), lambda i: (i, 0)), pl.BlockSpec((1, LANES), lambda i: (0, 0)),
                  pl.BlockSpec((LW, D), lambda i: (COL_CQ // LW, 0), pipeline_mode=pl.Buffered(1))],
        out_specs=[pl.BlockSpec((rows, D), lambda i: (i, 0)), tab, tab,
                   pl.BlockSpec((rows, LW), lambda i: (i, 0))],
        out_shape=[jax.ShapeDtypeStruct((T, D), BF16), jax.ShapeDtypeStruct((T, LANES), F32),
                   jax.ShapeDtypeStruct((T, LANES), F32), jax.ShapeDtypeStruct((T, LW), BF16)],
        scratch_shapes=[pltpu.VMEM((LW, D), BF16)],
        compiler_params=_params(("arbitrary",)),
        name="norm_rope_latent",
    )(x, g, pos, inv, w_in_t)


def _ret_rope_tables(cs, sn):
    lo = lax.broadcasted_iota(jnp.int32, cs.shape, 1) < LANES // 2
    return (jnp.where(lo, cs, pltpu.roll(cs, LANES // 2, 1)),
            jnp.where(lo, -sn, pltpu.roll(sn, LANES // 2, 1)))


def _mla_rope_tables(cs, sn):
    lane = lax.broadcasted_iota(jnp.int32, cs.shape, 1)
    half = QK_ROPE // 2
    c = pltpu.roll(cs, LANES // 2, 1)
    s = pltpu.roll(sn, LANES // 2, 1)
    return (jnp.where(lane < QK_ROPE, c, 0.0), jnp.where(lane < half, -s, 0.0),
            jnp.where((lane >= half) & (lane < QK_ROPE), s, 0.0))


def _inproj_body(u_ref, wt_ref, o_ref, wb_ref):
    @pl.when(pl.program_id(1) == 0)
    def _cast():
        wb_ref[...] = wt_ref[...].astype(BF16)

    o_ref[...] = lax.dot_general(u_ref[...], wb_ref[...], (((1,), (1,)), ((), ())),
                                 preferred_element_type=F32).astype(BF16)


def _inproj(u, w_in_t, tm=2048, tn=1024):
    T, D = u.shape
    n_head = COL_CQ // tn
    n_gate = (w_in_t.shape[0] - COL_GRET) // tn

    def w_rows(j, i):
        row = jnp.where(j < n_head, j * tn, COL_GRET + (j - n_head) * tn)
        return (pl.multiple_of(row, QK_ROPE), 0)

    return pl.pallas_call(
        _inproj_body,
        grid=(n_head + n_gate, T // tm),
        in_specs=[
            pl.BlockSpec((tm, D), lambda j, i: (i, 0)),
            pl.BlockSpec((pl.Element(tn), pl.Element(D)), w_rows),
        ],
        out_specs=pl.BlockSpec((tm, tn), lambda j, i: (i, j)),
        out_shape=jax.ShapeDtypeStruct((T, (n_head + n_gate) * tn), BF16),
        scratch_shapes=[pltpu.VMEM((tn, D), BF16)],
        compiler_params=_params(("parallel", "arbitrary")),
        name="inproj",
    )(u, w_in_t)


def _mla_part(cq_ref, ckv_ref, u_ref, wpe_ref, gq_ref, gkv_ref, wq_ref, wk_ref, wv_ref, cs, sn,
              q_ref, kn_ref, kpe_ref, v_ref):
    qa = _rms(cq_ref[...].astype(F32), gq_ref[...]).astype(BF16)
    kva = _rms(ckv_ref[...].astype(F32), gkv_ref[...]).astype(BF16)
    c, s_lo, s_hi = _mla_rope_tables(cs, sn)
    kpe_raw = lax.dot_general(u_ref[...], wpe_ref[...].astype(BF16), (((1,), (1,)), ((), ())),
                              preferred_element_type=F32)
    kpe_ref[...] = _rope64(kpe_raw, c, s_lo, s_hi).astype(BF16)
    scale = MLA_QK_DIM ** -0.5 * LOG2_E
    cq = c * scale
    sq_lo = s_lo * scale
    sq_hi = s_hi * scale
    q = jnp.dot(qa, wq_ref[...], preferred_element_type=F32)
    for h in range(MLA_HEADS):
        b = h * MLA_HEAD_PAD
        q_ref[:, b:b + QK_NOPE] = (q[:, b:b + QK_NOPE] * scale).astype(BF16)
        q_ref[:, b + QK_NOPE:b + MLA_HEAD_PAD] = _rope64(
            q[:, b + QK_NOPE:b + MLA_HEAD_PAD], cq, sq_lo, sq_hi).astype(BF16)
    kn_ref[...] = jnp.dot(kva, wk_ref[...], preferred_element_type=F32).astype(BF16)
    v_ref[...] = jnp.dot(kva, wv_ref[...], preferred_element_type=F32).astype(BF16)


def _ret_tables_init(lg_ref, st_ref, dec_ref, qdec_ref, kdec_ref, bc):
    st_ref[...] = jnp.zeros_like(st_ref)
    n = lax.broadcasted_iota(jnp.int32, (bc, bc), 0)
    m = lax.broadcasted_iota(jnp.int32, (bc, bc), 1)
    dist = jnp.abs(n - m).astype(F32)
    visible = (m // CHUNK) <= (n // CHUNK)
    nl = lax.broadcasted_iota(jnp.int32, (bc, LANES), 0).astype(F32)
    for h in range(RET_HEADS):
        lg = lg_ref[h]
        dec_ref[h] = jnp.where(visible, jnp.exp(lg * dist), 0.0)
        qdec_ref[h] = jnp.exp(lg * (nl + 1.0))
        kdec_ref[h] = jnp.exp(lg * (bc - 1.0 - nl))


def _ret_block(rows, lg_ref, q_ref, k_ref, v_ref, rg_ref, cs, sn, g_ref, o_ref,
               st_ref, dec_ref, qdec_ref, kdec_ref, bc):
    c, s_ = _ret_rope_tables(cs, sn)
    k_scale = RET_QK_DIM ** -0.5
    ck = c * k_scale
    sk = s_ * k_scale
    for h in range(RET_HEADS):
        qs = slice(h * RET_QK_DIM, (h + 1) * RET_QK_DIM)
        vs = slice(h * RET_V_DIM, (h + 1) * RET_V_DIM)
        q32 = _rope128(q_ref[rows, qs].astype(F32), c, s_)
        k32 = _rope128(k_ref[rows, qs].astype(F32), ck, sk)
        q = q32.astype(BF16)
        k = k32.astype(BF16)
        v = v_ref[rows, vs]
        s = lax.dot_general(q, k, (((1,), (1,)), ((), ())), preferred_element_type=F32) * dec_ref[h]
        qd = (q32 * qdec_ref[h]).astype(BF16)
        st = st_ref[h]
        o = (jnp.dot(s.astype(BF16), v, preferred_element_type=F32)
             + jnp.dot(qd, st.astype(BF16), preferred_element_type=F32))
        kd = (k32 * kdec_ref[h]).astype(BF16)
        c_dec = jnp.exp(jnp.full((1, RET_V_DIM), lg_ref[h] * bc, F32))
        st_ref[h] = st * c_dec + lax.dot_general(
            kd, v, (((0,), (0,)), ((), ())), preferred_element_type=F32)
        mu = jnp.mean(o, axis=-1, keepdims=True)
        oc = o - mu
        var = jnp.mean(oc * oc, axis=-1, keepdims=True)
        y = oc * lax.rsqrt(var + EPS) * g_ref[:, vs]
        rg = rg_ref[rows, vs].astype(F32)
        o_ref[rows, vs] = (y * (rg * _sigmoid(rg))).astype(BF16)


def _branches_body(lg_ref, cq_ref, ckv_ref, u_ref, wpe_ref, gq_ref, gkv_ref, wq_ref, wk_ref, wv_ref,
                   cs_ref, sn_ref, rq_ref, rk_ref, rv_ref, rg_ref, gr_ref,
                   q_ref, kn_ref, kpe_ref, v_ref, ry_ref, st_ref, dec_ref, qdec_ref, kdec_ref,
                   *, tm, bc, tiles_per_seq):
    @pl.when(pl.program_id(0) % tiles_per_seq == 0)
    def _init():
        _ret_tables_init(lg_ref, st_ref, dec_ref, qdec_ref, kdec_ref, bc)

    _mla_part(cq_ref, ckv_ref, u_ref, wpe_ref, gq_ref, gkv_ref, wq_ref, wk_ref, wv_ref,
              cs_ref[...], sn_ref[...], q_ref, kn_ref, kpe_ref, v_ref)
    for blk in range(tm // bc):
        rows = slice(blk * bc, (blk + 1) * bc)
        _ret_block(rows, lg_ref, rq_ref, rk_ref, rv_ref, rg_ref, cs_ref[rows, :], sn_ref[rows, :],
                   gr_ref, ry_ref, st_ref, dec_ref, qdec_ref, kdec_ref, bc)


def _branches(proj, lat, u, w_in_t, gq, gkv, wq, wk, wv, cs, sn, lg, g_ret, seq, tm=512, bc=256):
    T, D = u.shape
    const = lambda shape: pl.BlockSpec(shape, lambda i, lg: (0, 0), pipeline_mode=pl.Buffered(1))
    rows = lambda w, col=0: pl.BlockSpec((tm, w), lambda i, lg: (i, col))
    QW = MLA_HEADS * MLA_HEAD_PAD
    KW = MLA_HEADS * QK_NOPE
    VW = MLA_HEADS * V_HEAD
    grid_spec = pltpu.PrefetchScalarGridSpec(
        num_scalar_prefetch=1,
        grid=(T // tm,),
        in_specs=[rows(Q_LORA, 0), rows(KV_LORA, 1), rows(D),
                  pl.BlockSpec((LANES, D), lambda i, lg: (COL_KPE // LANES, 0),
                               pipeline_mode=pl.Buffered(1)),
                  const((1, Q_LORA)), const((1, KV_LORA)),
                  const(wq.shape), const(wk.shape), const(wv.shape),
                  rows(LANES), rows(LANES),
                  rows(RET_QK_W, 0), rows(RET_QK_W, 1),
                  rows(RET_V_W, COL_RV // RET_V_W), rows(RET_V_W, COL_RG // RET_V_W),
                  const((1, RET_V_W))],
        out_specs=[rows(QW), rows(KW), rows(LANES), rows(VW), rows(RET_V_W)],
        scratch_shapes=[pltpu.VMEM((RET_HEADS, RET_QK_DIM, RET_V_DIM), F32),
                        pltpu.VMEM((RET_HEADS, bc, bc), F32),
                        pltpu.VMEM((RET_HEADS, bc, LANES), F32),
                        pltpu.VMEM((RET_HEADS, bc, LANES), F32)],
    )
    return pl.pallas_call(
        functools.partial(_branches_body, tm=tm, bc=bc, tiles_per_seq=seq // tm),
        grid_spec=grid_spec,
        out_shape=[jax.ShapeDtypeStruct((T, QW), BF16), jax.ShapeDtypeStruct((T, KW), BF16),
                   jax.ShapeDtypeStruct((T, LANES), BF16), jax.ShapeDtypeStruct((T, VW), BF16),
                   jax.ShapeDtypeStruct((T, RET_V_W), BF16)],
        compiler_params=_params(("arbitrary",), VMEM_LIMIT_BIG),
        name="mla_proj_retention",
    )(lg, lat, lat, u, w_in_t, gq, gkv, wq, wk, wv, cs, sn, proj, proj, proj, proj, g_ret)


def _attn_body(q_ref, kn_ref, kpe_ref, v_ref, wr_ref, wm_ref, wo_ref, o_ref, wrb_ref, wmb_ref, wob_ref,
               m_ref, acc_ref, *, tq, tk, nh):
    wrb_ref[...] = wr_ref[...].astype(BF16)
    wmb_ref[...] = wm_ref[...].astype(BF16)
    wob_ref[...] = wo_ref[...].astype(BF16)
    qi = pl.program_id(2)
    m_ref[...] = jnp.full_like(m_ref, -jnp.inf)
    acc_ref[...] = jnp.zeros_like(acc_ref)
    reps = tk // LANES

    def step(start, diag_off):
        r0 = 0 if diag_off is None else diag_off
        rows = tq - r0
        for hh in range(nh):
            q = q_ref[r0:, hh * MLA_HEAD_PAD:(hh + 1) * MLA_HEAD_PAD]
            k = jnp.concatenate([kn_ref[pl.ds(start, tk), hh * QK_NOPE:(hh + 1) * QK_NOPE],
                                 kpe_ref[pl.ds(start, tk), :]], axis=1)
            v = v_ref[pl.ds(start, tk), hh * V_HEAD:(hh + 1) * V_HEAD]
            s = lax.dot_general(q, k, (((1,), (1,)), ((), ())), preferred_element_type=F32)
            if diag_off is not None:
                rq = lax.broadcasted_iota(jnp.int32, (rows, tk), 0) // CHUNK
                ck = lax.broadcasted_iota(jnp.int32, (rows, tk), 1) // CHUNK
                s = jnp.where(ck <= rq, s, -jnp.inf)
            m_old = m_ref[hh, r0:, :]
            m_new = jnp.maximum(m_old, jnp.max(s, axis=-1, keepdims=True))
            alpha = jnp.exp2(m_old - m_new)
            p = jnp.exp2(s - jnp.concatenate([m_new] * reps, axis=1))
            pv = jnp.dot(p.astype(BF16), jnp.concatenate([v, jnp.ones_like(v)], axis=1),
                         preferred_element_type=F32)
            acc_ref[hh, r0:, :] = jnp.concatenate([alpha, alpha], axis=1) * acc_ref[hh, r0:, :] + pv
            m_ref[hh, r0:, :] = m_new

    def full_steps(kb, carry):
        for j in range(tq // tk):
            step(pl.multiple_of(kb * tq + j * tk, tk), None)
        return carry

    lax.fori_loop(0, qi, full_steps, 0)
    for j in range(tq // tk):
        step(pl.multiple_of(qi * tq + j * tk, tk), j * tk)
    for hh in range(nh):
        o_ref[:, hh * V_HEAD:(hh + 1) * V_HEAD] = (
            acc_ref[hh, :, :V_HEAD] / acc_ref[hh, :, V_HEAD:]).astype(BF16)


def _attention(q, kn, kpe, v, w_ret_o, w_mla_o, w_out, batch, seq, tq=2048, tk=512, nh=2):
    nq = seq // tq
    nhp = MLA_HEADS // nh
    steps = batch * nhp * nq
    K, N = w_ret_o.shape
    rs = K // steps
    assert K % steps == 0 and rs % 16 == 0 and w_mla_o.shape == (K, N) and w_out.shape == (K, N)
    wspec = pl.BlockSpec((rs, N), lambda b, h, i: ((b * nhp + h) * nq + i, 0))
    return pl.pallas_call(
        functools.partial(_attn_body, tq=tq, tk=tk, nh=nh),
        grid=(batch, nhp, nq),
        in_specs=[
            pl.BlockSpec((tq, nh * MLA_HEAD_PAD), lambda b, h, i: (b * nq + i, h)),
            pl.BlockSpec((seq, nh * QK_NOPE), lambda b, h, i: (b, h)),
            pl.BlockSpec((seq, LANES), lambda b, h, i: (b, 0)),
            pl.BlockSpec((seq, nh * V_HEAD), lambda b, h, i: (b, h)),
            wspec, wspec, wspec,
        ],
        out_specs=[pl.BlockSpec((tq, nh * V_HEAD), lambda b, h, i: (b * nq + i, h)), wspec, wspec, wspec],
        out_shape=[jax.ShapeDtypeStruct((batch * seq, MLA_HEADS * V_HEAD), BF16)]
        + [jax.ShapeDtypeStruct((K, N), BF16)] * 3,
        scratch_shapes=[pltpu.VMEM((nh, tq, LANES), F32), pltpu.VMEM((nh, tq, 2 * V_HEAD), F32)],
        compiler_params=_params(("parallel", "parallel", "arbitrary")),
        name="mla_attention",
    )(q, kn, kpe, v, w_ret_o, w_mla_o, w_out)


def _merge_body(a_ref, b_ref, wa_ref, wb_ref, ga_ref, gb_ref, o_ref):
    ya = jnp.dot(a_ref[...], wa_ref[...], preferred_element_type=F32)
    yb = jnp.dot(b_ref[...], wb_ref[...], preferred_element_type=F32)
    ga = _sigmoid(ga_ref[...].astype(F32))
    gb = _sigmoid(gb_ref[...].astype(F32))
    o_ref[...] = (ga * ya + gb * yb).astype(BF16)


def _merge(ry, my, w_ret_o, w_mla_o, proj, gate_col, tm=512):
    T, K = ry.shape
    N = w_ret_o.shape[1]
    g0 = gate_col // N
    wspec = pl.BlockSpec((K, N), lambda i: (0, 0), pipeline_mode=pl.Buffered(1))
    return pl.pallas_call(
        _merge_body,
        grid=(T // tm,),
        in_specs=[
            pl.BlockSpec((tm, K), lambda i: (i, 0)),
            pl.BlockSpec((tm, K), lambda i: (i, 0)),
            wspec, wspec,
            pl.BlockSpec((tm, N), lambda i: (i, g0)),
            pl.BlockSpec((tm, N), lambda i: (i, g0 + 1)),
        ],
        out_specs=pl.BlockSpec((tm, N), lambda i: (i, 0)),
        out_shape=jax.ShapeDtypeStruct((T, N), BF16),
        compiler_params=_params(("parallel",), VMEM_LIMIT_BIG),
        name="gated_merge",
    )(ry, my, w_ret_o, w_mla_o, proj, proj)


def _outproj_body(a_ref, w_ref, x_ref, o_ref):
    o_ref[...] = x_ref[...] + jnp.dot(a_ref[...], w_ref[...], preferred_element_type=F32)


def _outproj(a, w, x, tm=512):
    T, K = a.shape
    N = w.shape[1]
    return pl.pallas_call(
        _outproj_body,
        grid=(T // tm,),
        in_specs=[
            pl.BlockSpec((tm, K), lambda i: (i, 0)),
            pl.BlockSpec((K, N), lambda i: (0, 0), pipeline_mode=pl.Buffered(1)),
            pl.BlockSpec((tm, N), lambda i: (i, 0)),
        ],
        out_specs=pl.BlockSpec((tm, N), lambda i: (i, 0)),
        out_shape=jax.ShapeDtypeStruct((T, N), F32),
        compiler_params=_params(("parallel",)),
        name="out_proj",
    )(a, w, x)


def _mlp_body(h_ref, g_ref, wu_ref, wd_ref, gf_ref, o_ref, hn_ref):
    f = pl.program_id(1)

    @pl.when(f == 0)
    def _init():
        hn_ref[...] = _rms(h_ref[...], g_ref[...]).astype(BF16)
        o_ref[...] = jnp.zeros_like(o_ref)

    z = jnp.dot(hn_ref[...], wu_ref[...].astype(BF16), preferred_element_type=F32)
    a = jnp.square(jnp.maximum(z, 0.0)).astype(BF16)
    o_ref[...] += jnp.dot(a, wd_ref[...].astype(BF16), preferred_element_type=F32)

    @pl.when(f == pl.num_programs(1) - 1)
    def _finish():
        o_ref[...] = _rms(h_ref[...] + o_ref[...], gf_ref[...])


def _mlp(h, g, w_up, w_down, gf, tm=1024, tf=512):
    T, D = h.shape
    FF = w_up.shape[1]
    return pl.pallas_call(
        _mlp_body,
        grid=(T // tm, FF // tf),
        in_specs=[
            pl.BlockSpec((tm, D), lambda i, f: (i, 0)),
            pl.BlockSpec((1, D), lambda i, f: (0, 0)),
            pl.BlockSpec((D, tf), lambda i, f: (0, f)),
            pl.BlockSpec((tf, D), lambda i, f: (f, 0)),
            pl.BlockSpec((1, D), lambda i, f: (0, 0)),
        ],
        out_specs=pl.BlockSpec((tm, D), lambda i, f: (i, 0)),
        out_shape=jax.ShapeDtypeStruct((T, D), F32),
        scratch_shapes=[pltpu.VMEM((tm, D), BF16)],
        compiler_params=_params(("parallel", "arbitrary"), VMEM_LIMIT_BIG),
        name="mlp_final_norm",
    )(h, g, w_up, w_down, gf)


def _prep_q_weights(w_q_b):
    w = w_q_b.reshape(Q_LORA, MLA_HEADS, MLA_QK_DIM)
    w = jnp.pad(w, ((0, 0), (0, 0), (0, MLA_HEAD_PAD - MLA_QK_DIM)))
    return w.reshape(Q_LORA, MLA_HEADS * MLA_HEAD_PAD).astype(BF16)


def _prep_kv_weights(w_kv_b):
    w = w_kv_b.reshape(KV_LORA, MLA_HEADS, QK_NOPE + V_HEAD)
    wk = w[..., :QK_NOPE].reshape(KV_LORA, MLA_HEADS * QK_NOPE).astype(BF16)
    wv = w[..., QK_NOPE:].reshape(KV_LORA, MLA_HEADS * V_HEAD).astype(BF16)
    return wk, wv


def kernel(x, positions, norm_mix_g, w_in, ret_norm_g, w_ret_o, q_a_norm_g, w_q_b, kv_a_norm_g,
           w_kv_b, w_mla_o, w_out, norm_mlp_g, w_up, w_down, norm_f_g):
    B, S, D = x.shape
    T = B * S
    assert w_in.shape[0] == 1, "single-layer block: the final norm is fused into the MLP kernel"
    log_gamma = jnp.log(1.0 - 2.0 ** (-5.0 - jnp.arange(RET_HEADS, dtype=F32)))
    h = x.reshape(T, D)
    w_in_t = jnp.swapaxes(w_in[0], 0, 1)
    wq = _prep_q_weights(w_q_b[0])
    wk, wv = _prep_kv_weights(w_kv_b[0])
    u, cs, sn, lat = _norm_rope(h, norm_mix_g[0][None], positions.reshape(T, 1), w_in_t)
    proj = _inproj(u, w_in_t)
    q, kn, kpe, v, ry = _branches(proj, lat, u, w_in_t, q_a_norm_g[0][None], kv_a_norm_g[0][None],
                            wq, wk, wv, cs, sn, log_gamma, ret_norm_g[0][None], S)
    my, w_ret_b, w_mla_b, w_out_b = _attention(q, kn, kpe, v, w_ret_o[0], w_mla_o[0], w_out[0], B, S)
    merged = _merge(ry, my, w_ret_b, w_mla_b, proj, COL_CQ)
    h1 = _outproj(merged, w_out_b, h)
    out = _mlp(h1, norm_mlp_g[0][None], w_up[0], w_down[0], norm_f_g[None])
    return out.reshape(B, S, D)
```

```python
import functools

import jax
import jax.numpy as jnp
from jax import lax
from jax.experimental import pallas as pl
from jax.experimental.pallas import tpu as pltpu

F32 = jnp.float32
BF16 = jnp.bfloat16

EPS = 1e-6
ROPE_THETA = 10000.0
CHUNK = 64

RET_HEADS = 8
RET_QK_DIM = 128
RET_V_DIM = 256
RET_QK_W = RET_HEADS * RET_QK_DIM
RET_V_W = RET_HEADS * RET_V_DIM

MLA_HEADS = 16
Q_LORA = 512
KV_LORA = 512
QK_NOPE = 128
QK_ROPE = 64
V_HEAD = 128
MLA_QK_DIM = QK_NOPE + QK_ROPE
MLA_HEAD_PAD = 256

COL_RV = 2 * RET_QK_W
COL_RG = COL_RV + RET_V_W
COL_CQ = COL_RG + RET_V_W
COL_CKV = COL_CQ + Q_LORA
COL_KPE = COL_CKV + KV_LORA
COL_GRET = COL_KPE + QK_ROPE

LOG2_E = 1.4426950408889634
LANES = 128
VMEM_LIMIT = 56 * 1024 * 1024
VMEM_LIMIT_BIG = 62 * 1024 * 1024


def _params(sem, vmem_limit=VMEM_LIMIT):
    return pltpu.CompilerParams(dimension_semantics=sem, vmem_limit_bytes=vmem_limit)


def _rms(x, g):
    return x * lax.rsqrt(jnp.mean(x * x, axis=-1, keepdims=True) + EPS) * g


def _rope128(t, c, s):
    return t * c + pltpu.roll(t, LANES // 2, 1) * s


def _rope64(t, c, s_lo, s_hi):
    return t * c + pltpu.roll(t, LANES - QK_ROPE // 2, 1) * s_lo + pltpu.roll(t, QK_ROPE // 2, 1) * s_hi


def _sigmoid(x):
    return 1.0 / (1.0 + jnp.exp(-x))


def _norm_rope_body(x_ref, g_ref, pos_ref, inv_ref, wt_ref, u_ref, cs_ref, sn_ref, lat_ref, wb_ref):
    @pl.when(pl.program_id(0) == 0)
    def _cast():
        wb_ref[...] = wt_ref[...].astype(BF16)

    u = _rms(x_ref[...], g_ref[...]).astype(BF16)
    u_ref[...] = u
    ang = pos_ref[...].astype(F32) * inv_ref[...]
    cs_ref[...] = jnp.cos(ang)
    sn_ref[...] = jnp.sin(ang)
    lat_ref[...] = lax.dot_general(u, wb_ref[...], (((1,), (1,)), ((), ())),
                                   preferred_element_type=F32).astype(BF16)


def _norm_rope(x, g, pos, w_in_t, rows=1024):
    T, D = x.shape
    LW = Q_LORA + KV_LORA
    half_r = RET_QK_DIM // 2
    half_m = QK_ROPE // 2
    inv_r = ROPE_THETA ** (-jnp.arange(half_r, dtype=F32) / half_r)
    inv_m = ROPE_THETA ** (-jnp.arange(half_m, dtype=F32) / half_m)
    inv = jnp.concatenate([inv_r, inv_m, inv_m])[None]
    tab = pl.BlockSpec((rows, LANES), lambda i: (i, 0))
    return pl.pallas_call(
        _norm_rope_body,
        grid=(T // rows,),
        in_specs=[pl.BlockSpec((rows, D), lambda i: (i, 0)), pl.BlockSpec((1, D), lambda i: (0, 0)),
                  pl.BlockSpec((rows, 1), lambda i: (i, 0)), pl.BlockSpec((1, LANES), lambda i: (0, 0)),
                  pl.BlockSpec((LW, D), lambda i: (COL_CQ // LW, 0), pipeline_mode=pl.Buffered(1))],
        out_specs=[pl.BlockSpec((rows, D), lambda i: (i, 0)), tab, tab,
                   pl.BlockSpec((rows, LW), lambda i: (i, 0))],
        out_shape=[jax.ShapeDtypeStruct((T, D), BF16), jax.ShapeDtypeStruct((T, LANES), F32),
                   jax.ShapeDtypeStruct((T, LANES), F32), jax.ShapeDtypeStruct((T, LW), BF16)],
        scratch_shapes=[pltpu.VMEM((LW, D), BF16)],
        compiler_params=_params(("arbitrary",)),
        name="norm_rope_latent",
    )(x, g, pos, inv, w_in_t)


def _ret_rope_tables(cs, sn):
    lo = lax.broadcasted_iota(jnp.int32, cs.shape, 1) < LANES // 2
    return (jnp.where(lo, cs, pltpu.roll(cs, LANES // 2, 1)),
            jnp.where(lo, -sn, pltpu.roll(sn, LANES // 2, 1)))


def _mla_rope_tables(cs, sn):
    lane = lax.broadcasted_iota(jnp.int32, cs.shape, 1)
    half = QK_ROPE // 2
    c = pltpu.roll(cs, LANES // 2, 1)
    s = pltpu.roll(sn, LANES // 2, 1)
    return (jnp.where(lane < QK_ROPE, c, 0.0), jnp.where(lane < half, -s, 0.0),
            jnp.where((lane >= half) & (lane < QK_ROPE), s, 0.0))


def _inproj_body(u_ref, wt_ref, o_ref, wb_ref):
    @pl.when(pl.program_id(1) == 0)
    def _cast():
        wb_ref[...] = wt_ref[...].astype(BF16)

    o_ref[...] = lax.dot_general(u_ref[...], wb_ref[...], (((1,), (1,)), ((), ())),
                                 preferred_element_type=F32).astype(BF16)


def _inproj(u, w_in_t, tm=2048, tn=1024):
    T, D = u.shape
    n_head = COL_CQ // tn
    n_gate = (w_in_t.shape[0] - COL_GRET) // tn

    def w_rows(j, i):
        row = jnp.where(j < n_head, j * tn, COL_GRET + (j - n_head) * tn)
        return (pl.multiple_of(row, QK_ROPE), 0)

    return pl.pallas_call(
        _inproj_body,
        grid=(n_head + n_gate, T // tm),
        in_specs=[
            pl.BlockSpec((tm, D), lambda j, i: (i, 0)),
            pl.BlockSpec((pl.Element(tn), pl.Element(D)), w_rows),
        ],
        out_specs=pl.BlockSpec((tm, tn), lambda j, i: (i, j)),
        out_shape=jax.ShapeDtypeStruct((T, (n_head + n_gate) * tn), BF16),
        scratch_shapes=[pltpu.VMEM((tn, D), BF16)],
        compiler_params=_params(("parallel", "arbitrary")),
        name="inproj",
    )(u, w_in_t)


def _mla_part(cq_ref, ckv_ref, u_ref, wpe_ref, gq_ref, gkv_ref, wq_ref, wkv_ref, cs, sn,
              q_ref, kn_ref, kpe_ref, v_ref):
    qa = _rms(cq_ref[...].astype(F32), gq_ref[...]).astype(BF16)
    kva = _rms(ckv_ref[...].astype(F32), gkv_ref[...]).astype(BF16)
    c, s_lo, s_hi = _mla_rope_tables(cs, sn)
    kpe_raw = lax.dot_general(u_ref[...], wpe_ref[...].astype(BF16), (((1,), (1,)), ((), ())),
                              preferred_element_type=F32)
    kpe_ref[...] = _rope64(kpe_raw, c, s_lo, s_hi).astype(BF16)
    scale = MLA_QK_DIM ** -0.5 * LOG2_E
    cq = c * scale
    sq_lo = s_lo * scale
    sq_hi = s_hi * scale
    q = jnp.dot(qa, wq_ref[...], preferred_element_type=F32)
    for h in range(MLA_HEADS):
        b = h * MLA_HEAD_PAD
        q_ref[:, b:b + QK_NOPE] = (q[:, b:b + QK_NOPE] * scale).astype(BF16)
        q_ref[:, b + QK_NOPE:b + MLA_HEAD_PAD] = _rope64(
            q[:, b + QK_NOPE:b + MLA_HEAD_PAD], cq, sq_lo, sq_hi).astype(BF16)
    kv = jnp.dot(kva, wkv_ref[...], preferred_element_type=F32)
    for h in range(MLA_HEADS):
        b = h * (QK_NOPE + V_HEAD)
        kn_ref[:, h * QK_NOPE:(h + 1) * QK_NOPE] = kv[:, b:b + QK_NOPE].astype(BF16)
        v_ref[:, h * V_HEAD:(h + 1) * V_HEAD] = kv[:, b + QK_NOPE:b + QK_NOPE + V_HEAD].astype(BF16)


def _ret_tables_init(lg_ref, st_ref, dec_ref, qdec_ref, kdec_ref, bc):
    st_ref[...] = jnp.zeros_like(st_ref)
    n = lax.broadcasted_iota(jnp.int32, (bc, bc), 0)
    m = lax.broadcasted_iota(jnp.int32, (bc, bc), 1)
    dist = jnp.abs(n - m).astype(F32)
    visible = (m // CHUNK) <= (n // CHUNK)
    nl = lax.broadcasted_iota(jnp.int32, (bc, LANES), 0).astype(F32)
    for h in range(RET_HEADS):
        lg = lg_ref[h]
        dec_ref[h] = jnp.where(visible, jnp.exp(lg * dist), 0.0)
        qdec_ref[h] = jnp.exp(lg * (nl + 1.0))
        kdec_ref[h] = jnp.exp(lg * (bc - 1.0 - nl))


def _ret_block(rows, lg_ref, q_ref, k_ref, v_ref, rg_ref, cs, sn, g_ref, o_ref,
               st_ref, dec_ref, qdec_ref, kdec_ref, bc):
    c, s_ = _ret_rope_tables(cs, sn)
    k_scale = RET_QK_DIM ** -0.5
    ck = c * k_scale
    sk = s_ * k_scale
    for h in range(RET_HEADS):
        qs = slice(h * RET_QK_DIM, (h + 1) * RET_QK_DIM)
        vs = slice(h * RET_V_DIM, (h + 1) * RET_V_DIM)
        q32 = _rope128(q_ref[rows, qs].astype(F32), c, s_)
        k32 = _rope128(k_ref[rows, qs].astype(F32), ck, sk)
        q = q32.astype(BF16)
        k = k32.astype(BF16)
        v = v_ref[rows, vs]
        s = lax.dot_general(q, k, (((1,), (1,)), ((), ())), preferred_element_type=F32) * dec_ref[h]
        qd = (q32 * qdec_ref[h]).astype(BF16)
        st = st_ref[h]
        o = (jnp.dot(s.astype(BF16), v, preferred_element_type=F32)
             + jnp.dot(qd, st.astype(BF16), preferred_element_type=F32))
        kd = (k32 * kdec_ref[h]).astype(BF16)
        c_dec = jnp.exp(jnp.full((1, RET_V_DIM), lg_ref[h] * bc, F32))
        st_ref[h] = st * c_dec + lax.dot_general(
            kd, v, (((0,), (0,)), ((), ())), preferred_element_type=F32)
        mu = jnp.mean(o, axis=-1, keepdims=True)
        oc = o - mu
        var = jnp.mean(oc * oc, axis=-1, keepdims=True)
        y = oc * lax.rsqrt(var + EPS) * g_ref[:, vs]
        rg = rg_ref[rows, vs].astype(F32)
        o_ref[rows, vs] = (y * (rg * _sigmoid(rg))).astype(BF16)


def _branches_body(lg_ref, cq_ref, ckv_ref, u_ref, wpe_ref, gq_ref, gkv_ref, wq_ref, wkv_ref,
                   cs_ref, sn_ref, rq_ref, rk_ref, rv_ref, rg_ref, gr_ref,
                   q_ref, kn_ref, kpe_ref, v_ref, ry_ref, st_ref, dec_ref, qdec_ref, kdec_ref,
                   *, tm, bc, tiles_per_seq):
    @pl.when(pl.program_id(0) % tiles_per_seq == 0)
    def _init():
        _ret_tables_init(lg_ref, st_ref, dec_ref, qdec_ref, kdec_ref, bc)

    _mla_part(cq_ref, ckv_ref, u_ref, wpe_ref, gq_ref, gkv_ref, wq_ref, wkv_ref,
              cs_ref[...], sn_ref[...], q_ref, kn_ref, kpe_ref, v_ref)
    for blk in range(tm // bc):
        rows = slice(blk * bc, (blk + 1) * bc)
        _ret_block(rows, lg_ref, rq_ref, rk_ref, rv_ref, rg_ref, cs_ref[rows, :], sn_ref[rows, :],
                   gr_ref, ry_ref, st_ref, dec_ref, qdec_ref, kdec_ref, bc)


def _branches(proj, lat, u, w_in_t, gq, gkv, wq, wkv, cs, sn, lg, g_ret, seq, tm=512, bc=256):
    T, D = u.shape
    const = lambda shape: pl.BlockSpec(shape, lambda i, lg: (0, 0), pipeline_mode=pl.Buffered(1))
    rows = lambda w, col=0: pl.BlockSpec((tm, w), lambda i, lg: (i, col))
    QW = MLA_HEADS * MLA_HEAD_PAD
    KW = MLA_HEADS * QK_NOPE
    VW = MLA_HEADS * V_HEAD
    grid_spec = pltpu.PrefetchScalarGridSpec(
        num_scalar_prefetch=1,
        grid=(T // tm,),
        in_specs=[rows(Q_LORA, 0), rows(KV_LORA, 1), rows(D),
                  pl.BlockSpec((LANES, D), lambda i, lg: (COL_KPE // LANES, 0),
                               pipeline_mode=pl.Buffered(1)),
                  const((1, Q_LORA)), const((1, KV_LORA)),
                  const(wq.shape), const(wkv.shape),
                  rows(LANES), rows(LANES),
                  rows(RET_QK_W, 0), rows(RET_QK_W, 1),
                  rows(RET_V_W, COL_RV // RET_V_W), rows(RET_V_W, COL_RG // RET_V_W),
                  const((1, RET_V_W))],
        out_specs=[rows(QW), rows(KW), rows(LANES), rows(VW), rows(RET_V_W)],
        scratch_shapes=[pltpu.VMEM((RET_HEADS, RET_QK_DIM, RET_V_DIM), F32),
                        pltpu.VMEM((RET_HEADS, bc, bc), F32),
                        pltpu.VMEM((RET_HEADS, bc, LANES), F32),
                        pltpu.VMEM((RET_HEADS, bc, LANES), F32)],
    )
    return pl.pallas_call(
        functools.partial(_branches_body, tm=tm, bc=bc, tiles_per_seq=seq // tm),
        grid_spec=grid_spec,
        out_shape=[jax.ShapeDtypeStruct((T, QW), BF16), jax.ShapeDtypeStruct((T, KW), BF16),
                   jax.ShapeDtypeStruct((T, LANES), BF16), jax.ShapeDtypeStruct((T, VW), BF16),
                   jax.ShapeDtypeStruct((T, RET_V_W), BF16)],
        compiler_params=_params(("arbitrary",), VMEM_LIMIT_BIG),
        name="mla_proj_retention",
    )(lg, lat, lat, u, w_in_t, gq, gkv, wq, wkv, cs, sn, proj, proj, proj, proj, g_ret)


def _attn_body(q_ref, kn_ref, kpe_ref, v_ref, wr_ref, wm_ref, wo_ref, o_ref, wrb_ref, wmb_ref, wob_ref,
               m_ref, acc_ref, *, tq, tk, nh):
    wrb_ref[...] = wr_ref[...].astype(BF16)
    wmb_ref[...] = wm_ref[...].astype(BF16)
    wob_ref[...] = wo_ref[...].astype(BF16)
    qi = pl.program_id(2)
    m_ref[...] = jnp.full_like(m_ref, -jnp.inf)
    acc_ref[...] = jnp.zeros_like(acc_ref)
    reps = tk // LANES

    def step(start, diag_off):
        r0 = 0 if diag_off is None else diag_off
        rows = tq - r0
        for hh in range(nh):
            q = q_ref[r0:, hh * MLA_HEAD_PAD:(hh + 1) * MLA_HEAD_PAD]
            k = jnp.concatenate([kn_ref[pl.ds(start, tk), hh * QK_NOPE:(hh + 1) * QK_NOPE],
                                 kpe_ref[pl.ds(start, tk), :]], axis=1)
            v = v_ref[pl.ds(start, tk), hh * V_HEAD:(hh + 1) * V_HEAD]
            s = lax.dot_general(q, k, (((1,), (1,)), ((), ())), preferred_element_type=F32)
            if diag_off is not None:
                rq = lax.broadcasted_iota(jnp.int32, (rows, tk), 0) // CHUNK
                ck = lax.broadcasted_iota(jnp.int32, (rows, tk), 1) // CHUNK
                s = jnp.where(ck <= rq, s, -jnp.inf)
            m_old = m_ref[hh, r0:, :]
            m_new = jnp.maximum(m_old, jnp.max(s, axis=-1, keepdims=True))
            alpha = jnp.exp2(m_old - m_new)
            p = jnp.exp2(s - jnp.concatenate([m_new] * reps, axis=1))
            pv = jnp.dot(p.astype(BF16), jnp.concatenate([v, jnp.ones_like(v)], axis=1),
                         preferred_element_type=F32)
            acc_ref[hh, r0:, :] = jnp.concatenate([alpha, alpha], axis=1) * acc_ref[hh, r0:, :] + pv
            m_ref[hh, r0:, :] = m_new

    def full_steps(kb, carry):
        for j in range(tq // tk):
            step(pl.multiple_of(kb * tq + j * tk, tk), None)
        return carry

    lax.fori_loop(0, qi, full_steps, 0)
    for j in range(tq // tk):
        step(pl.multiple_of(qi * tq + j * tk, tk), j * tk)
    for hh in range(nh):
        o_ref[:, hh * V_HEAD:(hh + 1) * V_HEAD] = (
            acc_ref[hh, :, :V_HEAD] / acc_ref[hh, :, V_HEAD:]).astype(BF16)


def _attention(q, kn, kpe, v, w_ret_o, w_mla_o, w_out, batch, seq, tq=2048, tk=512, nh=2):
    nq = seq // tq
    nhp = MLA_HEADS // nh
    steps = batch * nhp * nq
    K, N = w_ret_o.shape
    rs = K // steps
    assert K % steps == 0 and rs % 16 == 0 and w_mla_o.shape == (K, N) and w_out.shape == (K, N)
    wspec = pl.BlockSpec((rs, N), lambda b, h, i: ((b * nhp + h) * nq + i, 0))
    return pl.pallas_call(
        functools.partial(_attn_body, tq=tq, tk=tk, nh=nh),
        grid=(batch, nhp, nq),
        in_specs=[
            pl.BlockSpec((tq, nh * MLA_HEAD_PAD), lambda b, h, i: (b * nq + i, h)),
            pl.BlockSpec((seq, nh * QK_NOPE), lambda b, h, i: (b, h)),
            pl.BlockSpec((seq, LANES), lambda b, h, i: (b, 0)),
            pl.BlockSpec((seq, nh * V_HEAD), lambda b, h, i: (b, h)),
            wspec, wspec, wspec,
        ],
        out_specs=[pl.BlockSpec((tq, nh * V_HEAD), lambda b, h, i: (b * nq + i, h)), wspec, wspec, wspec],
        out_shape=[jax.ShapeDtypeStruct((batch * seq, MLA_HEADS * V_HEAD), BF16)]
        + [jax.ShapeDtypeStruct((K, N), BF16)] * 3,
        scratch_shapes=[pltpu.VMEM((nh, tq, LANES), F32), pltpu.VMEM((nh, tq, 2 * V_HEAD), F32)],
        compiler_params=_params(("parallel", "parallel", "arbitrary")),
        name="mla_attention",
    )(q, kn, kpe, v, w_ret_o, w_mla_o, w_out)


def _merge_body(a_ref, b_ref, wa_ref, wb_ref, ga_ref, gb_ref, o_ref):
    ya = jnp.dot(a_ref[...], wa_ref[...], preferred_element_type=F32)
    yb = jnp.dot(b_ref[...], wb_ref[...], preferred_element_type=F32)
    ga = _sigmoid(ga_ref[...].astype(F32))
    gb = _sigmoid(gb_ref[...].astype(F32))
    o_ref[...] = (ga * ya + gb * yb).astype(BF16)


def _merge(ry, my, w_ret_o, w_mla_o, proj, gate_col, tm=512):
    T, K = ry.shape
    N = w_ret_o.shape[1]
    g0 = gate_col // N
    wspec = pl.BlockSpec((K, N), lambda i: (0, 0), pipeline_mode=pl.Buffered(1))
    return pl.pallas_call(
        _merge_body,
        grid=(T // tm,),
        in_specs=[
            pl.BlockSpec((tm, K), lambda i: (i, 0)),
            pl.BlockSpec((tm, K), lambda i: (i, 0)),
            wspec, wspec,
            pl.BlockSpec((tm, N), lambda i: (i, g0)),
            pl.BlockSpec((tm, N), lambda i: (i, g0 + 1)),
        ],
        out_specs=pl.BlockSpec((tm, N), lambda i: (i, 0)),
        out_shape=jax.ShapeDtypeStruct((T, N), BF16),
        compiler_params=_params(("parallel",), VMEM_LIMIT_BIG),
        name="gated_merge",
    )(ry, my, w_ret_o, w_mla_o, proj, proj)


def _outproj_body(a_ref, w_ref, x_ref, o_ref):
    o_ref[...] = x_ref[...] + jnp.dot(a_ref[...], w_ref[...], preferred_element_type=F32)


def _outproj(a, w, x, tm=1024):
    T, K = a.shape
    N = w.shape[1]
    return pl.pallas_call(
        _outproj_body,
        grid=(T // tm,),
        in_specs=[
            pl.BlockSpec((tm, K), lambda i: (i, 0)),
            pl.BlockSpec((K, N), lambda i: (0, 0), pipeline_mode=pl.Buffered(1)),
            pl.BlockSpec((tm, N), lambda i: (i, 0)),
        ],
        out_specs=pl.BlockSpec((tm, N), lambda i: (i, 0)),
        out_shape=jax.ShapeDtypeStruct((T, N), F32),
        compiler_params=_params(("parallel",), VMEM_LIMIT_BIG),
        name="out_proj",
    )(a, w, x)


def _mlp_body(h_ref, g_ref, wu_ref, wd_ref, gf_ref, o_ref, hn_ref):
    f = pl.program_id(1)

    @pl.when(f == 0)
    def _init():
        hn_ref[...] = _rms(h_ref[...], g_ref[...]).astype(BF16)
        o_ref[...] = jnp.zeros_like(o_ref)

    z = jnp.dot(hn_ref[...], wu_ref[...].astype(BF16), preferred_element_type=F32)
    a = jnp.square(jnp.maximum(z, 0.0)).astype(BF16)
    o_ref[...] += jnp.dot(a, wd_ref[...].astype(BF16), preferred_element_type=F32)

    @pl.when(f == pl.num_programs(1) - 1)
    def _finish():
        o_ref[...] = _rms(h_ref[...] + o_ref[...], gf_ref[...])


def _mlp(h, g, w_up, w_down, gf, tm=1024, tf=512):
    T, D = h.shape
    FF = w_up.shape[1]
    return pl.pallas_call(
        _mlp_body,
        grid=(T // tm, FF // tf),
        in_specs=[
            pl.BlockSpec((tm, D), lambda i, f: (i, 0)),
            pl.BlockSpec((1, D), lambda i, f: (0, 0)),
            pl.BlockSpec((D, tf), lambda i, f: (0, f)),
            pl.BlockSpec((tf, D), lambda i, f: (f, 0)),
            pl.BlockSpec((1, D), lambda i, f: (0, 0)),
        ],
        out_specs=pl.BlockSpec((tm, D), lambda i, f: (i, 0)),
        out_shape=jax.ShapeDtypeStruct((T, D), F32),
        scratch_shapes=[pltpu.VMEM((tm, D), BF16)],
        compiler_params=_params(("parallel", "arbitrary"), VMEM_LIMIT_BIG),
        name="mlp_final_norm",
    )(h, g, w_up, w_down, gf)


def _prep_q_weights(w_q_b):
    w = w_q_b.reshape(Q_LORA, MLA_HEADS, MLA_QK_DIM)
    w = jnp.pad(w, ((0, 0), (0, 0), (0, MLA_HEAD_PAD - MLA_QK_DIM)))
    return w.reshape(Q_LORA, MLA_HEADS * MLA_HEAD_PAD).astype(BF16)


def kernel(x, positions, norm_mix_g, w_in, ret_norm_g, w_ret_o, q_a_norm_g, w_q_b, kv_a_norm_g,
           w_kv_b, w_mla_o, w_out, norm_mlp_g, w_up, w_down, norm_f_g):
    B, S, D = x.shape
    T = B * S
    assert w_in.shape[0] == 1, "single-layer block: the final norm is fused into the MLP kernel"
    log_gamma = jnp.log(1.0 - 2.0 ** (-5.0 - jnp.arange(RET_HEADS, dtype=F32)))
    h = x.reshape(T, D)
    w_in_t = jnp.swapaxes(w_in[0], 0, 1)
    wq = _prep_q_weights(w_q_b[0])
    wkv = w_kv_b[0].astype(BF16)
    u, cs, sn, lat = _norm_rope(h, norm_mix_g[0][None], positions.reshape(T, 1), w_in_t)
    proj = _inproj(u, w_in_t)
    q, kn, kpe, v, ry = _branches(proj, lat, u, w_in_t, q_a_norm_g[0][None], kv_a_norm_g[0][None],
                            wq, wkv, cs, sn, log_gamma, ret_norm_g[0][None], S)
    my, w_ret_b, w_mla_b, w_out_b = _attention(q, kn, kpe, v, w_ret_o[0], w_mla_o[0], w_out[0], B, S)
    merged = _merge(ry, my, w_ret_b, w_mla_b, proj, COL_CQ)
    h1 = _outproj(merged, w_out_b, h)
    out = _mlp(h1, norm_mlp_g[0][None], w_up[0], w_down[0], norm_f_g[None])
    return out.reshape(B, S, D)
```

```python
import functools

import jax
import jax.numpy as jnp
from jax import lax
from jax.experimental import pallas as pl
from jax.experimental.pallas import tpu as pltpu

F32 = jnp.float32
BF16 = jnp.bfloat16

EPS = 1e-6
ROPE_THETA = 10000.0
CHUNK = 64

RET_HEADS = 8
RET_QK_DIM = 128
RET_V_DIM = 256
RET_QK_W = RET_HEADS * RET_QK_DIM
RET_V_W = RET_HEADS * RET_V_DIM

MLA_HEADS = 16
Q_LORA = 512
KV_LORA = 512
QK_NOPE = 128
QK_ROPE = 64
V_HEAD = 128
MLA_QK_DIM = QK_NOPE + QK_ROPE
MLA_HEAD_PAD = 256

COL_RV = 2 * RET_QK_W
COL_RG = COL_RV + RET_V_W
COL_CQ = COL_RG + RET_V_W
COL_CKV = COL_CQ + Q_LORA
COL_KPE = COL_CKV + KV_LORA
COL_GRET = COL_KPE + QK_ROPE

LOG2_E = 1.4426950408889634
LANES = 128
VMEM_LIMIT = 56 * 1024 * 1024
VMEM_LIMIT_BIG = 62 * 1024 * 1024


def _params(sem, vmem_limit=VMEM_LIMIT):
    return pltpu.CompilerParams(dimension_semantics=sem, vmem_limit_bytes=vmem_limit)


def _rms(x, g):
    return x * lax.rsqrt(jnp.mean(x * x, axis=-1, keepdims=True) + EPS) * g


def _rope128(t, c, s):
    return t * c + pltpu.roll(t, LANES // 2, 1) * s


def _rope64(t, c, s_lo, s_hi):
    return t * c + pltpu.roll(t, LANES - QK_ROPE // 2, 1) * s_lo + pltpu.roll(t, QK_ROPE // 2, 1) * s_hi


def _sigmoid(x):
    return 1.0 / (1.0 + jnp.exp(-x))


def _norm_rope_body(x_ref, g_ref, pos_ref, inv_ref, wt_ref, u_ref, cs_ref, sn_ref, lat_ref, wb_ref):
    @pl.when(pl.program_id(0) == 0)
    def _cast():
        wb_ref[...] = wt_ref[...].astype(BF16)

    ang = pos_ref[...].astype(F32) * inv_ref[...]
    cs_ref[...] = jnp.cos(ang)
    sn_ref[...] = jnp.sin(ang)
    half = x_ref.shape[0] // 2
    for r in range(2):
        rows = slice(r * half, (r + 1) * half)
        u = _rms(x_ref[rows, :], g_ref[...]).astype(BF16)
        u_ref[rows, :] = u
        lat_ref[rows, :] = lax.dot_general(u, wb_ref[...], (((1,), (1,)), ((), ())),
                                           preferred_element_type=F32).astype(BF16)


def _norm_rope(x, g, pos, w_in_t, rows=1024):
    T, D = x.shape
    LW = Q_LORA + KV_LORA
    half_r = RET_QK_DIM // 2
    half_m = QK_ROPE // 2
    inv_r = ROPE_THETA ** (-jnp.arange(half_r, dtype=F32) / half_r)
    inv_m = ROPE_THETA ** (-jnp.arange(half_m, dtype=F32) / half_m)
    inv = jnp.concatenate([inv_r, inv_m, inv_m])[None]
    tab = pl.BlockSpec((rows, LANES), lambda i: (i, 0))
    return pl.pallas_call(
        _norm_rope_body,
        grid=(T // rows,),
        in_specs=[pl.BlockSpec((rows, D), lambda i: (i, 0)), pl.BlockSpec((1, D), lambda i: (0, 0)),
                  pl.BlockSpec((rows, 1), lambda i: (i, 0)), pl.BlockSpec((1, LANES), lambda i: (0, 0)),
                  pl.BlockSpec((LW, D), lambda i: (COL_CQ // LW, 0), pipeline_mode=pl.Buffered(1))],
        out_specs=[pl.BlockSpec((rows, D), lambda i: (i, 0)), tab, tab,
                   pl.BlockSpec((rows, LW), lambda i: (i, 0))],
        out_shape=[jax.ShapeDtypeStruct((T, D), BF16), jax.ShapeDtypeStruct((T, LANES), F32),
                   jax.ShapeDtypeStruct((T, LANES), F32), jax.ShapeDtypeStruct((T, LW), BF16)],
        scratch_shapes=[pltpu.VMEM((LW, D), BF16)],
        compiler_params=_params(("arbitrary",)),
        name="norm_rope_latent",
    )(x, g, pos, inv, w_in_t)


def _ret_rope_tables(cs, sn):
    lo = lax.broadcasted_iota(jnp.int32, cs.shape, 1) < LANES // 2
    return (jnp.where(lo, cs, pltpu.roll(cs, LANES // 2, 1)),
            jnp.where(lo, -sn, pltpu.roll(sn, LANES // 2, 1)))


def _mla_rope_tables(cs, sn):
    lane = lax.broadcasted_iota(jnp.int32, cs.shape, 1)
    half = QK_ROPE // 2
    c = pltpu.roll(cs, LANES // 2, 1)
    s = pltpu.roll(sn, LANES // 2, 1)
    return (jnp.where(lane < QK_ROPE, c, 0.0), jnp.where(lane < half, -s, 0.0),
            jnp.where((lane >= half) & (lane < QK_ROPE), s, 0.0))


def _inproj_body(u_ref, wt_ref, o_ref, wb_ref):
    @pl.when(pl.program_id(1) == 0)
    def _cast():
        wb_ref[...] = wt_ref[...].astype(BF16)

    o_ref[...] = lax.dot_general(u_ref[...], wb_ref[...], (((1,), (1,)), ((), ())),
                                 preferred_element_type=F32).astype(BF16)


def _inproj(u, w_in_t, tm=2048, tn=1024):
    T, D = u.shape
    n_head = COL_CQ // tn
    n_gate = (w_in_t.shape[0] - COL_GRET) // tn

    def w_rows(j, i):
        row = jnp.where(j < n_head, j * tn, COL_GRET + (j - n_head) * tn)
        return (pl.multiple_of(row, QK_ROPE), 0)

    return pl.pallas_call(
        _inproj_body,
        grid=(n_head + n_gate, T // tm),
        in_specs=[
            pl.BlockSpec((tm, D), lambda j, i: (i, 0)),
            pl.BlockSpec((pl.Element(tn), pl.Element(D)), w_rows),
        ],
        out_specs=pl.BlockSpec((tm, tn), lambda j, i: (i, j)),
        out_shape=jax.ShapeDtypeStruct((T, (n_head + n_gate) * tn), BF16),
        scratch_shapes=[pltpu.VMEM((tn, D), BF16)],
        compiler_params=_params(("parallel", "arbitrary")),
        name="inproj",
    )(u, w_in_t)


def _mla_part(cq_ref, ckv_ref, u_ref, wpe_ref, gq_ref, gkv_ref, wq_ref, wkv_ref, cs, sn,
              q_ref, kn_ref, kpe_ref, v_ref):
    qa = _rms(cq_ref[...].astype(F32), gq_ref[...]).astype(BF16)
    kva = _rms(ckv_ref[...].astype(F32), gkv_ref[...]).astype(BF16)
    c, s_lo, s_hi = _mla_rope_tables(cs, sn)
    kpe_raw = lax.dot_general(u_ref[...], wpe_ref[...].astype(BF16), (((1,), (1,)), ((), ())),
                              preferred_element_type=F32)
    kpe_ref[...] = _rope64(kpe_raw, c, s_lo, s_hi).astype(BF16)
    scale = MLA_QK_DIM ** -0.5 * LOG2_E
    cq = c * scale
    sq_lo = s_lo * scale
    sq_hi = s_hi * scale
    q = jnp.dot(qa, wq_ref[...], preferred_element_type=F32)
    for h in range(MLA_HEADS):
        b = h * MLA_HEAD_PAD
        q_ref[:, b:b + QK_NOPE] = (q[:, b:b + QK_NOPE] * scale).astype(BF16)
        q_ref[:, b + QK_NOPE:b + MLA_HEAD_PAD] = _rope64(
            q[:, b + QK_NOPE:b + MLA_HEAD_PAD], cq, sq_lo, sq_hi).astype(BF16)
    kv = jnp.dot(kva, wkv_ref[...], preferred_element_type=F32)
    for h in range(MLA_HEADS):
        b = h * (QK_NOPE + V_HEAD)
        kn_ref[:, h * QK_NOPE:(h + 1) * QK_NOPE] = kv[:, b:b + QK_NOPE].astype(BF16)
        v_ref[:, h * V_HEAD:(h + 1) * V_HEAD] = kv[:, b + QK_NOPE:b + QK_NOPE + V_HEAD].astype(BF16)


def _ret_tables_init(lg_ref, st_ref, dec_ref, qdec_ref, kdec_ref, bc):
    st_ref[...] = jnp.zeros_like(st_ref)
    n = lax.broadcasted_iota(jnp.int32, (bc, bc), 0)
    m = lax.broadcasted_iota(jnp.int32, (bc, bc), 1)
    dist = jnp.abs(n - m).astype(F32)
    visible = (m // CHUNK) <= (n // CHUNK)
    nl = lax.broadcasted_iota(jnp.int32, (bc, LANES), 0).astype(F32)
    for h in range(RET_HEADS):
        lg = lg_ref[h]
        dec_ref[h] = jnp.where(visible, jnp.exp(lg * dist), 0.0)
        qdec_ref[h] = jnp.exp(lg * (nl + 1.0))
        kdec_ref[h] = jnp.exp(lg * (bc - 1.0 - nl))


def _ret_block(rows, lg_ref, q_ref, k_ref, v_ref, rg_ref, cs, sn, g_ref, o_ref,
               st_ref, dec_ref, qdec_ref, kdec_ref, bc):
    c, s_ = _ret_rope_tables(cs, sn)
    k_scale = RET_QK_DIM ** -0.5
    ck = c * k_scale
    sk = s_ * k_scale
    for h in range(RET_HEADS):
        qs = slice(h * RET_QK_DIM, (h + 1) * RET_QK_DIM)
        vs = slice(h * RET_V_DIM, (h + 1) * RET_V_DIM)
        q32 = _rope128(q_ref[rows, qs].astype(F32), c, s_)
        k32 = _rope128(k_ref[rows, qs].astype(F32), ck, sk)
        q = q32.astype(BF16)
        k = k32.astype(BF16)
        v = v_ref[rows, vs]
        s = lax.dot_general(q, k, (((1,), (1,)), ((), ())), preferred_element_type=F32) * dec_ref[h]
        qd = (q32 * qdec_ref[h]).astype(BF16)
        st = st_ref[h]
        o = (jnp.dot(s.astype(BF16), v, preferred_element_type=F32)
             + jnp.dot(qd, st.astype(BF16), preferred_element_type=F32))
        kd = (k32 * kdec_ref[h]).astype(BF16)
        c_dec = jnp.exp(jnp.full((1, RET_V_DIM), lg_ref[h] * bc, F32))
        st_ref[h] = st * c_dec + lax.dot_general(
            kd, v, (((0,), (0,)), ((), ())), preferred_element_type=F32)
        mu = jnp.mean(o, axis=-1, keepdims=True)
        oc = o - mu
        var = jnp.mean(oc * oc, axis=-1, keepdims=True)
        y = oc * lax.rsqrt(var + EPS) * g_ref[:, vs]
        rg = rg_ref[rows, vs].astype(F32)
        o_ref[rows, vs] = (y * (rg * _sigmoid(rg))).astype(BF16)


def _branches_body(lg_ref, cq_ref, ckv_ref, u_ref, wpe_ref, gq_ref, gkv_ref, wq_ref, wkv_ref,
                   cs_ref, sn_ref, rq_ref, rk_ref, rv_ref, rg_ref, gr_ref,
                   q_ref, kn_ref, kpe_ref, v_ref, ry_ref, st_ref, dec_ref, qdec_ref, kdec_ref,
                   *, tm, bc, tiles_per_seq):
    @pl.when(pl.program_id(0) % tiles_per_seq == 0)
    def _init():
        _ret_tables_init(lg_ref, st_ref, dec_ref, qdec_ref, kdec_ref, bc)

    _mla_part(cq_ref, ckv_ref, u_ref, wpe_ref, gq_ref, gkv_ref, wq_ref, wkv_ref,
              cs_ref[...], sn_ref[...], q_ref, kn_ref, kpe_ref, v_ref)
    for blk in range(tm // bc):
        rows = slice(blk * bc, (blk + 1) * bc)
        _ret_block(rows, lg_ref, rq_ref, rk_ref, rv_ref, rg_ref, cs_ref[rows, :], sn_ref[rows, :],
                   gr_ref, ry_ref, st_ref, dec_ref, qdec_ref, kdec_ref, bc)


def _branches(proj, lat, u, w_in_t, gq, gkv, wq, wkv, cs, sn, lg, g_ret, seq, tm=512, bc=256):
    T, D = u.shape
    const = lambda shape: pl.BlockSpec(shape, lambda i, lg: (0, 0), pipeline_mode=pl.Buffered(1))
    rows = lambda w, col=0: pl.BlockSpec((tm, w), lambda i, lg: (i, col))
    QW = MLA_HEADS * MLA_HEAD_PAD
    KW = MLA_HEADS * QK_NOPE
    VW = MLA_HEADS * V_HEAD
    grid_spec = pltpu.PrefetchScalarGridSpec(
        num_scalar_prefetch=1,
        grid=(T // tm,),
        in_specs=[rows(Q_LORA, 0), rows(KV_LORA, 1), rows(D),
                  pl.BlockSpec((LANES, D), lambda i, lg: (COL_KPE // LANES, 0),
                               pipeline_mode=pl.Buffered(1)),
                  const((1, Q_LORA)), const((1, KV_LORA)),
                  const(wq.shape), const(wkv.shape),
                  rows(LANES), rows(LANES),
                  rows(RET_QK_W, 0), rows(RET_QK_W, 1),
                  rows(RET_V_W, COL_RV // RET_V_W), rows(RET_V_W, COL_RG // RET_V_W),
                  const((1, RET_V_W))],
        out_specs=[rows(QW), rows(KW), rows(LANES), rows(VW), rows(RET_V_W)],
        scratch_shapes=[pltpu.VMEM((RET_HEADS, RET_QK_DIM, RET_V_DIM), F32),
                        pltpu.VMEM((RET_HEADS, bc, bc), F32),
                        pltpu.VMEM((RET_HEADS, bc, LANES), F32),
                        pltpu.VMEM((RET_HEADS, bc, LANES), F32)],
    )
    return pl.pallas_call(
        functools.partial(_branches_body, tm=tm, bc=bc, tiles_per_seq=seq // tm),
        grid_spec=grid_spec,
        out_shape=[jax.ShapeDtypeStruct((T, QW), BF16), jax.ShapeDtypeStruct((T, KW), BF16),
                   jax.ShapeDtypeStruct((T, LANES), BF16), jax.ShapeDtypeStruct((T, VW), BF16),
                   jax.ShapeDtypeStruct((T, RET_V_W), BF16)],
        compiler_params=_params(("arbitrary",), VMEM_LIMIT_BIG),
        name="mla_proj_retention",
    )(lg, lat, lat, u, w_in_t, gq, gkv, wq, wkv, cs, sn, proj, proj, proj, proj, g_ret)


def _attn_body(q_ref, kn_ref, kpe_ref, v_ref, wr_ref, wm_ref, wo_ref, o_ref, wrb_ref, wmb_ref, wob_ref,
               m_ref, acc_ref, *, tq, tk, nh):
    wrb_ref[...] = wr_ref[...].astype(BF16)
    wmb_ref[...] = wm_ref[...].astype(BF16)
    wob_ref[...] = wo_ref[...].astype(BF16)
    qi = pl.program_id(2)
    m_ref[...] = jnp.full_like(m_ref, -jnp.inf)
    acc_ref[...] = jnp.zeros_like(acc_ref)
    reps = tk // LANES

    def step(start, diag_off):
        r0 = 0 if diag_off is None else diag_off
        rows = tq - r0
        for hh in range(nh):
            q = q_ref[r0:, hh * MLA_HEAD_PAD:(hh + 1) * MLA_HEAD_PAD]
            k = jnp.concatenate([kn_ref[pl.ds(start, tk), hh * QK_NOPE:(hh + 1) * QK_NOPE],
                                 kpe_ref[pl.ds(start, tk), :]], axis=1)
            v = v_ref[pl.ds(start, tk), hh * V_HEAD:(hh + 1) * V_HEAD]
            s = lax.dot_general(q, k, (((1,), (1,)), ((), ())), preferred_element_type=F32)
            if diag_off is not None:
                rq = lax.broadcasted_iota(jnp.int32, (rows, tk), 0) // CHUNK
                ck = lax.broadcasted_iota(jnp.int32, (rows, tk), 1) // CHUNK
                s = jnp.where(ck <= rq, s, -jnp.inf)
            m_old = m_ref[hh, r0:, :]
            m_new = jnp.maximum(m_old, jnp.max(s, axis=-1, keepdims=True))
            alpha = jnp.exp2(m_old - m_new)
            p = jnp.exp2(s - jnp.concatenate([m_new] * reps, axis=1))
            pv = jnp.dot(p.astype(BF16), jnp.concatenate([v, jnp.ones_like(v)], axis=1),
                         preferred_element_type=F32)
            acc_ref[hh, r0:, :] = jnp.concatenate([alpha, alpha], axis=1) * acc_ref[hh, r0:, :] + pv
            m_ref[hh, r0:, :] = m_new

    def full_steps(kb, carry):
        for j in range(tq // tk):
            step(pl.multiple_of(kb * tq + j * tk, tk), None)
        return carry

    lax.fori_loop(0, qi, full_steps, 0)
    for j in range(tq // tk):
        step(pl.multiple_of(qi * tq + j * tk, tk), j * tk)
    for hh in range(nh):
        o_ref[:, hh * V_HEAD:(hh + 1) * V_HEAD] = (
            acc_ref[hh, :, :V_HEAD] / acc_ref[hh, :, V_HEAD:]).astype(BF16)


def _attention(q, kn, kpe, v, w_ret_o, w_mla_o, w_out, batch, seq, tq=2048, tk=512, nh=2):
    nq = seq // tq
    nhp = MLA_HEADS // nh
    steps = batch * nhp * nq
    K, N = w_ret_o.shape
    rs = K // steps
    assert K % steps == 0 and rs % 16 == 0 and w_mla_o.shape == (K, N) and w_out.shape == (K, N)
    wspec = pl.BlockSpec((rs, N), lambda b, h, i: ((b * nhp + h) * nq + i, 0))
    return pl.pallas_call(
        functools.partial(_attn_body, tq=tq, tk=tk, nh=nh),
        grid=(batch, nhp, nq),
        in_specs=[
            pl.BlockSpec((tq, nh * MLA_HEAD_PAD), lambda b, h, i: (b * nq + i, h)),
            pl.BlockSpec((seq, nh * QK_NOPE), lambda b, h, i: (b, h)),
            pl.BlockSpec((seq, LANES), lambda b, h, i: (b, 0)),
            pl.BlockSpec((seq, nh * V_HEAD), lambda b, h, i: (b, h)),
            wspec, wspec, wspec,
        ],
        out_specs=[pl.BlockSpec((tq, nh * V_HEAD), lambda b, h, i: (b * nq + i, h)), wspec, wspec, wspec],
        out_shape=[jax.ShapeDtypeStruct((batch * seq, MLA_HEADS * V_HEAD), BF16)]
        + [jax.ShapeDtypeStruct((K, N), BF16)] * 3,
        scratch_shapes=[pltpu.VMEM((nh, tq, LANES), F32), pltpu.VMEM((nh, tq, 2 * V_HEAD), F32)],
        compiler_params=_params(("parallel", "parallel", "arbitrary")),
        name="mla_attention",
    )(q, kn, kpe, v, w_ret_o, w_mla_o, w_out)


def _merge_body(a_ref, b_ref, wa_ref, wb_ref, ga_ref, gb_ref, o_ref):
    ya = jnp.dot(a_ref[...], wa_ref[...], preferred_element_type=F32)
    yb = jnp.dot(b_ref[...], wb_ref[...], preferred_element_type=F32)
    ga = _sigmoid(ga_ref[...].astype(F32))
    gb = _sigmoid(gb_ref[...].astype(F32))
    o_ref[...] = (ga * ya + gb * yb).astype(BF16)


def _merge(ry, my, w_ret_o, w_mla_o, proj, gate_col, tm=512):
    T, K = ry.shape
    N = w_ret_o.shape[1]
    g0 = gate_col // N
    wspec = pl.BlockSpec((K, N), lambda i: (0, 0), pipeline_mode=pl.Buffered(1))
    return pl.pallas_call(
        _merge_body,
        grid=(T // tm,),
        in_specs=[
            pl.BlockSpec((tm, K), lambda i: (i, 0)),
            pl.BlockSpec((tm, K), lambda i: (i, 0)),
            wspec, wspec,
            pl.BlockSpec((tm, N), lambda i: (i, g0)),
            pl.BlockSpec((tm, N), lambda i: (i, g0 + 1)),
        ],
        out_specs=pl.BlockSpec((tm, N), lambda i: (i, 0)),
        out_shape=jax.ShapeDtypeStruct((T, N), BF16),
        compiler_params=_params(("parallel",), VMEM_LIMIT_BIG),
        name="gated_merge",
    )(ry, my, w_ret_o, w_mla_o, proj, proj)


def _outproj_body(a_ref, w_ref, x_ref, o_ref):
    o_ref[...] = x_ref[...] + jnp.dot(a_ref[...], w_ref[...], preferred_element_type=F32)


def _outproj(a, w, x, tm=512):
    T, K = a.shape
    N = w.shape[1]
    return pl.pallas_call(
        _outproj_body,
        grid=(T // tm,),
        in_specs=[
            pl.BlockSpec((tm, K), lambda i: (i, 0)),
            pl.BlockSpec((K, N), lambda i: (0, 0), pipeline_mode=pl.Buffered(1)),
            pl.BlockSpec((tm, N), lambda i: (i, 0)),
        ],
        out_specs=pl.BlockSpec((tm, N), lambda i: (i, 0)),
        out_shape=jax.ShapeDtypeStruct((T, N), F32),
        compiler_params=_params(("parallel",)),
        name="out_proj",
    )(a, w, x)


def _mlp_body(h_ref, g_ref, wu_ref, wd_ref, gf_ref, o_ref, hn_ref):
    f = pl.program_id(1)

    @pl.when(f == 0)
    def _init():
        hn_ref[...] = _rms(h_ref[...], g_ref[...]).astype(BF16)
        o_ref[...] = jnp.zeros_like(o_ref)

    z = jnp.dot(hn_ref[...], wu_ref[...].astype(BF16), preferred_element_type=F32)
    a = jnp.square(jnp.maximum(z, 0.0)).astype(BF16)
    o_ref[...] += jnp.dot(a, wd_ref[...].astype(BF16), preferred_element_type=F32)

    @pl.when(f == pl.num_programs(1) - 1)
    def _finish():
        o_ref[...] = _rms(h_ref[...] + o_ref[...], gf_ref[...])


def _mlp(h, g, w_up, w_down, gf, tm=1024, tf=512):
    T, D = h.shape
    FF = w_up.shape[1]
    return pl.pallas_call(
        _mlp_body,
        grid=(T // tm, FF // tf),
        in_specs=[
            pl.BlockSpec((tm, D), lambda i, f: (i, 0)),
            pl.BlockSpec((1, D), lambda i, f: (0, 0)),
            pl.BlockSpec((D, tf), lambda i, f: (0, f)),
            pl.BlockSpec((tf, D), lambda i, f: (f, 0)),
            pl.BlockSpec((1, D), lambda i, f: (0, 0)),
        ],
        out_specs=pl.BlockSpec((tm, D), lambda i, f: (i, 0)),
        out_shape=jax.ShapeDtypeStruct((T, D), F32),
        scratch_shapes=[pltpu.VMEM((tm, D), BF16)],
        compiler_params=_params(("parallel", "arbitrary"), VMEM_LIMIT_BIG),
        name="mlp_final_norm",
    )(h, g, w_up, w_down, gf)


def _prep_q_weights(w_q_b):
    w = w_q_b.reshape(Q_LORA, MLA_HEADS, MLA_QK_DIM)
    w = jnp.pad(w, ((0, 0), (0, 0), (0, MLA_HEAD_PAD - MLA_QK_DIM)))
    return w.reshape(Q_LORA, MLA_HEADS * MLA_HEAD_PAD).astype(BF16)


def kernel(x, positions, norm_mix_g, w_in, ret_norm_g, w_ret_o, q_a_norm_g, w_q_b, kv_a_norm_g,
           w_kv_b, w_mla_o, w_out, norm_mlp_g, w_up, w_down, norm_f_g):
    B, S, D = x.shape
    T = B * S
    assert w_in.shape[0] == 1, "single-layer block: the final norm is fused into the MLP kernel"
    log_gamma = jnp.log(1.0 - 2.0 ** (-5.0 - jnp.arange(RET_HEADS, dtype=F32)))
    h = x.reshape(T, D)
    w_in_t = jnp.swapaxes(w_in[0], 0, 1)
    wq = _prep_q_weights(w_q_b[0])
    wkv = w_kv_b[0].astype(BF16)
    u, cs, sn, lat = _norm_rope(h, norm_mix_g[0][None], positions.reshape(T, 1), w_in_t)
    proj = _inproj(u, w_in_t)
    q, kn, kpe, v, ry = _branches(proj, lat, u, w_in_t, q_a_norm_g[0][None], kv_a_norm_g[0][None],
                            wq, wkv, cs, sn, log_gamma, ret_norm_g[0][None], S)
    my, w_ret_b, w_mla_b, w_out_b = _attention(q, kn, kpe, v, w_ret_o[0], w_mla_o[0], w_out[0], B, S)
    merged = _merge(ry, my, w_ret_b, w_mla_b, proj, COL_CQ)
    h1 = _outproj(merged, w_out_b, h)
    out = _mlp(h1, norm_mlp_g[0][None], w_up[0], w_down[0], norm_f_g[None])
    return out.reshape(B, S, D)
```

```python
import functools

import jax
import jax.numpy as jnp
from jax import lax
from jax.experimental import pallas as pl
from jax.experimental.pallas import tpu as pltpu

F32 = jnp.float32
BF16 = jnp.bfloat16

EPS = 1e-6
ROPE_THETA = 10000.0
CHUNK = 64

RET_HEADS = 8
RET_QK_DIM = 128
RET_V_DIM = 256
RET_QK_W = RET_HEADS * RET_QK_DIM
RET_V_W = RET_HEADS * RET_V_DIM

MLA_HEADS = 16
Q_LORA = 512
KV_LORA = 512
QK_NOPE = 128
QK_ROPE = 64
V_HEAD = 128
MLA_QK_DIM = QK_NOPE + QK_ROPE
MLA_HEAD_PAD = 256

COL_RV = 2 * RET_QK_W
COL_RG = COL_RV + RET_V_W
COL_CQ = COL_RG + RET_V_W
COL_CKV = COL_CQ + Q_LORA
COL_KPE = COL_CKV + KV_LORA
COL_GRET = COL_KPE + QK_ROPE

LOG2_E = 1.4426950408889634
LANES = 128
VMEM_LIMIT = 56 * 1024 * 1024
VMEM_LIMIT_BIG = 62 * 1024 * 1024


def _params(sem, vmem_limit=VMEM_LIMIT):
    return pltpu.CompilerParams(dimension_semantics=sem, vmem_limit_bytes=vmem_limit)


def _rms(x, g):
    return x * lax.rsqrt(jnp.mean(x * x, axis=-1, keepdims=True) + EPS) * g


def _rope128(t, c, s):
    return t * c + pltpu.roll(t, LANES // 2, 1) * s


def _rope64(t, c, s_lo, s_hi):
    return t * c + pltpu.roll(t, LANES - QK_ROPE // 2, 1) * s_lo + pltpu.roll(t, QK_ROPE // 2, 1) * s_hi


def _sigmoid(x):
    return 1.0 / (1.0 + jnp.exp(-x))


def _norm_rope_body(x_ref, g_ref, pos_ref, inv_ref, wt_ref, u_ref, cs_ref, sn_ref, lat_ref, wb_ref):
    @pl.when(pl.program_id(0) == 0)
    def _cast():
        wb_ref[...] = wt_ref[...].astype(BF16)

    u = _rms(x_ref[...], g_ref[...]).astype(BF16)
    u_ref[...] = u
    ang = pos_ref[...].astype(F32) * inv_ref[...]
    cs_ref[...] = jnp.cos(ang)
    sn_ref[...] = jnp.sin(ang)
    lat_ref[...] = lax.dot_general(u, wb_ref[...], (((1,), (1,)), ((), ())),
                                   preferred_element_type=F32).astype(BF16)


def _norm_rope(x, g, pos, w_in_t, rows=512):
    T, D = x.shape
    LW = Q_LORA + KV_LORA
    half_r = RET_QK_DIM // 2
    half_m = QK_ROPE // 2
    inv_r = ROPE_THETA ** (-jnp.arange(half_r, dtype=F32) / half_r)
    inv_m = ROPE_THETA ** (-jnp.arange(half_m, dtype=F32) / half_m)
    inv = jnp.concatenate([inv_r, inv_m, inv_m])[None]
    tab = pl.BlockSpec((rows, LANES), lambda i: (i, 0))
    return pl.pallas_call(
        _norm_rope_body,
        grid=(T // rows,),
        in_specs=[pl.BlockSpec((rows, D), lambda i: (i, 0)), pl.BlockSpec((1, D), lambda i: (0, 0)),
                  pl.BlockSpec((rows, 1), lambda i: (i, 0)), pl.BlockSpec((1, LANES), lambda i: (0, 0)),
                  pl.BlockSpec((LW, D), lambda i: (COL_CQ // LW, 0), pipeline_mode=pl.Buffered(1))],
        out_specs=[pl.BlockSpec((rows, D), lambda i: (i, 0)), tab, tab,
                   pl.BlockSpec((rows, LW), lambda i: (i, 0))],
        out_shape=[jax.ShapeDtypeStruct((T, D), BF16), jax.ShapeDtypeStruct((T, LANES), F32),
                   jax.ShapeDtypeStruct((T, LANES), F32), jax.ShapeDtypeStruct((T, LW), BF16)],
        scratch_shapes=[pltpu.VMEM((LW, D), BF16)],
        compiler_params=_params(("arbitrary",)),
        name="norm_rope_latent",
    )(x, g, pos, inv, w_in_t)


def _ret_rope_tables(cs, sn):
    lo = lax.broadcasted_iota(jnp.int32, cs.shape, 1) < LANES // 2
    return (jnp.where(lo, cs, pltpu.roll(cs, LANES // 2, 1)),
            jnp.where(lo, -sn, pltpu.roll(sn, LANES // 2, 1)))


def _mla_rope_tables(cs, sn):
    lane = lax.broadcasted_iota(jnp.int32, cs.shape, 1)
    half = QK_ROPE // 2
    c = pltpu.roll(cs, LANES // 2, 1)
    s = pltpu.roll(sn, LANES // 2, 1)
    return (jnp.where(lane < QK_ROPE, c, 0.0), jnp.where(lane < half, -s, 0.0),
            jnp.where((lane >= half) & (lane < QK_ROPE), s, 0.0))


def _inproj_body(u_ref, wt_ref, o_ref, wb_ref):
    @pl.when(pl.program_id(1) == 0)
    def _cast():
        wb_ref[...] = wt_ref[...].astype(BF16)

    o_ref[...] = lax.dot_general(u_ref[...], wb_ref[...], (((1,), (1,)), ((), ())),
                                 preferred_element_type=F32).astype(BF16)


def _inproj(u, w_in_t, tm=2048, tn=1024):
    T, D = u.shape
    n_head = COL_CQ // tn
    n_gate = (w_in_t.shape[0] - COL_GRET) // tn

    def w_rows(j, i):
        row = jnp.where(j < n_head, j * tn, COL_GRET + (j - n_head) * tn)
        return (pl.multiple_of(row, QK_ROPE), 0)

    return pl.pallas_call(
        _inproj_body,
        grid=(n_head + n_gate, T // tm),
        in_specs=[
            pl.BlockSpec((tm, D), lambda j, i: (i, 0)),
            pl.BlockSpec((pl.Element(tn), pl.Element(D)), w_rows),
        ],
        out_specs=pl.BlockSpec((tm, tn), lambda j, i: (i, j)),
        out_shape=jax.ShapeDtypeStruct((T, (n_head + n_gate) * tn), BF16),
        scratch_shapes=[pltpu.VMEM((tn, D), BF16)],
        compiler_params=_params(("parallel", "arbitrary")),
        name="inproj",
    )(u, w_in_t)


def _mla_part(cq_ref, ckv_ref, u_ref, wpe_ref, gq_ref, gkv_ref, wq_ref, wk_ref, wv_ref, cs, sn,
              q_ref, kn_ref, kpe_ref, v_ref):
    qa = _rms(cq_ref[...].astype(F32), gq_ref[...]).astype(BF16)
    kva = _rms(ckv_ref[...].astype(F32), gkv_ref[...]).astype(BF16)
    c, s_lo, s_hi = _mla_rope_tables(cs, sn)
    kpe_raw = lax.dot_general(u_ref[...], wpe_ref[...].astype(BF16), (((1,), (1,)), ((), ())),
                              preferred_element_type=F32)
    kpe_ref[...] = _rope64(kpe_raw, c, s_lo, s_hi).astype(BF16)
    scale = MLA_QK_DIM ** -0.5 * LOG2_E
    cq = c * scale
    sq_lo = s_lo * scale
    sq_hi = s_hi * scale
    q = lax.dot_general(qa, wq_ref[...], (((1,), (1,)), ((), ())),
                        preferred_element_type=F32)
    for h in range(MLA_HEADS):
        b = h * MLA_HEAD_PAD
        q_ref[:, b:b + QK_NOPE] = (q[:, b:b + QK_NOPE] * scale).astype(BF16)
        q_ref[:, b + QK_NOPE:b + MLA_HEAD_PAD] = _rope64(
            q[:, b + QK_NOPE:b + MLA_HEAD_PAD], cq, sq_lo, sq_hi).astype(BF16)
    kn_ref[...] = jnp.dot(kva, wk_ref[...], preferred_element_type=F32).astype(BF16)
    v_ref[...] = jnp.dot(kva, wv_ref[...], preferred_element_type=F32).astype(BF16)


def _ret_tables_init(lg_ref, st_ref, dec_ref, qdec_ref, kdec_ref, bc):
    st_ref[...] = jnp.zeros_like(st_ref)
    n = lax.broadcasted_iota(jnp.int32, (bc, bc), 0)
    m = lax.broadcasted_iota(jnp.int32, (bc, bc), 1)
    dist = jnp.abs(n - m).astype(F32)
    visible = (m // CHUNK) <= (n // CHUNK)
    nl = lax.broadcasted_iota(jnp.int32, (bc, LANES), 0).astype(F32)
    for h in range(RET_HEADS):
        lg = lg_ref[h]
        dec_ref[h] = jnp.where(visible, jnp.exp(lg * dist), 0.0)
        qdec_ref[h] = jnp.exp(lg * (nl + 1.0))
        kdec_ref[h] = jnp.exp(lg * (bc - 1.0 - nl))


def _ret_block(rows, lg_ref, q_ref, k_ref, v_ref, rg_ref, cs, sn, g_ref, o_ref,
               st_ref, dec_ref, qdec_ref, kdec_ref, bc):
    c, s_ = _ret_rope_tables(cs, sn)
    k_scale = RET_QK_DIM ** -0.5
    ck = c * k_scale
    sk = s_ * k_scale
    for h in range(RET_HEADS):
        qs = slice(h * RET_QK_DIM, (h + 1) * RET_QK_DIM)
        vs = slice(h * RET_V_DIM, (h + 1) * RET_V_DIM)
        q32 = _rope128(q_ref[rows, qs].astype(F32), c, s_)
        k32 = _rope128(k_ref[rows, qs].astype(F32), ck, sk)
        q = q32.astype(BF16)
        k = k32.astype(BF16)
        v = v_ref[rows, vs]
        s = lax.dot_general(q, k, (((1,), (1,)), ((), ())), preferred_element_type=F32) * dec_ref[h]
        qd = (q32 * qdec_ref[h]).astype(BF16)
        st = st_ref[h]
        o = (jnp.dot(s.astype(BF16), v, preferred_element_type=F32)
             + jnp.dot(qd, st.astype(BF16), preferred_element_type=F32))
        kd = (k32 * kdec_ref[h]).astype(BF16)
        c_dec = jnp.exp(jnp.full((1, RET_V_DIM), lg_ref[h] * bc, F32))
        st_ref[h] = st * c_dec + lax.dot_general(
            kd, v, (((0,), (0,)), ((), ())), preferred_element_type=F32)
        mu = jnp.mean(o, axis=-1, keepdims=True)
        oc = o - mu
        var = jnp.mean(oc * oc, axis=-1, keepdims=True)
        y = oc * lax.rsqrt(var + EPS) * g_ref[:, vs]
        rg = rg_ref[rows, vs].astype(F32)
        o_ref[rows, vs] = (y * (rg * _sigmoid(rg))).astype(BF16)


def _branches_body(lg_ref, cq_ref, ckv_ref, u_ref, wpe_ref, gq_ref, gkv_ref, wq_ref, wk_ref, wv_ref,
                   cs_ref, sn_ref, rq_ref, rk_ref, rv_ref, rg_ref, gr_ref,
                   q_ref, kn_ref, kpe_ref, v_ref, ry_ref, st_ref, dec_ref, qdec_ref, kdec_ref,
                   *, tm, bc, tiles_per_seq):
    @pl.when(pl.program_id(0) % tiles_per_seq == 0)
    def _init():
        _ret_tables_init(lg_ref, st_ref, dec_ref, qdec_ref, kdec_ref, bc)

    _mla_part(cq_ref, ckv_ref, u_ref, wpe_ref, gq_ref, gkv_ref, wq_ref, wk_ref, wv_ref,
              cs_ref[...], sn_ref[...], q_ref, kn_ref, kpe_ref, v_ref)
    for blk in range(tm // bc):
        rows = slice(blk * bc, (blk + 1) * bc)
        _ret_block(rows, lg_ref, rq_ref, rk_ref, rv_ref, rg_ref, cs_ref[rows, :], sn_ref[rows, :],
                   gr_ref, ry_ref, st_ref, dec_ref, qdec_ref, kdec_ref, bc)


def _branches(proj, lat, u, w_in_t, gq, gkv, wq, wk, wv, cs, sn, lg, g_ret, seq, tm=512, bc=256):
    T, D = u.shape
    const = lambda shape: pl.BlockSpec(shape, lambda i, lg: (0, 0), pipeline_mode=pl.Buffered(1))
    rows = lambda w, col=0: pl.BlockSpec((tm, w), lambda i, lg: (i, col))
    QW = MLA_HEADS * MLA_HEAD_PAD
    KW = MLA_HEADS * QK_NOPE
    VW = MLA_HEADS * V_HEAD
    grid_spec = pltpu.PrefetchScalarGridSpec(
        num_scalar_prefetch=1,
        grid=(T // tm,),
        in_specs=[rows(Q_LORA, 0), rows(KV_LORA, 1), rows(D),
                  pl.BlockSpec((LANES, D), lambda i, lg: (COL_KPE // LANES, 0),
                               pipeline_mode=pl.Buffered(1)),
                  const((1, Q_LORA)), const((1, KV_LORA)),
                  const(wq.shape), const(wk.shape), const(wv.shape),
                  rows(LANES), rows(LANES),
                  rows(RET_QK_W, 0), rows(RET_QK_W, 1),
                  rows(RET_V_W, COL_RV // RET_V_W), rows(RET_V_W, COL_RG // RET_V_W),
                  const((1, RET_V_W))],
        out_specs=[rows(QW), rows(KW), rows(LANES), rows(VW), rows(RET_V_W)],
        scratch_shapes=[pltpu.VMEM((RET_HEADS, RET_QK_DIM, RET_V_DIM), F32),
                        pltpu.VMEM((RET_HEADS, bc, bc), F32),
                        pltpu.VMEM((RET_HEADS, bc, LANES), F32),
                        pltpu.VMEM((RET_HEADS, bc, LANES), F32)],
    )
    return pl.pallas_call(
        functools.partial(_branches_body, tm=tm, bc=bc, tiles_per_seq=seq // tm),
        grid_spec=grid_spec,
        out_shape=[jax.ShapeDtypeStruct((T, QW), BF16), jax.ShapeDtypeStruct((T, KW), BF16),
                   jax.ShapeDtypeStruct((T, LANES), BF16), jax.ShapeDtypeStruct((T, VW), BF16),
                   jax.ShapeDtypeStruct((T, RET_V_W), BF16)],
        compiler_params=_params(("arbitrary",), VMEM_LIMIT_BIG),
        name="mla_proj_retention",
    )(lg, lat, lat, u, w_in_t, gq, gkv, wq, wk, wv, cs, sn, proj, proj, proj, proj, g_ret)


def _attn_body(q_ref, kn_ref, kpe_ref, v_ref, wr_ref, wm_ref, wo_ref, o_ref, wrb_ref, wmb_ref, wob_ref,
               m_ref, acc_ref, *, tq, tk, nh):
    wrb_ref[...] = wr_ref[...].astype(BF16)
    wmb_ref[...] = wm_ref[...].astype(BF16)
    wob_ref[...] = wo_ref[...].astype(BF16)
    qi = pl.program_id(2)
    m_ref[...] = jnp.full_like(m_ref, -jnp.inf)
    acc_ref[...] = jnp.zeros_like(acc_ref)
    reps = tk // LANES

    def step(start, diag_off):
        r0 = 0 if diag_off is None else diag_off
        rows = tq - r0
        for hh in range(nh):
            q = q_ref[r0:, hh * MLA_HEAD_PAD:(hh + 1) * MLA_HEAD_PAD]
            k = jnp.concatenate([kn_ref[pl.ds(start, tk), hh * QK_NOPE:(hh + 1) * QK_NOPE],
                                 kpe_ref[pl.ds(start, tk), :]], axis=1)
            v = v_ref[pl.ds(start, tk), hh * V_HEAD:(hh + 1) * V_HEAD]
            s = lax.dot_general(q, k, (((1,), (1,)), ((), ())), preferred_element_type=F32)
            if diag_off is not None:
                rq = lax.broadcasted_iota(jnp.int32, (rows, tk), 0) // CHUNK
                ck = lax.broadcasted_iota(jnp.int32, (rows, tk), 1) // CHUNK
                s = jnp.where(ck <= rq, s, -jnp.inf)
            m_old = m_ref[hh, r0:, :]
            m_new = jnp.maximum(m_old, jnp.max(s, axis=-1, keepdims=True))
            alpha = jnp.exp2(m_old - m_new)
            p = jnp.exp2(s - jnp.concatenate([m_new] * reps, axis=1))
            pv = jnp.dot(p.astype(BF16), jnp.concatenate([v, jnp.ones_like(v)], axis=1),
                         preferred_element_type=F32)
            acc_ref[hh, r0:, :] = jnp.concatenate([alpha, alpha], axis=1) * acc_ref[hh, r0:, :] + pv
            m_ref[hh, r0:, :] = m_new

    def full_steps(kb, carry):
        for j in range(tq // tk):
            step(pl.multiple_of(kb * tq + j * tk, tk), None)
        return carry

    lax.fori_loop(0, qi, full_steps, 0)
    for j in range(tq // tk):
        step(pl.multiple_of(qi * tq + j * tk, tk), j * tk)
    for hh in range(nh):
        o_ref[:, hh * V_HEAD:(hh + 1) * V_HEAD] = (
            acc_ref[hh, :, :V_HEAD] / acc_ref[hh, :, V_HEAD:]).astype(BF16)


def _attention(q, kn, kpe, v, w_ret_o, w_mla_o, w_out, batch, seq, tq=2048, tk=512, nh=2):
    nq = seq // tq
    nhp = MLA_HEADS // nh
    steps = batch * nhp * nq
    K, N = w_ret_o.shape
    rs = K // steps
    assert K % steps == 0 and rs % 16 == 0 and w_mla_o.shape == (K, N) and w_out.shape == (K, N)
    wspec = pl.BlockSpec((rs, N), lambda b, h, i: ((b * nhp + h) * nq + i, 0))
    return pl.pallas_call(
        functools.partial(_attn_body, tq=tq, tk=tk, nh=nh),
        grid=(batch, nhp, nq),
        in_specs=[
            pl.BlockSpec((tq, nh * MLA_HEAD_PAD), lambda b, h, i: (b * nq + i, h)),
            pl.BlockSpec((seq, nh * QK_NOPE), lambda b, h, i: (b, h)),
            pl.BlockSpec((seq, LANES), lambda b, h, i: (b, 0)),
            pl.BlockSpec((seq, nh * V_HEAD), lambda b, h, i: (b, h)),
            wspec, wspec, wspec,
        ],
        out_specs=[pl.BlockSpec((tq, nh * V_HEAD), lambda b, h, i: (b * nq + i, h)), wspec, wspec, wspec],
        out_shape=[jax.ShapeDtypeStruct((batch * seq, MLA_HEADS * V_HEAD), BF16)]
        + [jax.ShapeDtypeStruct((K, N), BF16)] * 3,
        scratch_shapes=[pltpu.VMEM((nh, tq, LANES), F32), pltpu.VMEM((nh, tq, 2 * V_HEAD), F32)],
        compiler_params=_params(("parallel", "parallel", "arbitrary")),
        name="mla_attention",
    )(q, kn, kpe, v, w_ret_o, w_mla_o, w_out)


def _merge_out_body(a_ref, b_ref, wa_ref, wb_ref, ga_ref, gb_ref, wo_ref, x_ref, o_ref):
    ya = jnp.dot(a_ref[...], wa_ref[...], preferred_element_type=F32)
    yb = jnp.dot(b_ref[...], wb_ref[...], preferred_element_type=F32)
    ga = _sigmoid(ga_ref[...].astype(F32))
    gb = _sigmoid(gb_ref[...].astype(F32))
    merged = (ga * ya + gb * yb).astype(BF16)
    o_ref[...] = x_ref[...] + jnp.dot(merged, wo_ref[...], preferred_element_type=F32)


def _merge_out(ry, my, w_ret_o, w_mla_o, w_out, proj, gate_col, x, tm=256):
    T, K = ry.shape
    N = w_ret_o.shape[1]
    g0 = gate_col // N
    wspec = pl.BlockSpec((K, N), lambda i: (0, 0), pipeline_mode=pl.Buffered(1))
    rows = lambda col: pl.BlockSpec((tm, N), lambda i: (i, col))
    return pl.pallas_call(
        _merge_out_body,
        grid=(T // tm,),
        in_specs=[rows(0), rows(0), wspec, wspec, rows(g0), rows(g0 + 1), wspec, rows(0)],
        out_specs=rows(0),
        out_shape=jax.ShapeDtypeStruct((T, N), F32),
        compiler_params=_params(("parallel",), VMEM_LIMIT_BIG),
        name="merge_out_proj",
    )(ry, my, w_ret_o, w_mla_o, proj, proj, w_out, x)


def _mlp_body(h_ref, g_ref, wu_ref, wd_ref, gf_ref, o_ref, hn_ref):
    f = pl.program_id(1)

    @pl.when(f == 0)
    def _init():
        hn_ref[...] = _rms(h_ref[...], g_ref[...]).astype(BF16)
        o_ref[...] = jnp.zeros_like(o_ref)

    z = jnp.dot(hn_ref[...], wu_ref[...].astype(BF16), preferred_element_type=F32)
    a = jnp.square(jnp.maximum(z, 0.0)).astype(BF16)
    o_ref[...] += jnp.dot(a, wd_ref[...].astype(BF16), preferred_element_type=F32)

    @pl.when(f == pl.num_programs(1) - 1)
    def _finish():
        o_ref[...] = _rms(h_ref[...] + o_ref[...], gf_ref[...])


def _mlp(h, g, w_up, w_down, gf, tm=1024, tf=512):
    T, D = h.shape
    FF = w_up.shape[1]
    return pl.pallas_call(
        _mlp_body,
        grid=(T // tm, FF // tf),
        in_specs=[
            pl.BlockSpec((tm, D), lambda i, f: (i, 0)),
            pl.BlockSpec((1, D), lambda i, f: (0, 0)),
            pl.BlockSpec((D, tf), lambda i, f: (0, f)),
            pl.BlockSpec((tf, D), lambda i, f: (f, 0)),
            pl.BlockSpec((1, D), lambda i, f: (0, 0)),
        ],
        out_specs=pl.BlockSpec((tm, D), lambda i, f: (i, 0)),
        out_shape=jax.ShapeDtypeStruct((T, D), F32),
        scratch_shapes=[pltpu.VMEM((tm, D), BF16)],
        compiler_params=_params(("parallel", "arbitrary"), VMEM_LIMIT_BIG),
        name="mlp_final_norm",
    )(h, g, w_up, w_down, gf)


def _prep_q_weights(w_q_b):
    w = jnp.swapaxes(w_q_b, 0, 1).astype(BF16).reshape(MLA_HEADS, MLA_QK_DIM, Q_LORA)
    w = jnp.pad(w, ((0, 0), (0, MLA_HEAD_PAD - MLA_QK_DIM), (0, 0)))
    return w.reshape(MLA_HEADS * MLA_HEAD_PAD, Q_LORA)


def _prep_kv_weights(w_kv_b):
    w = w_kv_b.reshape(KV_LORA, MLA_HEADS, QK_NOPE + V_HEAD)
    wk = w[..., :QK_NOPE].reshape(KV_LORA, MLA_HEADS * QK_NOPE).astype(BF16)
    wv = w[..., QK_NOPE:].reshape(KV_LORA, MLA_HEADS * V_HEAD).astype(BF16)
    return wk, wv


def kernel(x, positions, norm_mix_g, w_in, ret_norm_g, w_ret_o, q_a_norm_g, w_q_b, kv_a_norm_g,
           w_kv_b, w_mla_o, w_out, norm_mlp_g, w_up, w_down, norm_f_g):
    B, S, D = x.shape
    T = B * S
    assert w_in.shape[0] == 1, "single-layer block: the final norm is fused into the MLP kernel"
    log_gamma = jnp.log(1.0 - 2.0 ** (-5.0 - jnp.arange(RET_HEADS, dtype=F32)))
    h = x.reshape(T, D)
    w_in_t = jnp.swapaxes(w_in[0], 0, 1)
    wq = _prep_q_weights(w_q_b[0])
    wk, wv = _prep_kv_weights(w_kv_b[0])
    u, cs, sn, lat = _norm_rope(h, norm_mix_g[0][None], positions.reshape(T, 1), w_in_t)
    proj = _inproj(u, w_in_t)
    q, kn, kpe, v, ry = _branches(proj, lat, u, w_in_t, q_a_norm_g[0][None], kv_a_norm_g[0][None],
                            wq, wk, wv, cs, sn, log_gamma, ret_norm_g[0][None], S)
    my, w_ret_b, w_mla_b, w_out_b = _attention(q, kn, kpe, v, w_ret_o[0], w_mla_o[0], w_out[0], B, S)
    h1 = _merge_out(ry, my, w_ret_b, w_mla_b, w_out_b, proj, COL_CQ, h)
    out = _mlp(h1, norm_mlp_g[0][None], w_up[0], w_down[0], norm_f_g[None])
    return out.reshape(B, S, D)
```

```python
import functools

import jax
import jax.numpy as jnp
from jax import lax
from jax.experimental import pallas as pl
from jax.experimental.pallas import tpu as pltpu

F32 = jnp.float32
BF16 = jnp.bfloat16

EPS = 1e-6
ROPE_THETA = 10000.0
CHUNK = 64

RET_HEADS = 8
RET_QK_DIM = 128
RET_V_DIM = 256
RET_QK_W = RET_HEADS * RET_QK_DIM
RET_V_W = RET_HEADS * RET_V_DIM

MLA_HEADS = 16
Q_LORA = 512
KV_LORA = 512
QK_NOPE = 128
QK_ROPE = 64
V_HEAD = 128
MLA_QK_DIM = QK_NOPE + QK_ROPE
MLA_HEAD_PAD = 256

COL_RV = 2 * RET_QK_W
COL_RG = COL_RV + RET_V_W
COL_CQ = COL_RG + RET_V_W
COL_CKV = COL_CQ + Q_LORA
COL_KPE = COL_CKV + KV_LORA
COL_GRET = COL_KPE + QK_ROPE

LOG2_E = 1.4426950408889634
LANES = 128
VMEM_LIMIT = 56 * 1024 * 1024
VMEM_LIMIT_BIG = 62 * 1024 * 1024


def _params(sem, vmem_limit=VMEM_LIMIT):
    return pltpu.CompilerParams(dimension_semantics=sem, vmem_limit_bytes=vmem_limit)


def _rms(x, g):
    return x * lax.rsqrt(jnp.mean(x * x, axis=-1, keepdims=True) + EPS) * g


def _rope128(t, c, s):
    return t * c + pltpu.roll(t, LANES // 2, 1) * s


def _rope64(t, c, s_lo, s_hi):
    return t * c + pltpu.roll(t, LANES - QK_ROPE // 2, 1) * s_lo + pltpu.roll(t, QK_ROPE // 2, 1) * s_hi


def _sigmoid(x):
    return 1.0 / (1.0 + jnp.exp(-x))


def _norm_rope_body(x_ref, g_ref, pos_ref, inv_ref, wt_ref, u_ref, cs_ref, sn_ref, lat_ref, wb_ref):
    @pl.when(pl.program_id(0) == 0)
    def _cast():
        wb_ref[...] = wt_ref[...].astype(BF16)

    u = _rms(x_ref[...], g_ref[...]).astype(BF16)
    u_ref[...] = u
    ang = pos_ref[...].astype(F32) * inv_ref[...]
    cs_ref[...] = jnp.cos(ang)
    sn_ref[...] = jnp.sin(ang)
    lat_ref[...] = lax.dot_general(u, wb_ref[...], (((1,), (1,)), ((), ())),
                                   preferred_element_type=F32).astype(BF16)


def _norm_rope(x, g, pos, w_in_t, rows=512):
    T, D = x.shape
    LW = Q_LORA + KV_LORA
    half_r = RET_QK_DIM // 2
    half_m = QK_ROPE // 2
    inv_r = ROPE_THETA ** (-jnp.arange(half_r, dtype=F32) / half_r)
    inv_m = ROPE_THETA ** (-jnp.arange(half_m, dtype=F32) / half_m)
    inv = jnp.concatenate([inv_r, inv_m, inv_m])[None]
    tab = pl.BlockSpec((rows, LANES), lambda i: (i, 0))
    return pl.pallas_call(
        _norm_rope_body,
        grid=(T // rows,),
        in_specs=[pl.BlockSpec((rows, D), lambda i: (i, 0)), pl.BlockSpec((1, D), lambda i: (0, 0)),
                  pl.BlockSpec((rows, 1), lambda i: (i, 0)), pl.BlockSpec((1, LANES), lambda i: (0, 0)),
                  pl.BlockSpec((LW, D), lambda i: (COL_CQ // LW, 0), pipeline_mode=pl.Buffered(1))],
        out_specs=[pl.BlockSpec((rows, D), lambda i: (i, 0)), tab, tab,
                   pl.BlockSpec((rows, LW), lambda i: (i, 0))],
        out_shape=[jax.ShapeDtypeStruct((T, D), BF16), jax.ShapeDtypeStruct((T, LANES), F32),
                   jax.ShapeDtypeStruct((T, LANES), F32), jax.ShapeDtypeStruct((T, LW), BF16)],
        scratch_shapes=[pltpu.VMEM((LW, D), BF16)],
        compiler_params=_params(("arbitrary",)),
        name="norm_rope_latent",
    )(x, g, pos, inv, w_in_t)


def _ret_rope_tables(cs, sn):
    lo = lax.broadcasted_iota(jnp.int32, cs.shape, 1) < LANES // 2
    return (jnp.where(lo, cs, pltpu.roll(cs, LANES // 2, 1)),
            jnp.where(lo, -sn, pltpu.roll(sn, LANES // 2, 1)))


def _mla_rope_tables(cs, sn):
    lane = lax.broadcasted_iota(jnp.int32, cs.shape, 1)
    half = QK_ROPE // 2
    c = pltpu.roll(cs, LANES // 2, 1)
    s = pltpu.roll(sn, LANES // 2, 1)
    return (jnp.where(lane < QK_ROPE, c, 0.0), jnp.where(lane < half, -s, 0.0),
            jnp.where((lane >= half) & (lane < QK_ROPE), s, 0.0))


def _inproj_body(u_ref, wt_ref, o_ref, wb_ref):
    @pl.when(pl.program_id(1) == 0)
    def _cast():
        wb_ref[...] = wt_ref[...].astype(BF16)

    o_ref[...] = lax.dot_general(u_ref[...], wb_ref[...], (((1,), (1,)), ((), ())),
                                 preferred_element_type=F32).astype(BF16)


def _inproj(u, w_in_t, tm=2048, tn=1024):
    T, D = u.shape
    n_head = COL_CQ // tn
    n_gate = (w_in_t.shape[0] - COL_GRET) // tn

    def w_rows(j, i):
        row = jnp.where(j < n_head, j * tn, COL_GRET + (j - n_head) * tn)
        return (pl.multiple_of(row, QK_ROPE), 0)

    return pl.pallas_call(
        _inproj_body,
        grid=(n_head + n_gate, T // tm),
        in_specs=[
            pl.BlockSpec((tm, D), lambda j, i: (i, 0)),
            pl.BlockSpec((pl.Element(tn), pl.Element(D)), w_rows),
        ],
        out_specs=pl.BlockSpec((tm, tn), lambda j, i: (i, j)),
        out_shape=jax.ShapeDtypeStruct((T, (n_head + n_gate) * tn), BF16),
        scratch_shapes=[pltpu.VMEM((tn, D), BF16)],
        compiler_params=_params(("parallel", "arbitrary")),
        name="inproj",
    )(u, w_in_t)


def _mla_part(cq_ref, ckv_ref, u_ref, wpe_ref, gq_ref, gkv_ref, wq_ref, wk_ref, wv_ref, cs, sn,
              q_ref, kn_ref, kpe_ref, v_ref):
    qa = _rms(cq_ref[...].astype(F32), gq_ref[...]).astype(BF16)
    kva = _rms(ckv_ref[...].astype(F32), gkv_ref[...]).astype(BF16)
    c, s_lo, s_hi = _mla_rope_tables(cs, sn)
    kpe_raw = lax.dot_general(u_ref[...], wpe_ref[...].astype(BF16), (((1,), (1,)), ((), ())),
                              preferred_element_type=F32)
    kpe_ref[...] = _rope64(kpe_raw, c, s_lo, s_hi).astype(BF16)
    scale = MLA_QK_DIM ** -0.5 * LOG2_E
    cq = c * scale
    sq_lo = s_lo * scale
    sq_hi = s_hi * scale
    q = lax.dot_general(qa, wq_ref[...], (((1,), (1,)), ((), ())),
                        preferred_element_type=F32)
    for h in range(MLA_HEADS):
        b = h * MLA_HEAD_PAD
        q_ref[:, b:b + QK_NOPE] = (q[:, b:b + QK_NOPE] * scale).astype(BF16)
        q_ref[:, b + QK_NOPE:b + MLA_HEAD_PAD] = _rope64(
            q[:, b + QK_NOPE:b + MLA_HEAD_PAD], cq, sq_lo, sq_hi).astype(BF16)
    kn_ref[...] = jnp.dot(kva, wk_ref[...], preferred_element_type=F32).astype(BF16)
    v_ref[...] = jnp.dot(kva, wv_ref[...], preferred_element_type=F32).astype(BF16)


def _ret_tables_init(lg_ref, st_ref, dec_ref, qdec_ref, kdec_ref, bc):
    st_ref[...] = jnp.zeros_like(st_ref)
    n = lax.broadcasted_iota(jnp.int32, (bc, bc), 0)
    m = lax.broadcasted_iota(jnp.int32, (bc, bc), 1)
    dist = jnp.abs(n - m).astype(F32)
    visible = (m // CHUNK) <= (n // CHUNK)
    nl = lax.broadcasted_iota(jnp.int32, (bc, LANES), 0).astype(F32)
    for h in range(RET_HEADS):
        lg = lg_ref[h]
        dec_ref[h] = jnp.where(visible, jnp.exp(lg * dist), 0.0)
        qdec_ref[h] = jnp.exp(lg * (nl + 1.0))
        kdec_ref[h] = jnp.exp(lg * (bc - 1.0 - nl))


def _ret_block(rows, lg_ref, q_ref, k_ref, v_ref, rg_ref, cs, sn, g_ref, o_ref,
               st_ref, dec_ref, qdec_ref, kdec_ref, bc):
    c, s_ = _ret_rope_tables(cs, sn)
    k_scale = RET_QK_DIM ** -0.5
    ck = c * k_scale
    sk = s_ * k_scale
    for h in range(RET_HEADS):
        qs = slice(h * RET_QK_DIM, (h + 1) * RET_QK_DIM)
        vs = slice(h * RET_V_DIM, (h + 1) * RET_V_DIM)
        q32 = _rope128(q_ref[rows, qs].astype(F32), c, s_)
        k32 = _rope128(k_ref[rows, qs].astype(F32), ck, sk)
        q = q32.astype(BF16)
        k = k32.astype(BF16)
        v = v_ref[rows, vs]
        s = lax.dot_general(q, k, (((1,), (1,)), ((), ())), preferred_element_type=F32) * dec_ref[h]
        qd = (q32 * qdec_ref[h]).astype(BF16)
        st = st_ref[h]
        o = (jnp.dot(s.astype(BF16), v, preferred_element_type=F32)
             + jnp.dot(qd, st.astype(BF16), preferred_element_type=F32))
        kd = (k32 * kdec_ref[h]).astype(BF16)
        c_dec = jnp.exp(jnp.full((1, RET_V_DIM), lg_ref[h] * bc, F32))
        st_ref[h] = st * c_dec + lax.dot_general(
            kd, v, (((0,), (0,)), ((), ())), preferred_element_type=F32)
        mu = jnp.mean(o, axis=-1, keepdims=True)
        oc = o - mu
        var = jnp.mean(oc * oc, axis=-1, keepdims=True)
        y = oc * lax.rsqrt(var + EPS) * g_ref[:, vs]
        rg = rg_ref[rows, vs].astype(F32)
        o_ref[rows, vs] = (y * (rg * _sigmoid(rg))).astype(BF16)


def _branches_body(lg_ref, cq_ref, ckv_ref, u_ref, wpe_ref, gq_ref, gkv_ref, wq_ref, wk_ref, wv_ref,
                   cs_ref, sn_ref, rq_ref, rk_ref, rv_ref, rg_ref, gr_ref,
                   q_ref, kn_ref, kpe_ref, v_ref, ry_ref, st_ref, dec_ref, qdec_ref, kdec_ref,
                   *, tm, bc, tiles_per_seq):
    @pl.when(pl.program_id(0) % tiles_per_seq == 0)
    def _init():
        _ret_tables_init(lg_ref, st_ref, dec_ref, qdec_ref, kdec_ref, bc)

    _mla_part(cq_ref, ckv_ref, u_ref, wpe_ref, gq_ref, gkv_ref, wq_ref, wk_ref, wv_ref,
              cs_ref[...], sn_ref[...], q_ref, kn_ref, kpe_ref, v_ref)
    for blk in range(tm // bc):
        rows = slice(blk * bc, (blk + 1) * bc)
        _ret_block(rows, lg_ref, rq_ref, rk_ref, rv_ref, rg_ref, cs_ref[rows, :], sn_ref[rows, :],
                   gr_ref, ry_ref, st_ref, dec_ref, qdec_ref, kdec_ref, bc)


def _branches(proj, lat, u, w_in_t, gq, gkv, wq, wk, wv, cs, sn, lg, g_ret, seq, tm=512, bc=256):
    T, D = u.shape
    const = lambda shape: pl.BlockSpec(shape, lambda i, lg: (0, 0), pipeline_mode=pl.Buffered(1))
    rows = lambda w, col=0: pl.BlockSpec((tm, w), lambda i, lg: (i, col))
    QW = MLA_HEADS * MLA_HEAD_PAD
    KW = MLA_HEADS * QK_NOPE
    VW = MLA_HEADS * V_HEAD
    grid_spec = pltpu.PrefetchScalarGridSpec(
        num_scalar_prefetch=1,
        grid=(T // tm,),
        in_specs=[rows(Q_LORA, 0), rows(KV_LORA, 1), rows(D),
                  pl.BlockSpec((LANES, D), lambda i, lg: (COL_KPE // LANES, 0),
                               pipeline_mode=pl.Buffered(1)),
                  const((1, Q_LORA)), const((1, KV_LORA)),
                  const(wq.shape), const(wk.shape), const(wv.shape),
                  rows(LANES), rows(LANES),
                  rows(RET_QK_W, 0), rows(RET_QK_W, 1),
                  rows(RET_V_W, COL_RV // RET_V_W), rows(RET_V_W, COL_RG // RET_V_W),
                  const((1, RET_V_W))],
        out_specs=[rows(QW), rows(KW), rows(LANES), rows(VW), rows(RET_V_W)],
        scratch_shapes=[pltpu.VMEM((RET_HEADS, RET_QK_DIM, RET_V_DIM), F32),
                        pltpu.VMEM((RET_HEADS, bc, bc), F32),
                        pltpu.VMEM((RET_HEADS, bc, LANES), F32),
                        pltpu.VMEM((RET_HEADS, bc, LANES), F32)],
    )
    return pl.pallas_call(
        functools.partial(_branches_body, tm=tm, bc=bc, tiles_per_seq=seq // tm),
        grid_spec=grid_spec,
        out_shape=[jax.ShapeDtypeStruct((T, QW), BF16), jax.ShapeDtypeStruct((T, KW), BF16),
                   jax.ShapeDtypeStruct((T, LANES), BF16), jax.ShapeDtypeStruct((T, VW), BF16),
                   jax.ShapeDtypeStruct((T, RET_V_W), BF16)],
        compiler_params=_params(("arbitrary",), VMEM_LIMIT_BIG),
        name="mla_proj_retention",
    )(lg, lat, lat, u, w_in_t, gq, gkv, wq, wk, wv, cs, sn, proj, proj, proj, proj, g_ret)


def _attn_body(q_ref, kn_ref, kpe_ref, v_ref, wr_ref, wm_ref, wo_ref, o_ref, wrb_ref, wmb_ref, wob_ref,
               m_ref, acc_ref, *, tq, tk, tr, nh):
    wrb_ref[...] = wr_ref[...].astype(BF16)
    wmb_ref[...] = wm_ref[...].astype(BF16)
    wob_ref[...] = wo_ref[...].astype(BF16)
    qi = pl.program_id(2)
    m_ref[...] = jnp.full_like(m_ref, -jnp.inf)
    acc_ref[...] = jnp.zeros_like(acc_ref)
    reps = tk // LANES

    def step(start, diag_off):
        r0 = 0 if diag_off is None else diag_off
        for hh in range(nh):
            k = jnp.concatenate([kn_ref[pl.ds(start, tk), hh * QK_NOPE:(hh + 1) * QK_NOPE],
                                 kpe_ref[pl.ds(start, tk), :]], axis=1)
            v = v_ref[pl.ds(start, tk), hh * V_HEAD:(hh + 1) * V_HEAD]
            v1 = jnp.concatenate([v, jnp.ones_like(v)], axis=1)
            for ra in range(r0, tq, tr):
                rb = min(ra + tr, tq)
                q = q_ref[ra:rb, hh * MLA_HEAD_PAD:(hh + 1) * MLA_HEAD_PAD]
                s = lax.dot_general(q, k, (((1,), (1,)), ((), ())), preferred_element_type=F32)
                if diag_off is not None and ra - r0 < tk:
                    rq = (lax.broadcasted_iota(jnp.int32, (rb - ra, tk), 0) + (ra - r0)) // CHUNK
                    ck = lax.broadcasted_iota(jnp.int32, (rb - ra, tk), 1) // CHUNK
                    s = jnp.where(ck <= rq, s, -jnp.inf)
                m_old = m_ref[hh, ra:rb, :]
                m_new = jnp.maximum(m_old, jnp.max(s, axis=-1, keepdims=True))
                alpha = jnp.exp2(m_old - m_new)
                p = jnp.exp2(s - jnp.concatenate([m_new] * reps, axis=1))
                pv = jnp.dot(p.astype(BF16), v1, preferred_element_type=F32)
                acc_ref[hh, ra:rb, :] = jnp.concatenate([alpha, alpha], axis=1) * acc_ref[hh, ra:rb, :] + pv
                m_ref[hh, ra:rb, :] = m_new

    def full_steps(kb, carry):
        for j in range(tq // tk):
            step(pl.multiple_of(kb * tq + j * tk, tk), None)
        return carry

    lax.fori_loop(0, qi, full_steps, 0)
    for j in range(tq // tk):
        step(pl.multiple_of(qi * tq + j * tk, tk), j * tk)
    for hh in range(nh):
        o_ref[:, hh * V_HEAD:(hh + 1) * V_HEAD] = (
            acc_ref[hh, :, :V_HEAD] / acc_ref[hh, :, V_HEAD:]).astype(BF16)


def _attention(q, kn, kpe, v, w_ret_o, w_mla_o, w_out, batch, seq, tq=2048, tk=512, tr=1024, nh=2):
    nq = seq // tq
    nhp = MLA_HEADS // nh
    steps = batch * nhp * nq
    K, N = w_ret_o.shape
    rs = K // steps
    assert K % steps == 0 and rs % 16 == 0 and w_mla_o.shape == (K, N) and w_out.shape == (K, N)
    wspec = pl.BlockSpec((rs, N), lambda b, h, i: ((b * nhp + h) * nq + i, 0))
    return pl.pallas_call(
        functools.partial(_attn_body, tq=tq, tk=tk, tr=tr, nh=nh),
        grid=(batch, nhp, nq),
        in_specs=[
            pl.BlockSpec((tq, nh * MLA_HEAD_PAD), lambda b, h, i: (b * nq + i, h)),
            pl.BlockSpec((seq, nh * QK_NOPE), lambda b, h, i: (b, h)),
            pl.BlockSpec((seq, LANES), lambda b, h, i: (b, 0)),
            pl.BlockSpec((seq, nh * V_HEAD), lambda b, h, i: (b, h)),
            wspec, wspec, wspec,
        ],
        out_specs=[pl.BlockSpec((tq, nh * V_HEAD), lambda b, h, i: (b * nq + i, h)), wspec, wspec, wspec],
        out_shape=[jax.ShapeDtypeStruct((batch * seq, MLA_HEADS * V_HEAD), BF16)]
        + [jax.ShapeDtypeStruct((K, N), BF16)] * 3,
        scratch_shapes=[pltpu.VMEM((nh, tq, LANES), F32), pltpu.VMEM((nh, tq, 2 * V_HEAD), F32)],
        compiler_params=_params(("parallel", "parallel", "arbitrary")),
        name="mla_attention",
    )(q, kn, kpe, v, w_ret_o, w_mla_o, w_out)


def _merge_out_body(a_ref, b_ref, wa_ref, wb_ref, ga_ref, gb_ref, wo_ref, x_ref, o_ref):
    ya = jnp.dot(a_ref[...], wa_ref[...], preferred_element_type=F32)
    yb = jnp.dot(b_ref[...], wb_ref[...], preferred_element_type=F32)
    ga = _sigmoid(ga_ref[...].astype(F32))
    gb = _sigmoid(gb_ref[...].astype(F32))
    merged = (ga * ya + gb * yb).astype(BF16)
    o_ref[...] = x_ref[...] + jnp.dot(merged, wo_ref[...], preferred_element_type=F32)


def _merge_out(ry, my, w_ret_o, w_mla_o, w_out, proj, gate_col, x, tm=256):
    T, K = ry.shape
    N = w_ret_o.shape[1]
    g0 = gate_col // N
    wspec = pl.BlockSpec((K, N), lambda i: (0, 0), pipeline_mode=pl.Buffered(1))
    rows = lambda col: pl.BlockSpec((tm, N), lambda i: (i, col))
    return pl.pallas_call(
        _merge_out_body,
        grid=(T // tm,),
        in_specs=[rows(0), rows(0), wspec, wspec, rows(g0), rows(g0 + 1), wspec, rows(0)],
        out_specs=rows(0),
        out_shape=jax.ShapeDtypeStruct((T, N), F32),
        compiler_params=_params(("parallel",), VMEM_LIMIT_BIG),
        name="merge_out_proj",
    )(ry, my, w_ret_o, w_mla_o, proj, proj, w_out, x)


def _mlp_body(h_ref, g_ref, wu_ref, wd_ref, gf_ref, o_ref, hn_ref):
    f = pl.program_id(1)

    @pl.when(f == 0)
    def _init():
        hn_ref[...] = _rms(h_ref[...], g_ref[...]).astype(BF16)
        o_ref[...] = jnp.zeros_like(o_ref)

    z = jnp.dot(hn_ref[...], wu_ref[...].astype(BF16), preferred_element_type=F32)
    a = jnp.square(jnp.maximum(z, 0.0)).astype(BF16)
    o_ref[...] += jnp.dot(a, wd_ref[...].astype(BF16), preferred_element_type=F32)

    @pl.when(f == pl.num_programs(1) - 1)
    def _finish():
        o_ref[...] = _rms(h_ref[...] + o_ref[...], gf_ref[...])


def _mlp(h, g, w_up, w_down, gf, tm=1024, tf=512):
    T, D = h.shape
    FF = w_up.shape[1]
    return pl.pallas_call(
        _mlp_body,
        grid=(T // tm, FF // tf),
        in_specs=[
            pl.BlockSpec((tm, D), lambda i, f: (i, 0)),
            pl.BlockSpec((1, D), lambda i, f: (0, 0)),
            pl.BlockSpec((D, tf), lambda i, f: (0, f)),
            pl.BlockSpec((tf, D), lambda i, f: (f, 0)),
            pl.BlockSpec((1, D), lambda i, f: (0, 0)),
        ],
        out_specs=pl.BlockSpec((tm, D), lambda i, f: (i, 0)),
        out_shape=jax.ShapeDtypeStruct((T, D), F32),
        scratch_shapes=[pltpu.VMEM((tm, D), BF16)],
        compiler_params=_params(("parallel", "arbitrary"), VMEM_LIMIT_BIG),
        name="mlp_final_norm",
    )(h, g, w_up, w_down, gf)


def _prep_q_weights(w_q_b):
    w = jnp.swapaxes(w_q_b, 0, 1).astype(BF16).reshape(MLA_HEADS, MLA_QK_DIM, Q_LORA)
    w = jnp.pad(w, ((0, 0), (0, MLA_HEAD_PAD - MLA_QK_DIM), (0, 0)))
    return w.reshape(MLA_HEADS * MLA_HEAD_PAD, Q_LORA)


def _prep_kv_weights(w_kv_b):
    w = w_kv_b.reshape(KV_LORA, MLA_HEADS, QK_NOPE + V_HEAD)
    wk = w[..., :QK_NOPE].reshape(KV_LORA, MLA_HEADS * QK_NOPE).astype(BF16)
    wv = w[..., QK_NOPE:].reshape(KV_LORA, MLA_HEADS * V_HEAD).astype(BF16)
    return wk, wv


def kernel(x, positions, norm_mix_g, w_in, ret_norm_g, w_ret_o, q_a_norm_g, w_q_b, kv_a_norm_g,
           w_kv_b, w_mla_o, w_out, norm_mlp_g, w_up, w_down, norm_f_g):
    B, S, D = x.shape
    T = B * S
    assert w_in.shape[0] == 1, "single-layer block: the final norm is fused into the MLP kernel"
    log_gamma = jnp.log(1.0 - 2.0 ** (-5.0 - jnp.arange(RET_HEADS, dtype=F32)))
    h = x.reshape(T, D)
    w_in_t = jnp.swapaxes(w_in[0], 0, 1)
    wq = _prep_q_weights(w_q_b[0])
    wk, wv = _prep_kv_weights(w_kv_b[0])
    u, cs, sn, lat = _norm_rope(h, norm_mix_g[0][None], positions.reshape(T, 1), w_in_t)
    proj = _inproj(u, w_in_t)
    q, kn, kpe, v, ry = _branches(proj, lat, u, w_in_t, q_a_norm_g[0][None], kv_a_norm_g[0][None],
                            wq, wk, wv, cs, sn, log_gamma, ret_norm_g[0][None], S)
    my, w_ret_b, w_mla_b, w_out_b = _attention(q, kn, kpe, v, w_ret_o[0], w_mla_o[0], w_out[0], B, S)
    h1 = _merge_out(ry, my, w_ret_b, w_mla_b, w_out_b, proj, COL_CQ, h)
    out = _mlp(h1, norm_mlp_g[0][None], w_up[0], w_down[0], norm_f_g[None])
    return out.reshape(B, S, D)
```

```python
import functools

import jax
import jax.numpy as jnp
from jax import lax
from jax.experimental import pallas as pl
from jax.experimental.pallas import tpu as pltpu

F32 = jnp.float32
BF16 = jnp.bfloat16

EPS = 1e-6
ROPE_THETA = 10000.0
CHUNK = 64

RET_HEADS = 8
RET_QK_DIM = 128
RET_V_DIM = 256
RET_QK_W = RET_HEADS * RET_QK_DIM
RET_V_W = RET_HEADS * RET_V_DIM

MLA_HEADS = 16
Q_LORA = 512
KV_LORA = 512
QK_NOPE = 128
QK_ROPE = 64
V_HEAD = 128
MLA_QK_DIM = QK_NOPE + QK_ROPE
MLA_HEAD_PAD = 256

COL_RV = 2 * RET_QK_W
COL_RG = COL_RV + RET_V_W
COL_CQ = COL_RG + RET_V_W
COL_CKV = COL_CQ + Q_LORA
COL_KPE = COL_CKV + KV_LORA
COL_GRET = COL_KPE + QK_ROPE

LOG2_E = 1.4426950408889634
LANES = 128
VMEM_LIMIT = 56 * 1024 * 1024
VMEM_LIMIT_BIG = 62 * 1024 * 1024


def _params(sem, vmem_limit=VMEM_LIMIT):
    return pltpu.CompilerParams(dimension_semantics=sem, vmem_limit_bytes=vmem_limit)


def _rms(x, g):
    return x * lax.rsqrt(jnp.mean(x * x, axis=-1, keepdims=True) + EPS) * g


def _rope128(t, c, s):
    return t * c + pltpu.roll(t, LANES // 2, 1) * s


def _rope64(t, c, s_lo, s_hi):
    return t * c + pltpu.roll(t, LANES - QK_ROPE // 2, 1) * s_lo + pltpu.roll(t, QK_ROPE // 2, 1) * s_hi


def _sigmoid(x):
    return 1.0 / (1.0 + jnp.exp(-x))


def _norm_rope_body(x_ref, g_ref, pos_ref, inv_ref, wt_ref, u_ref, cs_ref, sn_ref, lat_ref, wb_ref):
    @pl.when(pl.program_id(0) == 0)
    def _cast():
        wb_ref[...] = wt_ref[...].astype(BF16)

    u = _rms(x_ref[...], g_ref[...]).astype(BF16)
    u_ref[...] = u
    ang = pos_ref[...].astype(F32) * inv_ref[...]
    cs_ref[...] = jnp.cos(ang)
    sn_ref[...] = jnp.sin(ang)
    lat_ref[...] = lax.dot_general(u, wb_ref[...], (((1,), (1,)), ((), ())),
                                   preferred_element_type=F32).astype(BF16)


def _norm_rope(x, g, pos, w_in_t, rows=512):
    T, D = x.shape
    LW = Q_LORA + KV_LORA
    half_r = RET_QK_DIM // 2
    half_m = QK_ROPE // 2
    inv_r = ROPE_THETA ** (-jnp.arange(half_r, dtype=F32) / half_r)
    inv_m = ROPE_THETA ** (-jnp.arange(half_m, dtype=F32) / half_m)
    inv = jnp.concatenate([inv_r, inv_m, inv_m])[None]
    tab = pl.BlockSpec((rows, LANES), lambda i: (i, 0))
    return pl.pallas_call(
        _norm_rope_body,
        grid=(T // rows,),
        in_specs=[pl.BlockSpec((rows, D), lambda i: (i, 0)), pl.BlockSpec((1, D), lambda i: (0, 0)),
                  pl.BlockSpec((rows, 1), lambda i: (i, 0)), pl.BlockSpec((1, LANES), lambda i: (0, 0)),
                  pl.BlockSpec((LW, D), lambda i: (COL_CQ // LW, 0), pipeline_mode=pl.Buffered(1))],
        out_specs=[pl.BlockSpec((rows, D), lambda i: (i, 0)), tab, tab,
                   pl.BlockSpec((rows, LW), lambda i: (i, 0))],
        out_shape=[jax.ShapeDtypeStruct((T, D), BF16), jax.ShapeDtypeStruct((T, LANES), F32),
                   jax.ShapeDtypeStruct((T, LANES), F32), jax.ShapeDtypeStruct((T, LW), BF16)],
        scratch_shapes=[pltpu.VMEM((LW, D), BF16)],
        compiler_params=_params(("arbitrary",)),
        name="norm_rope_latent",
    )(x, g, pos, inv, w_in_t)


def _ret_rope_tables(cs, sn):
    lo = lax.broadcasted_iota(jnp.int32, cs.shape, 1) < LANES // 2
    return (jnp.where(lo, cs, pltpu.roll(cs, LANES // 2, 1)),
            jnp.where(lo, -sn, pltpu.roll(sn, LANES // 2, 1)))


def _mla_rope_tables(cs, sn):
    lane = lax.broadcasted_iota(jnp.int32, cs.shape, 1)
    half = QK_ROPE // 2
    c = pltpu.roll(cs, LANES // 2, 1)
    s = pltpu.roll(sn, LANES // 2, 1)
    return (jnp.where(lane < QK_ROPE, c, 0.0), jnp.where(lane < half, -s, 0.0),
            jnp.where((lane >= half) & (lane < QK_ROPE), s, 0.0))


def _inproj_body(u_ref, wt_ref, o_ref, wb_ref):
    @pl.when(pl.program_id(1) == 0)
    def _cast():
        wb_ref[...] = wt_ref[...].astype(BF16)

    o_ref[...] = lax.dot_general(u_ref[...], wb_ref[...], (((1,), (1,)), ((), ())),
                                 preferred_element_type=F32).astype(BF16)


def _inproj(u, w_in_t, tm=2048, tn=1024):
    T, D = u.shape
    n_head = COL_CQ // tn
    n_gate = (w_in_t.shape[0] - COL_GRET) // tn

    def w_rows(j, i):
        row = jnp.where(j < n_head, j * tn, COL_GRET + (j - n_head) * tn)
        return (pl.multiple_of(row, QK_ROPE), 0)

    return pl.pallas_call(
        _inproj_body,
        grid=(n_head + n_gate, T // tm),
        in_specs=[
            pl.BlockSpec((tm, D), lambda j, i: (i, 0)),
            pl.BlockSpec((pl.Element(tn), pl.Element(D)), w_rows),
        ],
        out_specs=pl.BlockSpec((tm, tn), lambda j, i: (i, j)),
        out_shape=jax.ShapeDtypeStruct((T, (n_head + n_gate) * tn), BF16),
        scratch_shapes=[pltpu.VMEM((tn, D), BF16)],
        compiler_params=_params(("parallel", "arbitrary")),
        name="inproj",
    )(u, w_in_t)


def _mla_part(cq_ref, ckv_ref, u_ref, wpe_ref, gq_ref, gkv_ref, wq_ref, wk_ref, wv_ref, cs, sn,
              q_ref, kn_ref, kpe_ref, v_ref):
    qa = _rms(cq_ref[...].astype(F32), gq_ref[...]).astype(BF16)
    kva = _rms(ckv_ref[...].astype(F32), gkv_ref[...]).astype(BF16)
    c, s_lo, s_hi = _mla_rope_tables(cs, sn)
    kpe_raw = lax.dot_general(u_ref[...], wpe_ref[...].astype(BF16), (((1,), (1,)), ((), ())),
                              preferred_element_type=F32)
    kpe_ref[...] = _rope64(kpe_raw, c, s_lo, s_hi).astype(BF16)
    scale = MLA_QK_DIM ** -0.5 * LOG2_E
    cq = c * scale
    sq_lo = s_lo * scale
    sq_hi = s_hi * scale
    q = lax.dot_general(qa, wq_ref[...], (((1,), (1,)), ((), ())),
                        preferred_element_type=F32)
    for h in range(MLA_HEADS):
        b = h * MLA_HEAD_PAD
        q_ref[:, b:b + QK_NOPE] = (q[:, b:b + QK_NOPE] * scale).astype(BF16)
        q_ref[:, b + QK_NOPE:b + MLA_HEAD_PAD] = _rope64(
            q[:, b + QK_NOPE:b + MLA_HEAD_PAD], cq, sq_lo, sq_hi).astype(BF16)
    kn_ref[...] = jnp.dot(kva, wk_ref[...], preferred_element_type=F32).astype(BF16)
    v_ref[...] = jnp.dot(kva, wv_ref[...], preferred_element_type=F32).astype(BF16)


def _ret_tables_init(lg_ref, st_ref, dec_ref, qdec_ref, kdec_ref, bc):
    st_ref[...] = jnp.zeros_like(st_ref)
    n = lax.broadcasted_iota(jnp.int32, (bc, bc), 0)
    m = lax.broadcasted_iota(jnp.int32, (bc, bc), 1)
    dist = jnp.abs(n - m).astype(F32)
    visible = (m // CHUNK) <= (n // CHUNK)
    nl = lax.broadcasted_iota(jnp.int32, (bc, LANES), 0).astype(F32)
    for h in range(RET_HEADS):
        lg = lg_ref[h]
        dec_ref[h] = jnp.where(visible, jnp.exp(lg * dist), 0.0)
        qdec_ref[h] = jnp.exp(lg * (nl + 1.0))
        kdec_ref[h] = jnp.exp(lg * (bc - 1.0 - nl))


def _ret_block(rows, lg_ref, q_ref, k_ref, v_ref, rg_ref, cs, sn, g_ref, o_ref,
               st_ref, dec_ref, qdec_ref, kdec_ref, bc):
    c, s_ = _ret_rope_tables(cs, sn)
    k_scale = RET_QK_DIM ** -0.5
    ck = c * k_scale
    sk = s_ * k_scale
    for h in range(RET_HEADS):
        qs = slice(h * RET_QK_DIM, (h + 1) * RET_QK_DIM)
        vs = slice(h * RET_V_DIM, (h + 1) * RET_V_DIM)
        q32 = _rope128(q_ref[rows, qs].astype(F32), c, s_)
        k32 = _rope128(k_ref[rows, qs].astype(F32), ck, sk)
        q = q32.astype(BF16)
        k = k32.astype(BF16)
        v = v_ref[rows, vs]
        s = lax.dot_general(q, k, (((1,), (1,)), ((), ())), preferred_element_type=F32) * dec_ref[h]
        qd = (q32 * qdec_ref[h]).astype(BF16)
        st = st_ref[h]
        o = (jnp.dot(s.astype(BF16), v, preferred_element_type=F32)
             + jnp.dot(qd, st.astype(BF16), preferred_element_type=F32))
        kd = (k32 * kdec_ref[h]).astype(BF16)
        c_dec = jnp.exp(jnp.full((1, RET_V_DIM), lg_ref[h] * bc, F32))
        st_ref[h] = st * c_dec + lax.dot_general(
            kd, v, (((0,), (0,)), ((), ())), preferred_element_type=F32)
        mu = jnp.mean(o, axis=-1, keepdims=True)
        oc = o - mu
        var = jnp.mean(oc * oc, axis=-1, keepdims=True)
        y = oc * lax.rsqrt(var + EPS) * g_ref[:, vs]
        rg = rg_ref[rows, vs].astype(F32)
        o_ref[rows, vs] = (y * (rg * _sigmoid(rg))).astype(BF16)


def _branches_body(lg_ref, cq_ref, ckv_ref, u_ref, wpe_ref, gq_ref, gkv_ref, wq_ref, wk_ref, wv_ref,
                   cs_ref, sn_ref, rq_ref, rk_ref, rv_ref, rg_ref, gr_ref,
                   q_ref, kn_ref, kpe_ref, v_ref, ry_ref, st_ref, dec_ref, qdec_ref, kdec_ref,
                   *, tm, bc, tiles_per_seq):
    @pl.when(pl.program_id(0) % tiles_per_seq == 0)
    def _init():
        _ret_tables_init(lg_ref, st_ref, dec_ref, qdec_ref, kdec_ref, bc)

    _mla_part(cq_ref, ckv_ref, u_ref, wpe_ref, gq_ref, gkv_ref, wq_ref, wk_ref, wv_ref,
              cs_ref[...], sn_ref[...], q_ref, kn_ref, kpe_ref, v_ref)
    for blk in range(tm // bc):
        rows = slice(blk * bc, (blk + 1) * bc)
        _ret_block(rows, lg_ref, rq_ref, rk_ref, rv_ref, rg_ref, cs_ref[rows, :], sn_ref[rows, :],
                   gr_ref, ry_ref, st_ref, dec_ref, qdec_ref, kdec_ref, bc)


def _branches(proj, lat, u, w_in_t, gq, gkv, wq, wk, wv, cs, sn, lg, g_ret, seq, tm=512, bc=256):
    T, D = u.shape
    const = lambda shape: pl.BlockSpec(shape, lambda i, lg: (0, 0), pipeline_mode=pl.Buffered(1))
    rows = lambda w, col=0: pl.BlockSpec((tm, w), lambda i, lg: (i, col))
    QW = MLA_HEADS * MLA_HEAD_PAD
    KW = MLA_HEADS * QK_NOPE
    VW = MLA_HEADS * V_HEAD
    grid_spec = pltpu.PrefetchScalarGridSpec(
        num_scalar_prefetch=1,
        grid=(T // tm,),
        in_specs=[rows(Q_LORA, 0), rows(KV_LORA, 1), rows(D),
                  pl.BlockSpec((LANES, D), lambda i, lg: (COL_KPE // LANES, 0),
                               pipeline_mode=pl.Buffered(1)),
                  const((1, Q_LORA)), const((1, KV_LORA)),
                  const(wq.shape), const(wk.shape), const(wv.shape),
                  rows(LANES), rows(LANES),
                  rows(RET_QK_W, 0), rows(RET_QK_W, 1),
                  rows(RET_V_W, COL_RV // RET_V_W), rows(RET_V_W, COL_RG // RET_V_W),
                  const((1, RET_V_W))],
        out_specs=[rows(QW), rows(KW), rows(LANES), rows(VW), rows(RET_V_W)],
        scratch_shapes=[pltpu.VMEM((RET_HEADS, RET_QK_DIM, RET_V_DIM), F32),
                        pltpu.VMEM((RET_HEADS, bc, bc), F32),
                        pltpu.VMEM((RET_HEADS, bc, LANES), F32),
                        pltpu.VMEM((RET_HEADS, bc, LANES), F32)],
    )
    return pl.pallas_call(
        functools.partial(_branches_body, tm=tm, bc=bc, tiles_per_seq=seq // tm),
        grid_spec=grid_spec,
        out_shape=[jax.ShapeDtypeStruct((T, QW), BF16), jax.ShapeDtypeStruct((T, KW), BF16),
                   jax.ShapeDtypeStruct((T, LANES), BF16), jax.ShapeDtypeStruct((T, VW), BF16),
                   jax.ShapeDtypeStruct((T, RET_V_W), BF16)],
        compiler_params=_params(("arbitrary",), VMEM_LIMIT_BIG),
        name="mla_proj_retention",
    )(lg, lat, lat, u, w_in_t, gq, gkv, wq, wk, wv, cs, sn, proj, proj, proj, proj, g_ret)


def _attn_body(q_ref, kn_ref, kpe_ref, v_ref, wr_ref, wm_ref, wo_ref, o_ref, wrb_ref, wmb_ref, wob_ref,
               m_ref, acc_ref, *, tq, tk, tr, nh):
    wrb_ref[...] = wr_ref[...].astype(BF16)
    wmb_ref[...] = wm_ref[...].astype(BF16)
    wob_ref[...] = wo_ref[...].astype(BF16)
    qi = pl.program_id(2)
    m_ref[...] = jnp.full_like(m_ref, -jnp.inf)
    acc_ref[...] = jnp.zeros_like(acc_ref)
    reps = tk // LANES

    def step(start, diag_off):
        r0 = 0 if diag_off is None else diag_off
        for hh in range(nh):
            k = jnp.concatenate([kn_ref[pl.ds(start, tk), hh * QK_NOPE:(hh + 1) * QK_NOPE],
                                 kpe_ref[pl.ds(start, tk), :]], axis=1)
            v = v_ref[pl.ds(start, tk), hh * V_HEAD:(hh + 1) * V_HEAD]
            v1 = jnp.concatenate([v, jnp.ones_like(v)], axis=1)
            for ra in range(r0, tq, tr):
                rb = min(ra + tr, tq)
                q = q_ref[ra:rb, hh * MLA_HEAD_PAD:(hh + 1) * MLA_HEAD_PAD]
                s = lax.dot_general(q, k, (((1,), (1,)), ((), ())), preferred_element_type=F32)
                if diag_off is not None and ra - r0 < tk:
                    rq = (lax.broadcasted_iota(jnp.int32, (rb - ra, tk), 0) + (ra - r0)) // CHUNK
                    ck = lax.broadcasted_iota(jnp.int32, (rb - ra, tk), 1) // CHUNK
                    s = jnp.where(ck <= rq, s, -jnp.inf)
                m_old = m_ref[hh, ra:rb, :]
                m_new = jnp.maximum(m_old, jnp.max(s, axis=-1, keepdims=True))
                alpha = jnp.exp2(m_old - m_new)
                p = jnp.exp2(s - jnp.concatenate([m_new] * reps, axis=1))
                pv = jnp.dot(p.astype(BF16), v1, preferred_element_type=F32)
                acc_ref[hh, ra:rb, :] = jnp.concatenate([alpha, alpha], axis=1) * acc_ref[hh, ra:rb, :] + pv
                m_ref[hh, ra:rb, :] = m_new

    def full_steps(kb, carry):
        for j in range(tq // tk):
            step(pl.multiple_of(kb * tq + j * tk, tk), None)
        return carry

    lax.fori_loop(0, qi, full_steps, 0)
    for j in range(tq // tk):
        step(pl.multiple_of(qi * tq + j * tk, tk), j * tk)
    for hh in range(nh):
        o_ref[:, hh * V_HEAD:(hh + 1) * V_HEAD] = (
            acc_ref[hh, :, :V_HEAD] / acc_ref[hh, :, V_HEAD:]).astype(BF16)


def _attention(q, kn, kpe, v, w_ret_o, w_mla_o, w_out, batch, seq, tq=2048, tk=512, tr=512, nh=2):
    nq = seq // tq
    nhp = MLA_HEADS // nh
    steps = batch * nhp * nq
    K, N = w_ret_o.shape
    rs = K // steps
    assert K % steps == 0 and rs % 16 == 0 and w_mla_o.shape == (K, N) and w_out.shape == (K, N)
    wspec = pl.BlockSpec((rs, N), lambda b, h, i: ((b * nhp + h) * nq + i, 0))
    return pl.pallas_call(
        functools.partial(_attn_body, tq=tq, tk=tk, tr=tr, nh=nh),
        grid=(batch, nhp, nq),
        in_specs=[
            pl.BlockSpec((tq, nh * MLA_HEAD_PAD), lambda b, h, i: (b * nq + i, h)),
            pl.BlockSpec((seq, nh * QK_NOPE), lambda b, h, i: (b, h)),
            pl.BlockSpec((seq, LANES), lambda b, h, i: (b, 0)),
            pl.BlockSpec((seq, nh * V_HEAD), lambda b, h, i: (b, h)),
            wspec, wspec, wspec,
        ],
        out_specs=[pl.BlockSpec((tq, nh * V_HEAD), lambda b, h, i: (b * nq + i, h)), wspec, wspec, wspec],
        out_shape=[jax.ShapeDtypeStruct((batch * seq, MLA_HEADS * V_HEAD), BF16)]
        + [jax.ShapeDtypeStruct((K, N), BF16)] * 3,
        scratch_shapes=[pltpu.VMEM((nh, tq, LANES), F32), pltpu.VMEM((nh, tq, 2 * V_HEAD), F32)],
        compiler_params=_params(("parallel", "parallel", "arbitrary")),
        name="mla_attention",
    )(q, kn, kpe, v, w_ret_o, w_mla_o, w_out)


def _merge_out_body(a_ref, b_ref, wa_ref, wb_ref, ga_ref, gb_ref, wo_ref, x_ref, o_ref):
    ya = jnp.dot(a_ref[...], wa_ref[...], preferred_element_type=F32)
    yb = jnp.dot(b_ref[...], wb_ref[...], preferred_element_type=F32)
    ga = _sigmoid(ga_ref[...].astype(F32))
    gb = _sigmoid(gb_ref[...].astype(F32))
    merged = (ga * ya + gb * yb).astype(BF16)
    o_ref[...] = x_ref[...] + jnp.dot(merged, wo_ref[...], preferred_element_type=F32)


def _merge_out(ry, my, w_ret_o, w_mla_o, w_out, proj, gate_col, x, tm=256):
    T, K = ry.shape
    N = w_ret_o.shape[1]
    g0 = gate_col // N
    wspec = pl.BlockSpec((K, N), lambda i: (0, 0), pipeline_mode=pl.Buffered(1))
    rows = lambda col: pl.BlockSpec((tm, N), lambda i: (i, col))
    return pl.pallas_call(
        _merge_out_body,
        grid=(T // tm,),
        in_specs=[rows(0), rows(0), wspec, wspec, rows(g0), rows(g0 + 1), wspec, rows(0)],
        out_specs=rows(0),
        out_shape=jax.ShapeDtypeStruct((T, N), F32),
        compiler_params=_params(("parallel",), VMEM_LIMIT_BIG),
        name="merge_out_proj",
    )(ry, my, w_ret_o, w_mla_o, proj, proj, w_out, x)


def _mlp_body(h_ref, g_ref, wu_ref, wd_ref, gf_ref, o_ref, hn_ref):
    f = pl.program_id(1)

    @pl.when(f == 0)
    def _init():
        hn_ref[...] = _rms(h_ref[...], g_ref[...]).astype(BF16)
        o_ref[...] = jnp.zeros_like(o_ref)

    z = jnp.dot(hn_ref[...], wu_ref[...].astype(BF16), preferred_element_type=F32)
    a = jnp.square(jnp.maximum(z, 0.0)).astype(BF16)
    o_ref[...] += jnp.dot(a, wd_ref[...].astype(BF16), preferred_element_type=F32)

    @pl.when(f == pl.num_programs(1) - 1)
    def _finish():
        o_ref[...] = _rms(h_ref[...] + o_ref[...], gf_ref[...])


def _mlp(h, g, w_up, w_down, gf, tm=1024, tf=512):
    T, D = h.shape
    FF = w_up.shape[1]
    return pl.pallas_call(
        _mlp_body,
        grid=(T // tm, FF // tf),
        in_specs=[
            pl.BlockSpec((tm, D), lambda i, f: (i, 0)),
            pl.BlockSpec((1, D), lambda i, f: (0, 0)),
            pl.BlockSpec((D, tf), lambda i, f: (0, f)),
            pl.BlockSpec((tf, D), lambda i, f: (f, 0)),
            pl.BlockSpec((1, D), lambda i, f: (0, 0)),
        ],
        out_specs=pl.BlockSpec((tm, D), lambda i, f: (i, 0)),
        out_shape=jax.ShapeDtypeStruct((T, D), F32),
        scratch_shapes=[pltpu.VMEM((tm, D), BF16)],
        compiler_params=_params(("parallel", "arbitrary"), VMEM_LIMIT_BIG),
        name="mlp_final_norm",
    )(h, g, w_up, w_down, gf)


def _prep_q_weights(w_q_b):
    w = jnp.swapaxes(w_q_b, 0, 1).astype(BF16).reshape(MLA_HEADS, MLA_QK_DIM, Q_LORA)
    w = jnp.pad(w, ((0, 0), (0, MLA_HEAD_PAD - MLA_QK_DIM), (0, 0)))
    return w.reshape(MLA_HEADS * MLA_HEAD_PAD, Q_LORA)


def _prep_kv_weights(w_kv_b):
    w = w_kv_b.reshape(KV_LORA, MLA_HEADS, QK_NOPE + V_HEAD)
    wk = w[..., :QK_NOPE].reshape(KV_LORA, MLA_HEADS * QK_NOPE).astype(BF16)
    wv = w[..., QK_NOPE:].reshape(KV_LORA, MLA_HEADS * V_HEAD).astype(BF16)
    return wk, wv


def kernel(x, positions, norm_mix_g, w_in, ret_norm_g, w_ret_o, q_a_norm_g, w_q_b, kv_a_norm_g,
           w_kv_b, w_mla_o, w_out, norm_mlp_g, w_up, w_down, norm_f_g):
    B, S, D = x.shape
    T = B * S
    assert w_in.shape[0] == 1, "single-layer block: the final norm is fused into the MLP kernel"
    log_gamma = jnp.log(1.0 - 2.0 ** (-5.0 - jnp.arange(RET_HEADS, dtype=F32)))
    h = x.reshape(T, D)
    w_in_t = jnp.swapaxes(w_in[0], 0, 1)
    wq = _prep_q_weights(w_q_b[0])
    wk, wv = _prep_kv_weights(w_kv_b[0])
    u, cs, sn, lat = _norm_rope(h, norm_mix_g[0][None], positions.reshape(T, 1), w_in_t)
    proj = _inproj(u, w_in_t)
    q, kn, kpe, v, ry = _branches(proj, lat, u, w_in_t, q_a_norm_g[0][None], kv_a_norm_g[0][None],
                            wq, wk, wv, cs, sn, log_gamma, ret_norm_g[0][None], S)
    my, w_ret_b, w_mla_b, w_out_b = _attention(q, kn, kpe, v, w_ret_o[0], w_mla_o[0], w_out[0], B, S)
    h1 = _merge_out(ry, my, w_ret_b, w_mla_b, w_out_b, proj, COL_CQ, h)
    out = _mlp(h1, norm_mlp_g[0][None], w_up[0], w_down[0], norm_f_g[None])
    return out.reshape(B, S, D)
```

```python
import functools

import jax
import jax.numpy as jnp
from jax import lax
from jax.experimental import pallas as pl
from jax.experimental.pallas import tpu as pltpu

F32 = jnp.float32
BF16 = jnp.bfloat16

EPS = 1e-6
ROPE_THETA = 10000.0
CHUNK = 64

RET_HEADS = 8
RET_QK_DIM = 128
RET_V_DIM = 256
RET_QK_W = RET_HEADS * RET_QK_DIM
RET_V_W = RET_HEADS * RET_V_DIM

MLA_HEADS = 16
Q_LORA = 512
KV_LORA = 512
QK_NOPE = 128
QK_ROPE = 64
V_HEAD = 128
MLA_QK_DIM = QK_NOPE + QK_ROPE
MLA_HEAD_PAD = 256

COL_RV = 2 * RET_QK_W
COL_RG = COL_RV + RET_V_W
COL_CQ = COL_RG + RET_V_W
COL_CKV = COL_CQ + Q_LORA
COL_KPE = COL_CKV + KV_LORA
COL_GRET = COL_KPE + QK_ROPE

LOG2_E = 1.4426950408889634
LANES = 128
VMEM_LIMIT = 56 * 1024 * 1024
VMEM_LIMIT_BIG = 62 * 1024 * 1024


def _params(sem, vmem_limit=VMEM_LIMIT):
    return pltpu.CompilerParams(dimension_semantics=sem, vmem_limit_bytes=vmem_limit)


def _rms(x, g):
    return x * lax.rsqrt(jnp.mean(x * x, axis=-1, keepdims=True) + EPS) * g


def _rope128(t, c, s):
    return t * c + pltpu.roll(t, LANES // 2, 1) * s


def _rope64(t, c, s_lo, s_hi):
    return t * c + pltpu.roll(t, LANES - QK_ROPE // 2, 1) * s_lo + pltpu.roll(t, QK_ROPE // 2, 1) * s_hi


def _sigmoid(x):
    return 1.0 / (1.0 + jnp.exp(-x))


def _norm_rope_body(x_ref, g_ref, pos_ref, inv_ref, wt_ref, u_ref, cs_ref, sn_ref, lat_ref, wb_ref):
    @pl.when(pl.program_id(0) == 0)
    def _cast():
        wb_ref[...] = wt_ref[...].astype(BF16)

    u = _rms(x_ref[...], g_ref[...]).astype(BF16)
    u_ref[...] = u
    ang = pos_ref[...].astype(F32) * inv_ref[...]
    cs_ref[...] = jnp.cos(ang)
    sn_ref[...] = jnp.sin(ang)
    lat_ref[...] = lax.dot_general(u, wb_ref[...], (((1,), (1,)), ((), ())),
                                   preferred_element_type=F32).astype(BF16)


def _norm_rope(x, g, pos, w_in_t, rows=512):
    T, D = x.shape
    LW = Q_LORA + KV_LORA
    half_r = RET_QK_DIM // 2
    half_m = QK_ROPE // 2
    inv_r = ROPE_THETA ** (-jnp.arange(half_r, dtype=F32) / half_r)
    inv_m = ROPE_THETA ** (-jnp.arange(half_m, dtype=F32) / half_m)
    inv = jnp.concatenate([inv_r, inv_m, inv_m])[None]
    tab = pl.BlockSpec((rows, LANES), lambda i: (i, 0))
    return pl.pallas_call(
        _norm_rope_body,
        grid=(T // rows,),
        in_specs=[pl.BlockSpec((rows, D), lambda i: (i, 0)), pl.BlockSpec((1, D), lambda i: (0, 0)),
                  pl.BlockSpec((rows, 1), lambda i: (i, 0)), pl.BlockSpec((1, LANES), lambda i: (0, 0)),
                  pl.BlockSpec((LW, D), lambda i: (COL_CQ // LW, 0), pipeline_mode=pl.Buffered(1))],
        out_specs=[pl.BlockSpec((rows, D), lambda i: (i, 0)), tab, tab,
                   pl.BlockSpec((rows, LW), lambda i: (i, 0))],
        out_shape=[jax.ShapeDtypeStruct((T, D), BF16), jax.ShapeDtypeStruct((T, LANES), F32),
                   jax.ShapeDtypeStruct((T, LANES), F32), jax.ShapeDtypeStruct((T, LW), BF16)],
        scratch_shapes=[pltpu.VMEM((LW, D), BF16)],
        compiler_params=_params(("arbitrary",)),
        name="norm_rope_latent",
    )(x, g, pos, inv, w_in_t)


def _ret_rope_tables(cs, sn):
    lo = lax.broadcasted_iota(jnp.int32, cs.shape, 1) < LANES // 2
    return (jnp.where(lo, cs, pltpu.roll(cs, LANES // 2, 1)),
            jnp.where(lo, -sn, pltpu.roll(sn, LANES // 2, 1)))


def _mla_rope_tables(cs, sn):
    lane = lax.broadcasted_iota(jnp.int32, cs.shape, 1)
    half = QK_ROPE // 2
    c = pltpu.roll(cs, LANES // 2, 1)
    s = pltpu.roll(sn, LANES // 2, 1)
    return (jnp.where(lane < QK_ROPE, c, 0.0), jnp.where(lane < half, -s, 0.0),
            jnp.where((lane >= half) & (lane < QK_ROPE), s, 0.0))


def _inproj_body(u_ref, wt_ref, o_ref, wb_ref):
    @pl.when(pl.program_id(1) == 0)
    def _cast():
        wb_ref[...] = wt_ref[...].astype(BF16)

    o_ref[...] = lax.dot_general(u_ref[...], wb_ref[...], (((1,), (1,)), ((), ())),
                                 preferred_element_type=F32).astype(BF16)


def _inproj(u, w_in_t, tm=2048, tn=1024):
    T, D = u.shape
    n_head = COL_CQ // tn
    n_gate = (w_in_t.shape[0] - COL_GRET) // tn

    def w_rows(j, i):
        row = jnp.where(j < n_head, j * tn, COL_GRET + (j - n_head) * tn)
        return (pl.multiple_of(row, QK_ROPE), 0)

    return pl.pallas_call(
        _inproj_body,
        grid=(n_head + n_gate, T // tm),
        in_specs=[
            pl.BlockSpec((tm, D), lambda j, i: (i, 0)),
            pl.BlockSpec((pl.Element(tn), pl.Element(D)), w_rows),
        ],
        out_specs=pl.BlockSpec((tm, tn), lambda j, i: (i, j)),
        out_shape=jax.ShapeDtypeStruct((T, (n_head + n_gate) * tn), BF16),
        scratch_shapes=[pltpu.VMEM((tn, D), BF16)],
        compiler_params=_params(("parallel", "arbitrary")),
        name="inproj",
    )(u, w_in_t)


def _mla_part(cq_ref, ckv_ref, u_ref, wpe_ref, gq_ref, gkv_ref, wq_ref, wk_ref, wv_ref, cs, sn,
              q_ref, kn_ref, kpe_ref, v_ref):
    qa = _rms(cq_ref[...].astype(F32), gq_ref[...]).astype(BF16)
    kva = _rms(ckv_ref[...].astype(F32), gkv_ref[...]).astype(BF16)
    c, s_lo, s_hi = _mla_rope_tables(cs, sn)
    kpe_raw = lax.dot_general(u_ref[...], wpe_ref[...].astype(BF16), (((1,), (1,)), ((), ())),
                              preferred_element_type=F32)
    kpe_ref[...] = _rope64(kpe_raw, c, s_lo, s_hi).astype(BF16)
    scale = MLA_QK_DIM ** -0.5 * LOG2_E
    cq = c * scale
    sq_lo = s_lo * scale
    sq_hi = s_hi * scale
    q = lax.dot_general(qa, wq_ref[...], (((1,), (1,)), ((), ())),
                        preferred_element_type=F32)
    for h in range(MLA_HEADS):
        b = h * MLA_HEAD_PAD
        q_ref[:, b:b + QK_NOPE] = (q[:, b:b + QK_NOPE] * scale).astype(BF16)
        q_ref[:, b + QK_NOPE:b + MLA_HEAD_PAD] = _rope64(
            q[:, b + QK_NOPE:b + MLA_HEAD_PAD], cq, sq_lo, sq_hi).astype(BF16)
    kn_ref[...] = jnp.dot(kva, wk_ref[...], preferred_element_type=F32).astype(BF16)
    v_ref[...] = jnp.dot(kva, wv_ref[...], preferred_element_type=F32).astype(BF16)


def _ret_tables_init(lg_ref, st_ref, dec_ref, qdec_ref, kdec_ref, bc):
    st_ref[...] = jnp.zeros_like(st_ref)
    n = lax.broadcasted_iota(jnp.int32, (bc, bc), 0)
    m = lax.broadcasted_iota(jnp.int32, (bc, bc), 1)
    dist = jnp.abs(n - m).astype(F32)
    visible = (m // CHUNK) <= (n // CHUNK)
    nl = lax.broadcasted_iota(jnp.int32, (bc, LANES), 0).astype(F32)
    for h in range(RET_HEADS):
        lg = lg_ref[h]
        dec_ref[h] = jnp.where(visible, jnp.exp(lg * dist), 0.0)
        qdec_ref[h] = jnp.exp(lg * (nl + 1.0))
        kdec_ref[h] = jnp.exp(lg * (bc - 1.0 - nl))


def _ret_block(rows, lg_ref, q_ref, k_ref, v_ref, rg_ref, cs, sn, g_ref, o_ref,
               st_ref, dec_ref, qdec_ref, kdec_ref, bc):
    c, s_ = _ret_rope_tables(cs, sn)
    k_scale = RET_QK_DIM ** -0.5
    ck = c * k_scale
    sk = s_ * k_scale
    for h in range(RET_HEADS):
        qs = slice(h * RET_QK_DIM, (h + 1) * RET_QK_DIM)
        vs = slice(h * RET_V_DIM, (h + 1) * RET_V_DIM)
        q32 = _rope128(q_ref[rows, qs].astype(F32), c, s_)
        k32 = _rope128(k_ref[rows, qs].astype(F32), ck, sk)
        q = q32.astype(BF16)
        k = k32.astype(BF16)
        v = v_ref[rows, vs]
        s = lax.dot_general(q, k, (((1,), (1,)), ((), ())), preferred_element_type=F32) * dec_ref[h]
        qd = (q32 * qdec_ref[h]).astype(BF16)
        st = st_ref[h]
        o = (jnp.dot(s.astype(BF16), v, preferred_element_type=F32)
             + jnp.dot(qd, st.astype(BF16), preferred_element_type=F32))
        kd = (k32 * kdec_ref[h]).astype(BF16)
        c_dec = jnp.exp(jnp.full((1, RET_V_DIM), lg_ref[h] * bc, F32))
        st_ref[h] = st * c_dec + lax.dot_general(
            kd, v, (((0,), (0,)), ((), ())), preferred_element_type=F32)
        mu = jnp.mean(o, axis=-1, keepdims=True)
        oc = o - mu
        var = jnp.mean(oc * oc, axis=-1, keepdims=True)
        y = oc * lax.rsqrt(var + EPS) * g_ref[:, vs]
        rg = rg_ref[rows, vs].astype(F32)
        o_ref[rows, vs] = (y * (rg * _sigmoid(rg))).astype(BF16)


def _branches_body(lg_ref, cq_ref, ckv_ref, u_ref, wpe_ref, gq_ref, gkv_ref, wq_ref, wk_ref, wv_ref,
                   cs_ref, sn_ref, rq_ref, rk_ref, rv_ref, rg_ref, gr_ref,
                   q_ref, kn_ref, kpe_ref, v_ref, ry_ref, st_ref, dec_ref, qdec_ref, kdec_ref,
                   *, tm, bc, tiles_per_seq):
    @pl.when(pl.program_id(0) % tiles_per_seq == 0)
    def _init():
        _ret_tables_init(lg_ref, st_ref, dec_ref, qdec_ref, kdec_ref, bc)

    _mla_part(cq_ref, ckv_ref, u_ref, wpe_ref, gq_ref, gkv_ref, wq_ref, wk_ref, wv_ref,
              cs_ref[...], sn_ref[...], q_ref, kn_ref, kpe_ref, v_ref)
    for blk in range(tm // bc):
        rows = slice(blk * bc, (blk + 1) * bc)
        _ret_block(rows, lg_ref, rq_ref, rk_ref, rv_ref, rg_ref, cs_ref[rows, :], sn_ref[rows, :],
                   gr_ref, ry_ref, st_ref, dec_ref, qdec_ref, kdec_ref, bc)


def _branches(proj, lat, u, w_in_t, gq, gkv, wq, wk, wv, cs, sn, lg, g_ret, seq, tm=512, bc=256):
    T, D = u.shape
    const = lambda shape: pl.BlockSpec(shape, lambda i, lg: (0, 0), pipeline_mode=pl.Buffered(1))
    rows = lambda w, col=0: pl.BlockSpec((tm, w), lambda i, lg: (i, col))
    QW = MLA_HEADS * MLA_HEAD_PAD
    KW = MLA_HEADS * QK_NOPE
    VW = MLA_HEADS * V_HEAD
    grid_spec = pltpu.PrefetchScalarGridSpec(
        num_scalar_prefetch=1,
        grid=(T // tm,),
        in_specs=[rows(Q_LORA, 0), rows(KV_LORA, 1), rows(D),
                  pl.BlockSpec((LANES, D), lambda i, lg: (COL_KPE // LANES, 0),
                               pipeline_mode=pl.Buffered(1)),
                  const((1, Q_LORA)), const((1, KV_LORA)),
                  const(wq.shape), const(wk.shape), const(wv.shape),
                  rows(LANES), rows(LANES),
                  rows(RET_QK_W, 0), rows(RET_QK_W, 1),
                  rows(RET_V_W, COL_RV // RET_V_W), rows(RET_V_W, COL_RG // RET_V_W),
                  const((1, RET_V_W))],
        out_specs=[rows(QW), rows(KW), rows(LANES), rows(VW), rows(RET_V_W)],
        scratch_shapes=[pltpu.VMEM((RET_HEADS, RET_QK_DIM, RET_V_DIM), F32),
                        pltpu.VMEM((RET_HEADS, bc, bc), F32),
                        pltpu.VMEM((RET_HEADS, bc, LANES), F32),
                        pltpu.VMEM((RET_HEADS, bc, LANES), F32)],
    )
    return pl.pallas_call(
        functools.partial(_branches_body, tm=tm, bc=bc, tiles_per_seq=seq // tm),
        grid_spec=grid_spec,
        out_shape=[jax.ShapeDtypeStruct((T, QW), BF16), jax.ShapeDtypeStruct((T, KW), BF16),
                   jax.ShapeDtypeStruct((T, LANES), BF16), jax.ShapeDtypeStruct((T, VW), BF16),
                   jax.ShapeDtypeStruct((T, RET_V_W), BF16)],
        compiler_params=_params(("arbitrary",), VMEM_LIMIT_BIG),
        name="mla_proj_retention",
    )(lg, lat, lat, u, w_in_t, gq, gkv, wq, wk, wv, cs, sn, proj, proj, proj, proj, g_ret)


def _attn_body(q_ref, kn_ref, kpe_ref, v_ref, wr_ref, wm_ref, wo_ref, o_ref, wrb_ref, wmb_ref, wob_ref,
               m_ref, acc_ref, *, tq, tk, tr, nh):
    wrb_ref[...] = wr_ref[...].astype(BF16)
    wmb_ref[...] = wm_ref[...].astype(BF16)
    wob_ref[...] = wo_ref[...].astype(BF16)
    qi = pl.program_id(2)
    m_ref[...] = jnp.full_like(m_ref, -jnp.inf)
    acc_ref[...] = jnp.zeros_like(acc_ref)
    reps = tk // LANES

    def step(start, diag_off):
        r0 = 0 if diag_off is None else diag_off
        for hh in range(nh):
            k = jnp.concatenate([kn_ref[pl.ds(start, tk), hh * QK_NOPE:(hh + 1) * QK_NOPE],
                                 kpe_ref[pl.ds(start, tk), :]], axis=1)
            v = v_ref[pl.ds(start, tk), hh * V_HEAD:(hh + 1) * V_HEAD]
            v1 = jnp.concatenate([v, jnp.ones_like(v)], axis=1)
            for ra in range(r0, tq, tr):
                rb = min(ra + tr, tq)
                q = q_ref[ra:rb, hh * MLA_HEAD_PAD:(hh + 1) * MLA_HEAD_PAD]
                s = lax.dot_general(q, k, (((1,), (1,)), ((), ())), preferred_element_type=F32)
                if diag_off is not None and ra - r0 < tk:
                    rq = (lax.broadcasted_iota(jnp.int32, (rb - ra, tk), 0) + (ra - r0)) // CHUNK
                    ck = lax.broadcasted_iota(jnp.int32, (rb - ra, tk), 1) // CHUNK
                    s = jnp.where(ck <= rq, s, -jnp.inf)
                m_old = m_ref[hh, ra:rb, :]
                m_new = jnp.maximum(m_old, jnp.max(s, axis=-1, keepdims=True))
                alpha = jnp.exp2(m_old - m_new)
                p = jnp.exp2(s - jnp.concatenate([m_new] * reps, axis=1))
                pv = jnp.dot(p.astype(BF16), v1, preferred_element_type=F32)
                acc_ref[hh, ra:rb, :] = jnp.concatenate([alpha, alpha], axis=1) * acc_ref[hh, ra:rb, :] + pv
                m_ref[hh, ra:rb, :] = m_new

    def full_steps(kb, carry):
        for j in range(tq // tk):
            step(pl.multiple_of(kb * tq + j * tk, tk), None)
        return carry

    lax.fori_loop(0, qi, full_steps, 0)
    for j in range(tq // tk):
        step(pl.multiple_of(qi * tq + j * tk, tk), j * tk)
    for hh in range(nh):
        o_ref[:, hh * V_HEAD:(hh + 1) * V_HEAD] = (
            acc_ref[hh, :, :V_HEAD] / acc_ref[hh, :, V_HEAD:]).astype(BF16)


def _attention(q, kn, kpe, v, w_ret_o, w_mla_o, w_out, batch, seq, tq=2048, tk=512, tr=256, nh=2):
    nq = seq // tq
    nhp = MLA_HEADS // nh
    steps = batch * nhp * nq
    K, N = w_ret_o.shape
    rs = K // steps
    assert K % steps == 0 and rs % 16 == 0 and w_mla_o.shape == (K, N) and w_out.shape == (K, N)
    wspec = pl.BlockSpec((rs, N), lambda b, h, i: ((b * nhp + h) * nq + i, 0))
    return pl.pallas_call(
        functools.partial(_attn_body, tq=tq, tk=tk, tr=tr, nh=nh),
        grid=(batch, nhp, nq),
        in_specs=[
            pl.BlockSpec((tq, nh * MLA_HEAD_PAD), lambda b, h, i: (b * nq + i, h)),
            pl.BlockSpec((seq, nh * QK_NOPE), lambda b, h, i: (b, h)),
            pl.BlockSpec((seq, LANES), lambda b, h, i: (b, 0)),
            pl.BlockSpec((seq, nh * V_HEAD), lambda b, h, i: (b, h)),
            wspec, wspec, wspec,
        ],
        out_specs=[pl.BlockSpec((tq, nh * V_HEAD), lambda b, h, i: (b * nq + i, h)), wspec, wspec, wspec],
        out_shape=[jax.ShapeDtypeStruct((batch * seq, MLA_HEADS * V_HEAD), BF16)]
        + [jax.ShapeDtypeStruct((K, N), BF16)] * 3,
        scratch_shapes=[pltpu.VMEM((nh, tq, LANES), F32), pltpu.VMEM((nh, tq, 2 * V_HEAD), F32)],
        compiler_params=_params(("parallel", "parallel", "arbitrary")),
        name="mla_attention",
    )(q, kn, kpe, v, w_ret_o, w_mla_o, w_out)


def _merge_out_body(a_ref, b_ref, wa_ref, wb_ref, ga_ref, gb_ref, wo_ref, x_ref, o_ref):
    ya = jnp.dot(a_ref[...], wa_ref[...], preferred_element_type=F32)
    yb = jnp.dot(b_ref[...], wb_ref[...], preferred_element_type=F32)
    ga = _sigmoid(ga_ref[...].astype(F32))
    gb = _sigmoid(gb_ref[...].astype(F32))
    merged = (ga * ya + gb * yb).astype(BF16)
    o_ref[...] = x_ref[...] + jnp.dot(merged, wo_ref[...], preferred_element_type=F32)


def _merge_out(ry, my, w_ret_o, w_mla_o, w_out, proj, gate_col, x, tm=256):
    T, K = ry.shape
    N = w_ret_o.shape[1]
    g0 = gate_col // N
    wspec = pl.BlockSpec((K, N), lambda i: (0, 0), pipeline_mode=pl.Buffered(1))
    rows = lambda col: pl.BlockSpec((tm, N), lambda i: (i, col))
    return pl.pallas_call(
        _merge_out_body,
        grid=(T // tm,),
        in_specs=[rows(0), rows(0), wspec, wspec, rows(g0), rows(g0 + 1), wspec, rows(0)],
        out_specs=rows(0),
        out_shape=jax.ShapeDtypeStruct((T, N), F32),
        compiler_params=_params(("parallel",), VMEM_LIMIT_BIG),
        name="merge_out_proj",
    )(ry, my, w_ret_o, w_mla_o, proj, proj, w_out, x)


def _mlp_body(h_ref, g_ref, wu_ref, wd_ref, gf_ref, o_ref, hn_ref):
    f = pl.program_id(1)

    @pl.when(f == 0)
    def _init():
        hn_ref[...] = _rms(h_ref[...], g_ref[...]).astype(BF16)
        o_ref[...] = jnp.zeros_like(o_ref)

    z = jnp.dot(hn_ref[...], wu_ref[...].astype(BF16), preferred_element_type=F32)
    a = jnp.square(jnp.maximum(z, 0.0)).astype(BF16)
    o_ref[...] += jnp.dot(a, wd_ref[...].astype(BF16), preferred_element_type=F32)

    @pl.when(f == pl.num_programs(1) - 1)
    def _finish():
        o_ref[...] = _rms(h_ref[...] + o_ref[...], gf_ref[...])


def _mlp(h, g, w_up, w_down, gf, tm=1024, tf=512):
    T, D = h.shape
    FF = w_up.shape[1]
    return pl.pallas_call(
        _mlp_body,
        grid=(T // tm, FF // tf),
        in_specs=[
            pl.BlockSpec((tm, D), lambda i, f: (i, 0)),
            pl.BlockSpec((1, D), lambda i, f: (0, 0)),
            pl.BlockSpec((D, tf), lambda i, f: (0, f)),
            pl.BlockSpec((tf, D), lambda i, f: (f, 0)),
            pl.BlockSpec((1, D), lambda i, f: (0, 0)),
        ],
        out_specs=pl.BlockSpec((tm, D), lambda i, f: (i, 0)),
        out_shape=jax.ShapeDtypeStruct((T, D), F32),
        scratch_shapes=[pltpu.VMEM((tm, D), BF16)],
        compiler_params=_params(("parallel", "arbitrary"), VMEM_LIMIT_BIG),
        name="mlp_final_norm",
    )(h, g, w_up, w_down, gf)


def _prep_q_weights(w_q_b):
    w = jnp.swapaxes(w_q_b, 0, 1).astype(BF16).reshape(MLA_HEADS, MLA_QK_DIM, Q_LORA)
    w = jnp.pad(w, ((0, 0), (0, MLA_HEAD_PAD - MLA_QK_DIM), (0, 0)))
    return w.reshape(MLA_HEADS * MLA_HEAD_PAD, Q_LORA)


def _prep_kv_weights(w_kv_b):
    w = w_kv_b.reshape(KV_LORA, MLA_HEADS, QK_NOPE + V_HEAD)
    wk = w[..., :QK_NOPE].reshape(KV_LORA, MLA_HEADS * QK_NOPE).astype(BF16)
    wv = w[..., QK_NOPE:].reshape(KV_LORA, MLA_HEADS * V_HEAD).astype(BF16)
    return wk, wv


def kernel(x, positions, norm_mix_g, w_in, ret_norm_g, w_ret_o, q_a_norm_g, w_q_b, kv_a_norm_g,
           w_kv_b, w_mla_o, w_out, norm_mlp_g, w_up, w_down, norm_f_g):
    B, S, D = x.shape
    T = B * S
    assert w_in.shape[0] == 1, "single-layer block: the final norm is fused into the MLP kernel"
    log_gamma = jnp.log(1.0 - 2.0 ** (-5.0 - jnp.arange(RET_HEADS, dtype=F32)))
    h = x.reshape(T, D)
    w_in_t = jnp.swapaxes(w_in[0], 0, 1)
    wq = _prep_q_weights(w_q_b[0])
    wk, wv = _prep_kv_weights(w_kv_b[0])
    u, cs, sn, lat = _norm_rope(h, norm_mix_g[0][None], positions.reshape(T, 1), w_in_t)
    proj = _inproj(u, w_in_t)
    q, kn, kpe, v, ry = _branches(proj, lat, u, w_in_t, q_a_norm_g[0][None], kv_a_norm_g[0][None],
                            wq, wk, wv, cs, sn, log_gamma, ret_norm_g[0][None], S)
    my, w_ret_b, w_mla_b, w_out_b = _attention(q, kn, kpe, v, w_ret_o[0], w_mla_o[0], w_out[0], B, S)
    h1 = _merge_out(ry, my, w_ret_b, w_mla_b, w_out_b, proj, COL_CQ, h)
    out = _mlp(h1, norm_mlp_g[0][None], w_up[0], w_down[0], norm_f_g[None])
    return out.reshape(B, S, D)
```
